```python
import jax
import jax.numpy as jnp
from jax import lax
import numpy as np

D_MODEL = 4096
BATCH = 2
SEQ = 8192
DEPTH = 2

PLE_DIM = 256
NORM_EPS = 1e-6
N_MIXERS = 3

ATT_HEADS = 16
ATT_KV_HEADS = 4
ATT_GROUP = ATT_HEADS // ATT_KV_HEADS
HEAD_DIM = 128
ATT_WIDTH = ATT_HEADS * HEAD_DIM
ROPE_THETA = 500000.0
ROPE_DIMS = HEAD_DIM // 4
CMP_LEN = 32
CMP_STRIDE = 16
SEL_LEN = 64
SEL_TOPN = 16
WINDOW = 512
NSA_QBLOCK = 64
N_NSA_BRANCHES = 3

GM_WIDTH = 2048
GM_GROUPS = 16
GM_GROUP_DIM = GM_WIDTH // GM_GROUPS
GM_CHUNK = 128

S5_WIDTH = 2048
S5_GROUP_DIM = 16
S5_GROUPS = S5_WIDTH // S5_GROUP_DIM
S5_STATE = 64
S5_CHUNK = 128
S5_DT_MIN = 0.001
S5_DT_MAX = 0.1

BRANCH_WIDTH = 2048

N_EXPERT_GROUPS = 4
EXPERTS_PER_GROUP = 8
N_EXPERTS = N_EXPERT_GROUPS * EXPERTS_PER_GROUP
TOP_K = 2
EXPERT_FF = 512
MOE_BLOCK = 128

Q_COLS = ATT_WIDTH
KV_COLS = N_NSA_BRANCHES * 2 * ATT_KV_HEADS * HEAD_DIM
NSA_GATE_COLS = ATT_HEADS * N_NSA_BRANCHES
GM_COLS = 2 * GM_WIDTH
S5_COLS = S5_WIDTH
MERGE_GATE_COLS = N_MIXERS * D_MODEL
IN_COLS = Q_COLS + KV_COLS + NSA_GATE_COLS + GM_COLS + S5_COLS + MERGE_GATE_COLS
IN_SPLITS = [Q_COLS,
             Q_COLS + KV_COLS,
             Q_COLS + KV_COLS + NSA_GATE_COLS,
             Q_COLS + KV_COLS + NSA_GATE_COLS + GM_COLS,
             Q_COLS + KV_COLS + NSA_GATE_COLS + GM_COLS + S5_COLS]

kernel_name = 'hybrid_gated_nsa_s5_gmlp_hmoe'


def rms_norm(x, g):
    xf = x.astype(jnp.float32)
    y = xf * lax.rsqrt(jnp.mean(xf * xf, axis=-1, keepdims=True) + NORM_EPS)
    return (y * g.astype(jnp.float32)).astype(x.dtype)


def rope_partial(x, pos):
    half = ROPE_DIMS // 2
    inv_freq = ROPE_THETA ** (-jnp.arange(half, dtype=jnp.float32) / half)
    ang = pos.astype(jnp.float32)[:, None] * inv_freq[None, :]
    cos = jnp.cos(ang)[:, None, :]
    sin = jnp.sin(ang)[:, None, :]
    xf = x.astype(jnp.float32)
    x1 = xf[..., :half]
    x2 = xf[..., half:ROPE_DIMS]
    out = jnp.concatenate([x1 * cos - x2 * sin, x1 * sin + x2 * cos, xf[..., ROPE_DIMS:]], axis=-1)
    return out.astype(x.dtype)


def masked_softmax(s, mask, axis):
    s = jnp.where(mask, s.astype(jnp.float32), -jnp.inf)
    m = jnp.max(s, axis=axis, keepdims=True)
    m = jnp.where(jnp.isfinite(m), m, 0.0)
    e = jnp.exp(s - m)
    return e / jnp.maximum(jnp.sum(e, axis=axis, keepdims=True), 1e-30)


def compress_tokens(t, pos_emb, w1, w2):
    b, l, nh, dh = t.shape
    ncmp = (l - CMP_LEN) // CMP_STRIDE + 1
    idx = jnp.arange(ncmp)[:, None] * CMP_STRIDE + jnp.arange(CMP_LEN)[None, :]
    blk = t[:, idx] + pos_emb[None, None, :, None, :]
    blk = jnp.moveaxis(blk, 3, 2).reshape(b, ncmp, nh, CMP_LEN * dh)
    return jax.nn.gelu(blk @ w1) @ w2


def nsa_mixer(q_raw, kv_raw, gate_raw, q_norm, k_norm, cmp_pos, cmp_w1, cmp_w2):
    b, l, _ = q_raw.shape
    dtype = q_raw.dtype
    pos = jnp.arange(l)
    scale = HEAD_DIM ** -0.5
    ncmp = (l - CMP_LEN) // CMP_STRIDE + 1
    nsel = l // SEL_LEN
    ntop = min(SEL_TOPN, nsel)
    nblk = l // NSA_QBLOCK

    q = rope_partial(rms_norm(q_raw.reshape(b, l, ATT_HEADS, HEAD_DIM), q_norm), pos)
    q = q.reshape(b, l, ATT_KV_HEADS, ATT_GROUP, HEAD_DIM).transpose(0, 2, 3, 1, 4)
    gates = jax.nn.sigmoid(gate_raw).reshape(b, l, ATT_KV_HEADS, ATT_GROUP, N_NSA_BRANCHES)
    gates = gates.transpose(0, 2, 3, 1, 4)
    kv = kv_raw.reshape(b, l, N_NSA_BRANCHES, 2, ATT_KV_HEADS, HEAD_DIM)

    cmp_end = jnp.arange(ncmp) * CMP_STRIDE + CMP_LEN - 1
    k_c = compress_tokens(kv[:, :, 0, 0], cmp_pos[0], cmp_w1[0], cmp_w2[0])
    k_c = rope_partial(rms_norm(k_c, k_norm[0]), cmp_end).transpose(0, 2, 1, 3)
    v_c = compress_tokens(kv[:, :, 0, 1], cmp_pos[1], cmp_w1[1], cmp_w2[1]).transpose(0, 2, 1, 3)

    k_s = rope_partial(rms_norm(kv[:, :, 1, 0], k_norm[1]), pos).transpose(0, 2, 1, 3)
    k_s = k_s.reshape(b, ATT_KV_HEADS, nsel, SEL_LEN, HEAD_DIM)
    v_s = kv[:, :, 1, 1].transpose(0, 2, 1, 3).reshape(b, ATT_KV_HEADS, nsel, SEL_LEN, HEAD_DIM)

    pad = ((0, 0), (0, 0), (WINDOW, 0), (0, 0))
    k_w = jnp.pad(rope_partial(rms_norm(kv[:, :, 2, 0], k_norm[2]), pos).transpose(0, 2, 1, 3), pad)
    v_w = jnp.pad(kv[:, :, 2, 1].transpose(0, 2, 1, 3), pad)

    cs = jnp.arange(ncmp)[:, None] * CMP_STRIDE
    ss = jnp.arange(nsel)[None, :] * SEL_LEN
    overlap = jnp.maximum(jnp.minimum(cs + CMP_LEN, ss + SEL_LEN) - jnp.maximum(cs, ss), 0)
    overlap = overlap.astype(jnp.float32) / CMP_STRIDE

    blk_ids = jnp.arange(nsel)
    b_ix = jnp.arange(b)[:, None, None, None]
    h_ix = jnp.arange(ATT_KV_HEADS)[None, :, None, None]
    win_off = jnp.arange(WINDOW + NSA_QBLOCK)
    sel_off = jnp.arange(SEL_LEN)

    def block(i):
        s0 = i * NSA_QBLOCK
        tq = s0 + jnp.arange(NSA_QBLOCK)
        qb = lax.dynamic_slice_in_dim(q, s0, NSA_QBLOCK, axis=3)
        gb = lax.dynamic_slice_in_dim(gates, s0, NSA_QBLOCK, axis=3)
        s_cmp = jnp.einsum('bhgqd,bhnd->bhgqn', qb, k_c) * scale
        p_cmp = masked_softmax(s_cmp, cmp_end[None, :] <= tq[:, None], -1)
        o_cmp = jnp.einsum('bhgqn,bhnd->bhgqd', p_cmp.astype(dtype), v_c)
        imp = jnp.einsum('bhgqn,ns->bhqs', p_cmp, overlap)
        cur = tq[:, None] // SEL_LEN
        forced = (blk_ids[None] == 0) | (blk_ids[None] == cur) | (blk_ids[None] == cur - 1)
        imp = jnp.where(forced, 1e9, imp)
        imp = jnp.where(blk_ids[None] <= cur, imp, -1e9)
        _, sel = lax.top_k(imp, ntop)
        ks = k_s[b_ix, h_ix, sel]
        vs = v_s[b_ix, h_ix, sel]
        s_sel = jnp.einsum('bhgqd,bhqnkd->bhgqnk', qb, ks) * scale
        kpos = sel[..., None] * SEL_LEN + sel_off
        m_sel = (kpos <= tq[:, None, None])[:, :, None]
        p_sel = masked_softmax(s_sel, m_sel, (-2, -1))
        o_sel = jnp.einsum('bhgqnk,bhqnkd->bhgqd', p_sel.astype(dtype), vs)
        kw = lax.dynamic_slice_in_dim(k_w, s0, WINDOW + NSA_QBLOCK, axis=2)
        vw = lax.dynamic_slice_in_dim(v_w, s0, WINDOW + NSA_QBLOCK, axis=2)
        kp = s0 - WINDOW + win_off
        m_win = (kp[None] <= tq[:, None]) & (kp[None] > tq[:, None] - WINDOW) & (kp[None] >= 0)
        s_win = jnp.einsum('bhgqd,bhkd->bhgqk', qb, kw) * scale
        p_win = masked_softmax(s_win, m_win, -1)
        o_win = jnp.einsum('bhgqk,bhkd->bhgqd', p_win.astype(dtype), vw)
        return gb[..., 0:1] * o_cmp + gb[..., 1:2] * o_sel + gb[..., 2:3] * o_win

    out = lax.map(block, jnp.arange(nblk))
    return out.transpose(1, 0, 4, 2, 3, 5).reshape(b, l, ATT_WIDTH)


def gmlp_mixer(z, gm_norm, ws, bs):
    b, l, _ = z.shape
    z = jax.nn.gelu(z)
    u, v = jnp.split(z, 2, axis=-1)
    v = rms_norm(v, gm_norm).reshape(b, l // GM_CHUNK, GM_CHUNK, GM_GROUPS, GM_GROUP_DIM)
    causal = jnp.tril(jnp.ones((GM_CHUNK, GM_CHUNK), dtype=bool))
    w = jnp.where(causal[None], ws, 0.0).astype(v.dtype)
    s = jnp.einsum('gts,bnsgc->bntgc', w, v) + bs.T[None, None, :, :, None]
    return u * s.reshape(b, l, GM_WIDTH)


def s5_mixer(u, lam_re, lam_im, log_dt, b_re, b_im, c_re, c_im, d_skip, w_glu, b_glu):
    dtype = u.dtype
    b, l, _ = u.shape
    f32 = jnp.float32
    uf = u.astype(f32).reshape(b, l // S5_CHUNK, S5_CHUNK, S5_GROUPS, S5_GROUP_DIM)
    uf = jnp.moveaxis(uf, 1, 0)
    dt = jnp.exp(log_dt.astype(f32))[:, None]
    lre = jnp.minimum(lam_re.astype(f32), -1e-4)
    lim = lam_im.astype(f32)
    mag = jnp.exp(lre * dt)
    a_re = mag * jnp.cos(lim * dt)
    a_im = mag * jnp.sin(lim * dt)
    den = lre * lre + lim * lim
    nr = a_re - 1.0
    f_re = (nr * lre + a_im * lim) / den
    f_im = (a_im * lre - nr * lim) / den
    br = b_re.astype(f32)
    bi = b_im.astype(f32)
    bb_re = f_re[..., None] * br - f_im[..., None] * bi
    bb_im = f_re[..., None] * bi + f_im[..., None] * br
    cr = c_re.astype(f32)
    ci = c_im.astype(f32)

    def combine(e1, e2):
        a1r, a1i, b1r, b1i = e1
        a2r, a2i, b2r, b2i = e2
        return (a2r * a1r - a2i * a1i,
                a2r * a1i + a2i * a1r,
                a2r * b1r - a2i * b1i + b2r,
                a2r * b1i + a2i * b1r + b2i)

    def chunk_step(carry, uc):
        hr, hi = carry
        bur = jnp.einsum('blgc,gpc->blgp', uc, bb_re)
        bui = jnp.einsum('blgc,gpc->blgp', uc, bb_im)
        ar = jnp.broadcast_to(a_re, bur.shape)
        ai = jnp.broadcast_to(a_im, bur.shape)
        pr, pi_, sr, si = lax.associative_scan(combine, (ar, ai, bur, bui), axis=1)
        xr = pr * hr[:, None] - pi_ * hi[:, None] + sr
        xi = pr * hi[:, None] + pi_ * hr[:, None] + si
        y = jnp.einsum('blgp,gcp->blgc', xr, cr) - jnp.einsum('blgp,gcp->blgc', xi, ci)
        return (xr[:, -1], xi[:, -1]), y

    h0 = jnp.zeros((b, S5_GROUPS, S5_STATE), f32)
    _, ys = lax.scan(chunk_step, (h0, h0), uf)
    y = jnp.moveaxis(ys, 0, 1).reshape(b, l, S5_WIDTH) + d_skip.astype(f32) * u.astype(f32)
    y = jax.nn.gelu(y).astype(dtype)
    return y * jax.nn.sigmoid(y @ w_glu + b_glu)


def moe_ffn(h, rg_w, rg_b, re_w, re_b, w1, w3, w2):
    b, l, d = h.shape
    n = b * l
    f32 = jnp.float32
    t = h.reshape(n, d)
    g_prob = jax.nn.softmax((t @ rg_w).astype(f32) + rg_b.astype(f32), axis=-1)
    g_p, g_idx = lax.top_k(g_prob, 1)
    e_logits = ((t @ re_w).astype(f32) + re_b.astype(f32)).reshape(n, N_EXPERT_GROUPS, EXPERTS_PER_GROUP)
    e_logits = e_logits[jnp.arange(n), g_idx[:, 0]]
    e_top, e_loc = lax.top_k(e_logits, TOP_K)
    e_wt = jax.nn.softmax(e_top, axis=-1) * g_p
    e_id = g_idx * EXPERTS_PER_GROUP + e_loc

    n_assign = n * TOP_K
    flat_e = e_id.reshape(n_assign)
    flat_tok = jnp.repeat(jnp.arange(n), TOP_K)
    flat_w = e_wt.reshape(n_assign)
    order = jnp.argsort(flat_e)
    se = flat_e[order]
    counts = jnp.bincount(flat_e, length=N_EXPERTS)
    padded = (counts + MOE_BLOCK - 1) // MOE_BLOCK * MOE_BLOCK
    pend = jnp.cumsum(padded)
    pstart = pend - padded
    cstart = jnp.cumsum(counts) - counts
    dest = pstart[se] + (jnp.arange(n_assign) - cstart[se])
    n_rows = (n_assign + N_EXPERTS * (MOE_BLOCK - 1) + MOE_BLOCK - 1) // MOE_BLOCK * MOE_BLOCK
    n_blocks = n_rows // MOE_BLOCK
    row_tok = jnp.zeros((n_rows,), jnp.int32).at[dest].set(flat_tok[order].astype(jnp.int32))
    row_w = jnp.zeros((n_rows,), f32).at[dest].set(flat_w[order])
    blk_e = jnp.minimum(jnp.searchsorted(pend, jnp.arange(n_blocks) * MOE_BLOCK, side='right'), N_EXPERTS - 1)

    def body(acc, inp):
        tok, wt, e = inp
        xb = t[tok]
        hb = jax.nn.silu(xb @ w1[e]) * (xb @ w3[e])
        yb = (hb @ w2[e]) * wt[:, None].astype(hb.dtype)
        return acc.at[tok].add(yb.astype(acc.dtype)), None

    acc0 = jnp.zeros((n, d), h.dtype)
    out, _ = lax.scan(body, acc0, (row_tok.reshape(n_blocks, MOE_BLOCK), row_w.reshape(n_blocks, MOE_BLOCK), blk_e))
    return out.reshape(b, l, d)


def setup_inputs(seed: int = 0) -> dict:
    key = jax.random.key(seed)
    ks = iter(jax.random.split(key, 40))
    f32 = jnp.float32

    def nrm(shape, scale):
        return jax.random.normal(next(ks), shape, f32) * scale

    def gain(shape):
        return 1.0 + 0.02 * jax.random.normal(next(ks), shape, f32)

    hd = HEAD_DIM
    return {
        'x': nrm((BATCH, SEQ, D_MODEL), 1.0),
        'p': nrm((DEPTH, BATCH, SEQ, PLE_DIM), 1.0),
        'norm1': gain((DEPTH, D_MODEL)),
        'w_in': nrm((DEPTH, D_MODEL, IN_COLS), D_MODEL ** -0.5),
        'q_norm': gain((DEPTH, hd)),
        'k_norm': gain((DEPTH, N_NSA_BRANCHES, hd)),
        'cmp_pos': nrm((DEPTH, 2, CMP_LEN, hd), 0.1),
        'cmp_w1': nrm((DEPTH, 2, CMP_LEN * hd, hd), (CMP_LEN * hd) ** -0.5),
        'cmp_w2': nrm((DEPTH, 2, hd, hd), hd ** -0.5),
        'gm_norm': gain((DEPTH, GM_WIDTH)),
        'gm_ws': nrm((DEPTH, GM_GROUPS, GM_CHUNK, GM_CHUNK), GM_CHUNK ** -0.5),
        'gm_bs': 1.0 + nrm((DEPTH, GM_GROUPS, GM_CHUNK), 0.01),
        's5_lambda_re': -0.5 + nrm((DEPTH, S5_GROUPS, S5_STATE), 0.01),
        's5_lambda_im': jnp.pi * jnp.arange(S5_STATE, dtype=f32)[None, None, :] + nrm((DEPTH, S5_GROUPS, S5_STATE), 0.01),
        's5_log_dt': jax.random.uniform(next(ks), (DEPTH, S5_GROUPS), f32, minval=float(np.log(S5_DT_MIN)), maxval=float(np.log(S5_DT_MAX))),
        's5_b_re': nrm((DEPTH, S5_GROUPS, S5_STATE, S5_GROUP_DIM), (2 * S5_GROUP_DIM) ** -0.5),
        's5_b_im': nrm((DEPTH, S5_GROUPS, S5_STATE, S5_GROUP_DIM), (2 * S5_GROUP_DIM) ** -0.5),
        's5_c_re': nrm((DEPTH, S5_GROUPS, S5_GROUP_DIM, S5_STATE), S5_STATE ** -0.5),
        's5_c_im': nrm((DEPTH, S5_GROUPS, S5_GROUP_DIM, S5_STATE), S5_STATE ** -0.5),
        's5_d': nrm((DEPTH, S5_WIDTH), 1.0),
        's5_w_glu': nrm((DEPTH, S5_WIDTH, S5_WIDTH), S5_WIDTH ** -0.5),
        's5_b_glu': nrm((DEPTH, S5_WIDTH), 0.01),
        'w_branch': nrm((DEPTH, N_MIXERS, BRANCH_WIDTH, D_MODEL), BRANCH_WIDTH ** -0.5),
        'w_out': nrm((DEPTH, D_MODEL, D_MODEL), D_MODEL ** -0.5),
        'norm2': gain((DEPTH, D_MODEL)),
        'router_group_w': nrm((DEPTH, D_MODEL, N_EXPERT_GROUPS), D_MODEL ** -0.5),
        'router_group_b': nrm((DEPTH, N_EXPERT_GROUPS), 0.01),
        'router_expert_w': nrm((DEPTH, D_MODEL, N_EXPERTS), D_MODEL ** -0.5),
        'router_expert_b': nrm((DEPTH, N_EXPERTS), 0.01),
        'expert_w1': nrm((DEPTH, N_EXPERTS, D_MODEL, EXPERT_FF), D_MODEL ** -0.5),
        'expert_w3': nrm((DEPTH, N_EXPERTS, D_MODEL, EXPERT_FF), D_MODEL ** -0.5),
        'expert_w2': nrm((DEPTH, N_EXPERTS, EXPERT_FF, D_MODEL), EXPERT_FF ** -0.5),
        'norm3': gain((DEPTH, D_MODEL)),
        'w_ple': nrm((DEPTH, PLE_DIM, D_MODEL), PLE_DIM ** -0.5),
        'w_ple_gate': nrm((DEPTH, D_MODEL, D_MODEL), D_MODEL ** -0.5),
    }


def reference(x, p, norm1, w_in, q_norm, k_norm, cmp_pos, cmp_w1, cmp_w2, gm_norm, gm_ws, gm_bs,
              s5_lambda_re, s5_lambda_im, s5_log_dt, s5_b_re, s5_b_im, s5_c_re, s5_c_im, s5_d,
              s5_w_glu, s5_b_glu, w_branch, w_out, norm2, router_group_w, router_group_b,
              router_expert_w, router_expert_b, expert_w1, expert_w3, expert_w2, norm3, w_ple,
              w_ple_gate):
    b, l, d = x.shape
    for i in range(DEPTH):
        h = rms_norm(x, norm1[i])
        q_raw, kv_raw, ng_raw, gm_raw, s5_raw, mg_raw = jnp.split(h @ w_in[i], IN_SPLITS, axis=-1)
        o_gm = gmlp_mixer(gm_raw, gm_norm[i], gm_ws[i], gm_bs[i])
        o_s5 = s5_mixer(s5_raw, s5_lambda_re[i], s5_lambda_im[i], s5_log_dt[i], s5_b_re[i], s5_b_im[i],
                        s5_c_re[i], s5_c_im[i], s5_d[i], s5_w_glu[i], s5_b_glu[i])
        o_att = nsa_mixer(q_raw, kv_raw, ng_raw, q_norm[i], k_norm[i], cmp_pos[i], cmp_w1[i], cmp_w2[i])
        g = jax.nn.sigmoid(mg_raw).reshape(b, l, N_MIXERS, d)
        merged = (g[:, :, 0] * (o_gm @ w_branch[i, 0])
                  + g[:, :, 1] * (o_s5 @ w_branch[i, 1])
                  + g[:, :, 2] * (o_att @ w_branch[i, 2]))
        x = x + merged @ w_out[i]
        x = x + moe_ffn(rms_norm(x, norm2[i]), router_group_w[i], router_group_b[i], router_expert_w[i],
                        router_expert_b[i], expert_w1[i], expert_w3[i], expert_w2[i])
        x = x + (p[i] @ w_ple[i]) * jax.nn.sigmoid(rms_norm(x, norm3[i]) @ w_ple_gate[i])
    return x
```

```python
import functools

import jax
import jax.numpy as jnp
from jax import lax
from jax.experimental import pallas as pl
from jax.experimental.pallas import tpu as pltpu

F32 = jnp.float32
BF16 = jnp.bfloat16

NORM_EPS = 1e-6
LANES = 128
VMEM_LIMIT = 52 * 1024 * 1024

HEADS = 16
KV_HEADS = 4
GROUP = HEADS // KV_HEADS
HEAD_DIM = 128
ROPE_THETA = 500000.0
ROPE_DIMS = HEAD_DIM // 4
ROPE_HALF = ROPE_DIMS // 2
CMP_LEN = 32
CMP_STRIDE = 16
SEL_LEN = 64
SEL_TOPN = 16
WINDOW = 512
N_BRANCH = 3
NEG = -1e30
ATT_SCALE = HEAD_DIM ** -0.5
ATT_TQ = 64
ATT_TK = 512

GM_WIDTH = 2048
GM_GROUPS = 16
GM_CHUNK = 128

S5_WIDTH = 2048
S5_GROUP_DIM = 16
S5_GROUPS = S5_WIDTH // S5_GROUP_DIM
S5_STATE = 64
S5_TILE_GROUPS = LANES // S5_GROUP_DIM
S5_TILE_STATES = S5_TILE_GROUPS * S5_STATE
S5_T = 128
S5_LEVELS = 7

N_GROUPS = 4
PER_GROUP = 8
N_EXPERTS = N_GROUPS * PER_GROUP
TOP_K = 2
MOE_ROWS = 256
RANK_T = 512
MOE_TT = 64


def _cparams(sem):
    return pltpu.CompilerParams(dimension_semantics=sem, vmem_limit_bytes=VMEM_LIMIT)


def _dot(a, b):
    return jnp.dot(a, b, preferred_element_type=F32)


def _dot_nt(a, b):
    return lax.dot_general(a, b, (((1,), (1,)), ((), ())), preferred_element_type=F32)


def _rms_body(x_ref, g_ref, o_ref):
    x = x_ref[...]
    y = x * lax.rsqrt(jnp.mean(x * x, axis=-1, keepdims=True) + NORM_EPS)
    o_ref[...] = (y * g_ref[...]).astype(o_ref.dtype)


def rms_cast(x, g, tm=256):
    m, d = x.shape
    return pl.pallas_call(
        _rms_body,
        out_shape=jax.ShapeDtypeStruct((m, d), BF16),
        grid=(m // tm,),
        in_specs=[pl.BlockSpec((tm, d), lambda i: (i, 0)),
                  pl.BlockSpec((1, d), lambda i: (0, 0))],
        out_specs=pl.BlockSpec((tm, d), lambda i: (i, 0)),
        compiler_params=_cparams(("parallel",)),
        name="rms_cast",
    )(x, g.reshape(1, d))


def _mm_plain_body(a_ref, w_ref, o_ref):
    o_ref[...] = _dot(a_ref[...], w_ref[...]).astype(o_ref.dtype)


def _mm_glu_body(a_ref, w_ref, y_ref, b_ref, o_ref):
    acc = _dot(a_ref[...].astype(BF16), w_ref[...]) + b_ref[...]
    o_ref[...] = (y_ref[...] * jax.nn.sigmoid(acc)).astype(o_ref.dtype)


def _mm_res_body(a_ref, w_ref, x_ref, o_ref):
    o_ref[...] = x_ref[...] + _dot(a_ref[...], w_ref[...])


def _mm_merge_body(a0, a1, a2, w0, w1, w2, g0, g1, g2, o_ref):
    acc = jax.nn.sigmoid(g0[...]) * _dot(a0[...], w0[...])
    acc = acc + jax.nn.sigmoid(g1[...]) * _dot(a1[...], w1[...])
    acc = acc + jax.nn.sigmoid(g2[...]) * _dot(a2[...], w2[...])
    o_ref[...] = acc.astype(o_ref.dtype)


def _mm_ple_body(a_ref, w_ref, p_ref, wp_ref, x_ref, o_ref):
    gate = jax.nn.sigmoid(_dot(a_ref[...], w_ref[...]))
    o_ref[...] = x_ref[...] + _dot(p_ref[...], wp_ref[...]) * gate


def _lhs_spec(tm, k):
    return pl.BlockSpec((tm, k), lambda j, i: (i, 0))


def _rhs_spec(k, tn):
    return pl.BlockSpec((k, tn), lambda j, i: (0, j))


def _tile_spec(tm, tn, col_blocks=0):
    return pl.BlockSpec((tm, tn), lambda j, i: (i, j + col_blocks))


def _mm_call(body, args, in_specs, m, n, tm, tn, out_dtype, name):
    return pl.pallas_call(
        body,
        out_shape=jax.ShapeDtypeStruct((m, n), out_dtype),
        grid=(n // tn, m // tm),
        in_specs=in_specs,
        out_specs=pl.BlockSpec((tm, tn), lambda j, i: (i, j)),
        compiler_params=_cparams(("parallel", "parallel")),
        name=name,
    )(*args)


def _pick(n, prefs):
    for t in prefs:
        if n % t == 0:
            return t
    return n


def mm_plain(a, w, out_dtype=F32, name="mm"):
    m, k = a.shape
    n = w.shape[1]
    tm, tn = _pick(m, (512, 256, 128)), _pick(n, (1024, 512, 256, 128))
    return _mm_call(_mm_plain_body, (a, w), [_lhs_spec(tm, k), _rhs_spec(k, tn)],
                    m, n, tm, tn, out_dtype, name)


def mm_glu(y, w, b):
    m, k = y.shape
    n = w.shape[1]
    tm, tn = _pick(m, (512, 256, 128)), _pick(n, (1024, 512, 256, 128))
    specs = [_lhs_spec(tm, k), _rhs_spec(k, tn), _tile_spec(tm, tn),
             pl.BlockSpec((1, tn), lambda j, i: (0, j))]
    return _mm_call(_mm_glu_body, (y, w, y, b.reshape(1, n)), specs, m, n, tm, tn, BF16, "mm_glu")


def mm_residual(a, w, x):
    m, k = a.shape
    n = w.shape[1]
    tm, tn = _pick(m, (512, 256, 128)), _pick(n, (1024, 512, 256, 128))
    specs = [_lhs_spec(tm, k), _rhs_spec(k, tn), _tile_spec(tm, tn)]
    return _mm_call(_mm_res_body, (a, w, x), specs, m, n, tm, tn, F32, "mm_residual")


def mm_merge(o_list, w_list, mg_raw):
    m, k = o_list[0].shape
    n = w_list[0].shape[1]
    tm, tn = _pick(m, (512, 256, 128)), _pick(n, (512, 256, 128))
    specs = ([_lhs_spec(tm, k)] * 3 + [_rhs_spec(k, tn)] * 3
             + [_tile_spec(tm, tn, c * (n // tn)) for c in range(3)])
    return _mm_call(_mm_merge_body, (*o_list, *w_list, mg_raw, mg_raw, mg_raw), specs,
                    m, n, tm, tn, BF16, "mm_merge")


def mm_ple(h, wg, p, wp, x):
    m, k = h.shape
    n = wg.shape[1]
    kp = p.shape[1]
    tm, tn = _pick(m, (512, 256, 128)), _pick(n, (1024, 512, 256, 128))
    specs = [_lhs_spec(tm, k), _rhs_spec(k, tn), _lhs_spec(tm, kp), _rhs_spec(kp, tn),
             _tile_spec(tm, tn)]
    return _mm_call(_mm_ple_body, (h, wg, p, wp, x), specs, m, n, tm, tn, F32, "mm_ple")


def _gmlp_body(z_ref, gn_ref, w_ref, b_ref, o_ref, *, rows):
    z = jax.nn.gelu(z_ref[...])
    u = z[:, :GM_WIDTH]
    v = z[:, GM_WIDTH:]
    v = v * lax.rsqrt(jnp.mean(v * v, axis=-1, keepdims=True) + NORM_EPS) * gn_ref[...]
    vb = v.astype(BF16)
    for c in range(rows // GM_CHUNK):
        r0 = c * GM_CHUNK
        for g in range(GM_GROUPS):
            c0 = g * LANES
            s = _dot(w_ref[g], vb[r0:r0 + GM_CHUNK, c0:c0 + LANES]) + b_ref[:, c0:c0 + LANES]
            o_ref[r0:r0 + GM_CHUNK, c0:c0 + LANES] = (
                u[r0:r0 + GM_CHUNK, c0:c0 + LANES] * s).astype(o_ref.dtype)


def gmlp_mixer(z, gm_norm, ws, bs, rows=256):
    m = z.shape[0]
    tri = jnp.tril(jnp.ones((GM_CHUNK, GM_CHUNK), dtype=bool))
    w = jnp.where(tri[None], ws, 0.0).astype(BF16)
    bias = jnp.repeat(bs.T, LANES, axis=1)
    return pl.pallas_call(
        functools.partial(_gmlp_body, rows=rows),
        out_shape=jax.ShapeDtypeStruct((m, GM_WIDTH), BF16),
        grid=(m // rows,),
        in_specs=[pl.BlockSpec((rows, 2 * GM_WIDTH), lambda i: (i, 0)),
                  pl.BlockSpec((1, GM_WIDTH), lambda i: (0, 0)),
                  pl.BlockSpec((GM_GROUPS, GM_CHUNK, GM_CHUNK), lambda i: (0, 0, 0)),
                  pl.BlockSpec((GM_CHUNK, GM_WIDTH), lambda i: (0, 0))],
        out_specs=pl.BlockSpec((rows, GM_WIDTH), lambda i: (i, 0)),
        compiler_params=_cparams(("parallel",)),
        name="gmlp",
    )(z, gm_norm.reshape(1, GM_WIDTH), w, bias)


def _shift_rows(x, s):
    t = x.shape[0]
    if s % 8 == 0:
        return jnp.concatenate([jnp.zeros((s, x.shape[1]), x.dtype), x[:t - s]], axis=0)
    rolled = pltpu.roll(x, s, 0)
    row = lax.broadcasted_iota(jnp.int32, x.shape, 0)
    return jnp.where(row >= s, rolled, 0.0)


def _s5_body(u_ref, bre_ref, bim_ref, cre_ref, cim_ref, are_ref, aim_ref, d_ref, o_ref,
             hre_ref, him_ref):
    @pl.when(pl.program_id(2) == 0)
    def _():
        hre_ref[...] = jnp.zeros_like(hre_ref)
        him_ref[...] = jnp.zeros_like(him_ref)

    u = u_ref[...]
    ub = u.astype(BF16)
    xr = _dot(ub, bre_ref[0])
    xi = _dot(ub, bim_ref[0])
    are = are_ref[0]
    aim = aim_ref[0]
    hr = hre_ref[0:1, :]
    hi = him_ref[0:1, :]
    a_r = are[0:1, :]
    a_i = aim[0:1, :]
    row = lax.broadcasted_iota(jnp.int32, xr.shape, 0)
    first = row == 0
    xr = xr + jnp.where(first, a_r * hr - a_i * hi, 0.0)
    xi = xi + jnp.where(first, a_r * hi + a_i * hr, 0.0)
    for k in range(S5_LEVELS):
        s = 1 << k
        cr = are[k:k + 1, :]
        ci = aim[k:k + 1, :]
        sr = _shift_rows(xr, s)
        si = _shift_rows(xi, s)
        xr, xi = xr + cr * sr - ci * si, xi + cr * si + ci * sr
    t = xr.shape[0]
    hre_ref[0:1, :] = xr[t - 1:t, :]
    him_ref[0:1, :] = xi[t - 1:t, :]
    y = _dot(xr.astype(BF16), cre_ref[0]) - _dot(xi.astype(BF16), cim_ref[0])
    y = y + d_ref[...] * u
    o_ref[...] = jax.nn.gelu(y)


def _s5_discretize(lam_re, lam_im, log_dt, b_re, b_im):
    dt = jnp.exp(log_dt.astype(F32))[:, None]
    lre = jnp.minimum(lam_re.astype(F32), -1e-4)
    lim = lam_im.astype(F32)
    mag = jnp.exp(lre * dt)
    a_re = mag * jnp.cos(lim * dt)
    a_im = mag * jnp.sin(lim * dt)
    den = lre * lre + lim * lim
    nr = a_re - 1.0
    f_re = (nr * lre + a_im * lim) / den
    f_im = (a_im * lre - nr * lim) / den
    br = b_re.astype(F32)
    bi = b_im.astype(F32)
    bb_re = f_re[..., None] * br - f_im[..., None] * bi
    bb_im = f_re[..., None] * bi + f_im[..., None] * br
    return a_re, a_im, bb_re, bb_im


def _block_diag(x):
    nt, gl, r, c = x.shape
    eye = jnp.eye(gl, dtype=x.dtype)
    out = x[:, :, :, None, :] * eye[None, :, None, :, None]
    return out.reshape(nt, gl * r, gl * c)


def s5_mixer(u, batch, lam_re, lam_im, log_dt, b_re, b_im, c_re, c_im, d_skip, w_glu, b_glu):
    m = u.shape[0]
    seq = m // batch
    nt = S5_GROUPS // S5_TILE_GROUPS
    a_re, a_im, bb_re, bb_im = _s5_discretize(lam_re, lam_im, log_dt, b_re, b_im)
    tile = lambda x: x.reshape(nt, S5_TILE_GROUPS, *x.shape[1:])
    bblk_re = _block_diag(jnp.swapaxes(tile(bb_re), 2, 3)).astype(BF16)
    bblk_im = _block_diag(jnp.swapaxes(tile(bb_im), 2, 3)).astype(BF16)
    cblk_re = _block_diag(jnp.swapaxes(tile(c_re.astype(F32)), 2, 3)).astype(BF16)
    cblk_im = _block_diag(jnp.swapaxes(tile(c_im.astype(F32)), 2, 3)).astype(BF16)
    pr, pi = [a_re.reshape(-1)], [a_im.reshape(-1)]
    for _ in range(7):
        r, i = pr[-1], pi[-1]
        pr.append(r * r - i * i)
        pi.append(2.0 * r * i)
    apow_re = jnp.stack(pr, 0).reshape(8, nt, S5_TILE_STATES).transpose(1, 0, 2)
    apow_im = jnp.stack(pi, 0).reshape(8, nt, S5_TILE_STATES).transpose(1, 0, 2)

    nchunk = seq // S5_T
    wspec = lambda r, c: pl.BlockSpec((1, r, c), lambda b, j, t: (j, 0, 0))
    y = pl.pallas_call(
        _s5_body,
        out_shape=jax.ShapeDtypeStruct((m, S5_WIDTH), F32),
        grid=(batch, nt, nchunk),
        in_specs=[pl.BlockSpec((S5_T, LANES), lambda b, j, t: (b * nchunk + t, j)),
                  wspec(LANES, S5_TILE_STATES), wspec(LANES, S5_TILE_STATES),
                  wspec(S5_TILE_STATES, LANES), wspec(S5_TILE_STATES, LANES),
                  wspec(8, S5_TILE_STATES), wspec(8, S5_TILE_STATES),
                  pl.BlockSpec((1, LANES), lambda b, j, t: (0, j))],
        out_specs=pl.BlockSpec((S5_T, LANES), lambda b, j, t: (b * nchunk + t, j)),
        scratch_shapes=[pltpu.VMEM((8, S5_TILE_STATES), F32), pltpu.VMEM((8, S5_TILE_STATES), F32)],
        compiler_params=_cparams(("parallel", "parallel", "arbitrary")),
        name="s5_scan",
    )(u, bblk_re, bblk_im, cblk_re, cblk_im, apow_re, apow_im, d_skip.reshape(1, S5_WIDTH))
    return mm_glu(y, w_glu.astype(BF16), b_glu)


def _rope_tables(pos):
    inv_freq = ROPE_THETA ** (-jnp.arange(ROPE_HALF, dtype=F32) / ROPE_HALF)
    ang = pos.astype(F32)[:, None] * inv_freq[None, :]
    cos, sin = jnp.cos(ang), jnp.sin(ang)
    n = pos.shape[0]
    rest = HEAD_DIM - ROPE_DIMS
    c = jnp.concatenate([cos, cos, jnp.ones((n, rest), F32)], axis=1)
    s1 = jnp.concatenate([-sin, jnp.zeros((n, HEAD_DIM - ROPE_HALF), F32)], axis=1)
    s2 = jnp.concatenate([jnp.zeros((n, ROPE_HALF), F32), sin, jnp.zeros((n, rest), F32)], axis=1)
    return c, s1, s2


def _norm_rope(x, gain, c, s1, s2):
    x = x * lax.rsqrt(jnp.mean(x * x, axis=-1, keepdims=True) + NORM_EPS) * gain
    return (x * c + pltpu.roll(x, HEAD_DIM - ROPE_HALF, 1) * s1 + pltpu.roll(x, ROPE_HALF, 1) * s2)


def _kv_prep_body(ks_ref, vs_ref, kw_ref, vw_ref, kn_ref, c_ref, s1_ref, s2_ref,
                  oks_ref, ovs_ref, okw_ref, ovw_ref):
    c, s1, s2 = c_ref[...], s1_ref[...], s2_ref[...]
    oks_ref[...] = _norm_rope(ks_ref[...], kn_ref[1:2, :], c, s1, s2).astype(BF16)
    okw_ref[...] = _norm_rope(kw_ref[...], kn_ref[2:3, :], c, s1, s2).astype(BF16)
    ovs_ref[...] = vs_ref[...].astype(BF16)
    ovw_ref[...] = vw_ref[...].astype(BF16)


def _cmp_prep_body(k_ref, v_ref, pos_ref, w1_ref, w2_ref, kn_ref, c_ref, s1_ref, s2_ref,
                   ok_ref, ov_ref, *, ncp):
    half = CMP_LEN // 2
    outs = []
    for which, t_ref in enumerate((k_ref, v_ref)):
        lo = jnp.zeros((ncp, HEAD_DIM), F32)
        hi = jnp.zeros((ncp, HEAD_DIM), F32)
        for j in range(half):
            tj = t_ref[pl.ds(j, ncp, stride=CMP_STRIDE), :]
            a = (tj + pos_ref[which, j:j + 1, :]).astype(BF16)
            b = (tj + pos_ref[which, half + j:half + j + 1, :]).astype(BF16)
            lo = lo + _dot(a, w1_ref[which, j])
            hi = hi + _dot(b, w1_ref[which, half + j])
        pre = lo + pltpu.roll(hi, ncp - 1, 0)
        outs.append(_dot(jax.nn.gelu(pre).astype(BF16), w2_ref[which]))
    valid = lax.broadcasted_iota(jnp.int32, (ncp, HEAD_DIM), 0) < ncp - 1
    kc = _norm_rope(outs[0], kn_ref[0:1, :], c_ref[...], s1_ref[...], s2_ref[...])
    ok_ref[...] = jnp.where(valid, kc, 0.0).astype(BF16)
    ov_ref[...] = jnp.where(valid, outs[1], 0.0).astype(BF16)


def _attn_body(q_ref, gate_ref, c_ref, s1_ref, s2_ref, qn_ref, kc_ref, vc_ref,
               ks_ref, vs_ref, kw_ref, vw_ref, ov_ref, o_ref, *, tq, ncp):
    s0 = pl.program_id(2) * tq
    tq_pos = s0 + lax.broadcasted_iota(jnp.int32, (tq, 1), 0)
    c, s1, s2 = c_ref[...], s1_ref[...], s2_ref[...]
    qn = qn_ref[...]
    q = jnp.concatenate(
        [(_norm_rope(q_ref[:, g * HEAD_DIM:(g + 1) * HEAD_DIM], qn, c, s1, s2) * ATT_SCALE).astype(BF16)
         for g in range(GROUP)], axis=0)

    sc = _dot_nt(q, kc_ref[...]).reshape(GROUP, tq, ncp)
    n_id = lax.broadcasted_iota(jnp.int32, (1, ncp), 1)
    cmask = (n_id * CMP_STRIDE + (CMP_LEN - 1) <= tq_pos)[None]
    sc = jnp.where(cmask, sc, NEG)
    mx = jnp.max(sc, axis=-1, keepdims=True)
    e = jnp.where(cmask, jnp.exp(sc - mx), 0.0)
    p = e / jnp.maximum(jnp.sum(e, axis=-1, keepdims=True), 1e-30)
    o_cmp = _dot(p.reshape(GROUP * tq, ncp).astype(BF16), vc_ref[...]).reshape(GROUP, tq, HEAD_DIM)

    pg = p[0] + p[1] + p[2] + p[3]
    p_hi = pg.astype(BF16)
    p_lo = (pg - p_hi.astype(F32)).astype(BF16)
    imp = _dot(p_hi, ov_ref[...]) + _dot(p_lo, ov_ref[...])
    lane = lax.broadcasted_iota(jnp.int32, (tq, LANES), 1)
    cur = lax.shift_right_logical(tq_pos, 6)
    forced = (lane == 0) | (lane == cur) | (lane == cur - 1)
    imp = jnp.where(forced, 1e9, imp)
    imp = jnp.where(lane <= cur, imp, -1e9)
    lane_f = lane.astype(F32)

    def pick(_, carry):
        val, sel = carry
        top = jnp.max(val, axis=-1, keepdims=True)
        idx = jnp.min(jnp.where(val == top, lane_f, float(LANES)), axis=-1, keepdims=True)
        hit = lane_f == idx
        return jnp.where(hit, -jnp.inf, val), jnp.where(hit, 1.0, sel)

    _, sel = lax.fori_loop(0, SEL_TOPN, pick, (imp, jnp.zeros((tq, LANES), F32)))
    sel_b = sel.astype(BF16)

    blocks_per_tile = ATT_TK // SEL_LEN
    blk_row = lax.broadcasted_iota(jnp.int32, (LANES, ATT_TK), 0)
    blk_col = lax.shift_right_logical(lax.broadcasted_iota(jnp.int32, (LANES, ATT_TK), 1), 6)
    key_off = lax.broadcasted_iota(jnp.int32, (1, ATT_TK), 1)

    def sel_step(kt, carry):
        m, l, acc = carry
        k0 = pl.multiple_of(kt * ATT_TK, ATT_TK)
        s = _dot_nt(q, ks_ref[pl.ds(k0, ATT_TK), :]).reshape(GROUP, tq, ATT_TK)
        expand = jnp.where(blk_row == blk_col + kt * blocks_per_tile, 1.0, 0.0).astype(BF16)
        chosen = _dot(sel_b, expand)
        mask = ((chosen > 0.5) & (k0 + key_off <= tq_pos))[None]
        s = jnp.where(mask, s, NEG)
        m_new = jnp.maximum(m, jnp.max(s, axis=-1, keepdims=True))
        alpha = jnp.exp(m - m_new)
        pe = jnp.exp(s - m_new)
        l = alpha * l + jnp.sum(pe, axis=-1, keepdims=True)
        pv = _dot(pe.reshape(GROUP * tq, ATT_TK).astype(BF16), vs_ref[pl.ds(k0, ATT_TK), :])
        return m_new, l, alpha * acc + pv.reshape(GROUP, tq, HEAD_DIM)

    n_tiles = lax.div(s0 + tq - 1, ATT_TK) + 1
    init = (jnp.full((GROUP, tq, 1), NEG, F32), jnp.zeros((GROUP, tq, 1), F32),
            jnp.zeros((GROUP, tq, HEAD_DIM), F32))
    _, l_sel, acc_sel = lax.fori_loop(0, n_tiles, sel_step, init)
    o_sel = acc_sel / l_sel

    span = WINDOW + tq
    w0 = pl.multiple_of(jnp.maximum(s0 - WINDOW, 0), tq)
    sw = _dot_nt(q, kw_ref[pl.ds(w0, span), :]).reshape(GROUP, tq, span)
    kp = w0 + lax.broadcasted_iota(jnp.int32, (1, span), 1)
    wmask = ((kp <= tq_pos) & (kp > tq_pos - WINDOW))[None]
    sw = jnp.where(wmask, sw, NEG)
    mw = jnp.max(sw, axis=-1, keepdims=True)
    ew = jnp.exp(sw - mw)
    pw = ew / jnp.sum(ew, axis=-1, keepdims=True)
    o_win = _dot(pw.reshape(GROUP * tq, span).astype(BF16), vw_ref[pl.ds(w0, span), :])
    o_win = o_win.reshape(GROUP, tq, HEAD_DIM)

    gates = jax.nn.sigmoid(gate_ref[...])
    for g in range(GROUP):
        gc = gates[:, g * N_BRANCH + 0:g * N_BRANCH + 1]
        gs = gates[:, g * N_BRANCH + 1:g * N_BRANCH + 2]
        gw = gates[:, g * N_BRANCH + 2:g * N_BRANCH + 3]
        o_ref[:, g * HEAD_DIM:(g + 1) * HEAD_DIM] = (
            gc * o_cmp[g] + gs * o_sel[g] + gw * o_win[g]).astype(o_ref.dtype)


def nsa_mixer(q_raw, kv_raw, gate_raw, batch, q_norm, k_norm, cmp_pos, cmp_w1, cmp_w2):
    m = q_raw.shape[0]
    seq = m // batch
    ncp = seq // CMP_STRIDE
    nsel = seq // SEL_LEN
    assert seq % ATT_TK == 0 and nsel <= LANES and seq >= WINDOW + ATT_TQ
    pos = jnp.arange(seq)
    c, s1, s2 = _rope_tables(pos)
    cc, cs1, cs2 = _rope_tables(jnp.arange(ncp) * CMP_STRIDE + CMP_LEN - 1)
    col = lambda branch, kv: (branch * 2 + kv) * KV_HEADS

    tl = _pick(seq, (1024, 512))
    nl = seq // tl
    kvspec = lambda base: pl.BlockSpec((tl, HEAD_DIM), lambda b, h, t: (b * nl + t, base + h))
    tab = pl.BlockSpec((tl, HEAD_DIM), lambda b, h, t: (t, 0))
    outspec = pl.BlockSpec((None, None, tl, HEAD_DIM), lambda b, h, t: (b, h, t, 0))
    kv_shape = jax.ShapeDtypeStruct((batch, KV_HEADS, seq, HEAD_DIM), BF16)
    k_s, v_s, k_w, v_w = pl.pallas_call(
        _kv_prep_body,
        out_shape=(kv_shape,) * 4,
        grid=(batch, KV_HEADS, nl),
        in_specs=[kvspec(col(1, 0)), kvspec(col(1, 1)), kvspec(col(2, 0)), kvspec(col(2, 1)),
                  pl.BlockSpec((N_BRANCH, HEAD_DIM), lambda b, h, t: (0, 0)), tab, tab, tab],
        out_specs=(outspec,) * 4,
        compiler_params=_cparams(("parallel", "parallel", "parallel")),
        name="nsa_kv_prep",
    )(kv_raw, kv_raw, kv_raw, kv_raw, k_norm, c, s1, s2)

    w1 = cmp_w1.reshape(2, CMP_LEN, HEAD_DIM, HEAD_DIM).astype(BF16)
    full = lambda shape: pl.BlockSpec(shape, lambda b, h: (0,) * len(shape))
    cshape = jax.ShapeDtypeStruct((batch, KV_HEADS, ncp, HEAD_DIM), BF16)
    cspec = pl.BlockSpec((None, None, ncp, HEAD_DIM), lambda b, h: (b, h, 0, 0))
    k_c, v_c = pl.pallas_call(
        functools.partial(_cmp_prep_body, ncp=ncp),
        out_shape=(cshape, cshape),
        grid=(batch, KV_HEADS),
        in_specs=[pl.BlockSpec((seq, HEAD_DIM), lambda b, h: (b, col(0, 0) + h)),
                  pl.BlockSpec((seq, HEAD_DIM), lambda b, h: (b, col(0, 1) + h)),
                  full((2, CMP_LEN, HEAD_DIM)), full((2, CMP_LEN, HEAD_DIM, HEAD_DIM)),
                  full((2, HEAD_DIM, HEAD_DIM)), full((N_BRANCH, HEAD_DIM)),
                  full((ncp, HEAD_DIM)), full((ncp, HEAD_DIM)), full((ncp, HEAD_DIM))],
        out_specs=(cspec, cspec),
        compiler_params=_cparams(("parallel", "parallel")),
        name="nsa_cmp_prep",
    )(kv_raw, kv_raw, cmp_pos, w1, cmp_w2.astype(BF16), k_norm, cc, cs1, cs2)

    cstart = jnp.arange(ncp)[:, None] * CMP_STRIDE
    sstart = jnp.arange(LANES)[None, :] * SEL_LEN
    overlap = jnp.maximum(jnp.minimum(cstart + CMP_LEN, sstart + SEL_LEN) - jnp.maximum(cstart, sstart), 0)
    overlap = jnp.where(jnp.arange(LANES)[None, :] < nsel, overlap, 0)
    overlap = (overlap.astype(F32) / CMP_STRIDE).astype(BF16)

    tq = ATT_TQ
    nq = seq // tq
    qtab = pl.BlockSpec((tq, HEAD_DIM), lambda b, h, i: (i, 0))
    kvfull = lambda n: pl.BlockSpec((None, None, n, HEAD_DIM), lambda b, h, i: (b, h, 0, 0))
    return pl.pallas_call(
        functools.partial(_attn_body, tq=tq, ncp=ncp),
        out_shape=jax.ShapeDtypeStruct((m, HEADS * HEAD_DIM), BF16),
        grid=(batch, KV_HEADS, nq),
        in_specs=[pl.BlockSpec((tq, GROUP * HEAD_DIM), lambda b, h, i: (b * nq + i, h)),
                  pl.BlockSpec((tq, LANES), lambda b, h, i: (b * nq + i, h)),
                  qtab, qtab, qtab,
                  pl.BlockSpec((1, HEAD_DIM), lambda b, h, i: (0, 0)),
                  kvfull(ncp), kvfull(ncp), kvfull(seq), kvfull(seq), kvfull(seq), kvfull(seq),
                  pl.BlockSpec((ncp, LANES), lambda b, h, i: (0, 0))],
        out_specs=pl.BlockSpec((tq, GROUP * HEAD_DIM), lambda b, h, i: (b * nq + i, h)),
        compiler_params=_cparams(("parallel", "parallel", "arbitrary")),
        name="nsa_attn",
    )(q_raw, gate_raw, c, s1, s2, q_norm.reshape(1, HEAD_DIM), k_c, v_c, k_s, v_s, k_w, v_w, overlap)


def _router_body(x_ref, g_ref, whi_ref, wlo_ref, b_ref, h_ref, r_ref):
    x = x_ref[...]
    h = x * lax.rsqrt(jnp.mean(x * x, axis=-1, keepdims=True) + NORM_EPS) * g_ref[...]
    h_ref[...] = h
    h_hi = h.astype(BF16)
    h_lo = (h - h_hi.astype(F32)).astype(BF16)
    logits = (_dot(h_hi, whi_ref[...]) + _dot(h_lo, whi_ref[...]) + _dot(h_hi, wlo_ref[...])
              + b_ref[...])
    lane = lax.broadcasted_iota(jnp.int32, logits.shape, 1)
    lane_f = lane.astype(F32)
    none = float(LANES)
    is_g = lane < N_GROUPS
    glog = jnp.where(is_g, logits, -jnp.inf)
    gmax = jnp.max(glog, axis=-1, keepdims=True)
    gidx = jnp.min(jnp.where(glog == gmax, lane_f, none), axis=-1, keepdims=True)
    g_p = 1.0 / jnp.sum(jnp.where(is_g, jnp.exp(logits - gmax), 0.0), axis=-1, keepdims=True)
    e_lo = N_GROUPS + gidx * PER_GROUP
    elog = jnp.where((lane_f >= e_lo) & (lane_f < e_lo + PER_GROUP), logits, -jnp.inf)
    m1 = jnp.max(elog, axis=-1, keepdims=True)
    i1 = jnp.min(jnp.where(elog == m1, lane_f, none), axis=-1, keepdims=True)
    elog = jnp.where(lane_f == i1, -jnp.inf, elog)
    m2 = jnp.max(elog, axis=-1, keepdims=True)
    i2 = jnp.min(jnp.where(elog == m2, lane_f, none), axis=-1, keepdims=True)
    t = jnp.exp(m2 - m1)
    w1 = g_p / (1.0 + t)
    w2 = g_p * t / (1.0 + t)
    r_ref[...] = jnp.where(lane == 0, i1 - N_GROUPS,
                           jnp.where(lane == 1, i2 - N_GROUPS,
                                     jnp.where(lane == 2, w1, jnp.where(lane == 3, w2, 0.0))))


def moe_router(x, norm, rg_w, rg_b, re_w, re_b, tm=256):
    m, d = x.shape
    pad = LANES - N_GROUPS - N_EXPERTS
    wr = jnp.pad(jnp.concatenate([rg_w, re_w], axis=1).astype(F32), ((0, 0), (0, pad)))
    whi = wr.astype(BF16)
    wlo = (wr - whi.astype(F32)).astype(BF16)
    bias = jnp.pad(jnp.concatenate([rg_b, re_b]).astype(F32), (0, pad)).reshape(1, LANES)
    const = lambda shape: pl.BlockSpec(shape, lambda i: (0, 0))
    return pl.pallas_call(
        _router_body,
        out_shape=(jax.ShapeDtypeStruct((m, d), F32), jax.ShapeDtypeStruct((m, LANES), F32)),
        grid=(m // tm,),
        in_specs=[pl.BlockSpec((tm, d), lambda i: (i, 0)), const((1, d)),
                  const((d, LANES)), const((d, LANES)), const((1, LANES))],
        out_specs=(pl.BlockSpec((tm, d), lambda i: (i, 0)), pl.BlockSpec((tm, LANES), lambda i: (i, 0))),
        compiler_params=_cparams(("parallel",)),
        name="moe_router",
    )(x, norm.reshape(1, d), whi, wlo, bias)


def _rank_body(e_ref, rank_ref, cnt_ref, carry_ref):
    @pl.when(pl.program_id(0) == 0)
    def _():
        carry_ref[...] = jnp.zeros_like(carry_ref)

    e = e_ref[0]
    sub = lax.broadcasted_iota(jnp.int32, (N_EXPERTS, RANK_T), 0)
    onehot = jnp.where(sub == e, 1.0, 0.0)
    r = lax.broadcasted_iota(jnp.int32, (RANK_T, RANK_T), 0)
    c = lax.broadcasted_iota(jnp.int32, (RANK_T, RANK_T), 1)
    before = jnp.where(r < c, 1.0, 0.0).astype(BF16)
    prefix = _dot(onehot.astype(BF16), before)
    carry = carry_ref[:, 0:1]
    rank = jnp.sum(onehot * (prefix + carry), axis=0, keepdims=True)
    rank_ref[0] = rank.astype(jnp.int32)
    carry_ref[...] = carry_ref[...] + jnp.sum(onehot, axis=1, keepdims=True)
    cnt_ref[...] = carry_ref[...]


def moe_rank(flat_e):
    a = flat_e.shape[0]
    nt = a // RANK_T
    rank, cnt = pl.pallas_call(
        _rank_body,
        out_shape=(jax.ShapeDtypeStruct((nt, 1, RANK_T), jnp.int32),
                   jax.ShapeDtypeStruct((N_EXPERTS, LANES), F32)),
        grid=(nt,),
        in_specs=[pl.BlockSpec((1, 1, RANK_T), lambda i: (i, 0, 0))],
        out_specs=(pl.BlockSpec((1, 1, RANK_T), lambda i: (i, 0, 0)),
                   pl.BlockSpec((N_EXPERTS, LANES), lambda i: (0, 0))),
        scratch_shapes=[pltpu.VMEM((N_EXPERTS, LANES), F32)],
        compiler_params=_cparams(("arbitrary",)),
        name="moe_rank",
    )(flat_e.reshape(nt, 1, RANK_T))
    return rank.reshape(a), cnt[:, 0].astype(jnp.int32)


def _row_copy(src, s, dst, d, sem):
    return pltpu.make_async_copy(src.at[pl.ds(s, 1), :], dst.at[pl.ds(d, 1), :], sem)


def _scatter_body(dest_ref, h_ref, xs_in_ref, xs_ref, sem):
    del xs_in_ref
    base = pl.program_id(0) * (MOE_TT * TOP_K)

    def start(r, carry):
        for k in range(TOP_K):
            _row_copy(h_ref, r, xs_ref, dest_ref[base + r * TOP_K + k], sem).start()
        return carry

    def wait(r, carry):
        for k in range(TOP_K):
            _row_copy(h_ref, 0, xs_ref, 0, sem).wait()
        return carry

    lax.fori_loop(0, MOE_TT, start, 0)
    lax.fori_loop(0, MOE_TT, wait, 0)


def moe_scatter(h, dest, n_rows):
    m, d = h.shape
    return pl.pallas_call(
        _scatter_body,
        out_shape=jax.ShapeDtypeStruct((n_rows, d), h.dtype),
        grid_spec=pltpu.PrefetchScalarGridSpec(
            num_scalar_prefetch=1,
            grid=(m // MOE_TT,),
            in_specs=[pl.BlockSpec((MOE_TT, d), lambda i, dest: (i, 0)),
                      pl.BlockSpec(memory_space=pl.ANY)],
            out_specs=pl.BlockSpec(memory_space=pl.ANY),
            scratch_shapes=[pltpu.SemaphoreType.DMA(())]),
        input_output_aliases={2: 0},
        compiler_params=_cparams(("arbitrary",)),
        name="moe_scatter",
    )(dest, h, jnp.zeros((n_rows, d), h.dtype))


def _moe_mm_body(blk_e_ref, nblk_ref, x_ref, w1_ref, w3_ref, w2_ref, y_ref):
    del blk_e_ref
    used = pl.program_id(0) < nblk_ref[0]

    @pl.when(used)
    def _():
        xb = x_ref[...].astype(BF16)
        hb = (jax.nn.silu(_dot(xb, w1_ref[0])) * _dot(xb, w3_ref[0])).astype(BF16)
        y_ref[...] = _dot(hb, w2_ref[0])

    @pl.when(jnp.logical_not(used))
    def _():
        y_ref[...] = jnp.zeros_like(y_ref)


def moe_experts(xs, blk_e, nblk, w1, w3, w2):
    n_rows, d = xs.shape
    ff = w1.shape[2]
    wspec = lambda r, c: pl.BlockSpec((1, r, c), lambda j, be, nb: (be[j], 0, 0))
    return pl.pallas_call(
        _moe_mm_body,
        out_shape=jax.ShapeDtypeStruct((n_rows, d), F32),
        grid_spec=pltpu.PrefetchScalarGridSpec(
            num_scalar_prefetch=2,
            grid=(n_rows // MOE_ROWS,),
            in_specs=[pl.BlockSpec((MOE_ROWS, d), lambda j, be, nb: (j, 0)),
                      wspec(d, ff), wspec(d, ff), wspec(ff, d)],
            out_specs=pl.BlockSpec((MOE_ROWS, d), lambda j, be, nb: (j, 0))),
        compiler_params=_cparams(("arbitrary",)),
        name="moe_experts",
    )(blk_e, nblk, xs, w1, w3, w2)


def _combine_body(dest_ref, x_ref, r_ref, g_ref, y_ref, xo_ref, h_ref, buf, sem):
    i = pl.program_id(0)
    n = pl.num_programs(0)

    def start_step(step, slot):
        base = step * (MOE_TT * TOP_K)

        def body(r, carry):
            for k in range(TOP_K):
                _row_copy(y_ref, dest_ref[base + r * TOP_K + k], buf.at[slot], k * MOE_TT + r,
                          sem.at[slot]).start()
            return carry

        lax.fori_loop(0, MOE_TT, body, 0)

    @pl.when(i == 0)
    def _():
        start_step(0, 0)

    @pl.when(i + 1 < n)
    def _():
        start_step(i + 1, (i + 1) % 2)

    slot = i % 2

    def wait(r, carry):
        for k in range(TOP_K):
            _row_copy(y_ref, 0, buf.at[slot], 0, sem.at[slot]).wait()
        return carry

    lax.fori_loop(0, MOE_TT, wait, 0)
    r = r_ref[...]
    moe = r[:, 2:3] * buf[slot, 0:MOE_TT, :] + r[:, 3:4] * buf[slot, MOE_TT:2 * MOE_TT, :]
    x = x_ref[...] + moe
    xo_ref[...] = x
    hn = x * lax.rsqrt(jnp.mean(x * x, axis=-1, keepdims=True) + NORM_EPS) * g_ref[...]
    h_ref[...] = hn.astype(h_ref.dtype)


def moe_combine(x, route, y, dest, norm):
    m, d = x.shape
    tok = lambda c: pl.BlockSpec((MOE_TT, c), lambda i, dest: (i, 0))
    return pl.pallas_call(
        _combine_body,
        out_shape=(jax.ShapeDtypeStruct((m, d), F32), jax.ShapeDtypeStruct((m, d), BF16)),
        grid_spec=pltpu.PrefetchScalarGridSpec(
            num_scalar_prefetch=1,
            grid=(m // MOE_TT,),
            in_specs=[tok(d), tok(LANES), pl.BlockSpec((1, d), lambda i, dest: (0, 0)),
                      pl.BlockSpec(memory_space=pl.ANY)],
            out_specs=(tok(d), tok(d)),
            scratch_shapes=[pltpu.VMEM((2, TOP_K * MOE_TT, d), F32),
                            pltpu.SemaphoreType.DMA((2,))]),
        compiler_params=_cparams(("arbitrary",)),
        name="moe_combine",
    )(dest, x, route, norm.reshape(1, d), y)


def moe_ffn(x, norm2, rg_w, rg_b, re_w, re_b, w1, w3, w2, norm3):
    m, d = x.shape
    h, route = moe_router(x, norm2, rg_w, rg_b, re_w, re_b)
    flat_e = route[:, :TOP_K].astype(jnp.int32).reshape(m * TOP_K)
    rank, counts = moe_rank(flat_e)
    padded = (counts + MOE_ROWS - 1) // MOE_ROWS * MOE_ROWS
    pend = jnp.cumsum(padded)
    pstart = pend - padded
    dest = (pstart[flat_e] + rank).astype(jnp.int32)
    n_blocks = (m * TOP_K + N_EXPERTS * (MOE_ROWS - 1) + MOE_ROWS - 1) // MOE_ROWS
    nblk = (pend[-1:] // MOE_ROWS).astype(jnp.int32)
    blk_e = jnp.minimum(jnp.searchsorted(pend, jnp.arange(n_blocks) * MOE_ROWS, side='right'),
                        N_EXPERTS - 1).astype(jnp.int32)
    xs = moe_scatter(h, dest, n_blocks * MOE_ROWS)
    y = moe_experts(xs, blk_e, nblk, w1.astype(BF16), w3.astype(BF16), w2.astype(BF16))
    return moe_combine(x, route, y, dest, norm3)


def kernel(x, p, norm1, w_in, q_norm, k_norm, cmp_pos, cmp_w1, cmp_w2, gm_norm, gm_ws, gm_bs,
           s5_lambda_re, s5_lambda_im, s5_log_dt, s5_b_re, s5_b_im, s5_c_re, s5_c_im, s5_d,
           s5_w_glu, s5_b_glu, w_branch, w_out, norm2, router_group_w, router_group_b,
           router_expert_w, router_expert_b, expert_w1, expert_w3, expert_w2, norm3, w_ple,
           w_ple_gate):
    b, l, d = x.shape
    m = b * l
    xf = x.reshape(m, d)
    q_cols = HEADS * HEAD_DIM
    kv_cols = N_BRANCH * 2 * KV_HEADS * HEAD_DIM
    ng_cols = HEADS * N_BRANCH
    splits = [0, q_cols, q_cols + kv_cols, q_cols + kv_cols + ng_cols]
    splits.append(splits[-1] + 2 * GM_WIDTH)
    splits.append(splits[-1] + S5_WIDTH)
    splits.append(splits[-1] + 3 * d)
    per_head = GROUP * N_BRANCH
    for i in range(p.shape[0]):
        h = rms_cast(xf, norm1[i])
        w = w_in[i]
        seg = lambda k: w[:, splits[k]:splits[k + 1]].astype(BF16)
        q_raw = mm_plain(h, seg(0), name="proj_q")
        kv_raw = mm_plain(h, seg(1), name="proj_kv")
        w_ng = jnp.pad(w[:, splits[2]:splits[3]].reshape(d, KV_HEADS, per_head),
                       ((0, 0), (0, 0), (0, LANES - per_head))).reshape(d, KV_HEADS * LANES)
        ng_raw = mm_plain(h, w_ng.astype(BF16), name="proj_gate")
        gm_raw = mm_plain(h, seg(3), name="proj_gm")
        s5_raw = mm_plain(h, seg(4), name="proj_s5")
        mg_raw = mm_plain(h, seg(5), name="proj_merge")

        o_gm = gmlp_mixer(gm_raw, gm_norm[i], gm_ws[i], gm_bs[i])
        o_s5 = s5_mixer(s5_raw, b, s5_lambda_re[i], s5_lambda_im[i], s5_log_dt[i], s5_b_re[i],
                        s5_b_im[i], s5_c_re[i], s5_c_im[i], s5_d[i], s5_w_glu[i], s5_b_glu[i])
        o_att = nsa_mixer(q_raw, kv_raw, ng_raw, b, q_norm[i], k_norm[i], cmp_pos[i], cmp_w1[i],
                          cmp_w2[i])
        wb = w_branch[i].astype(BF16)
        merged = mm_merge([o_gm, o_s5, o_att], [wb[0], wb[1], wb[2]], mg_raw)
        xf = mm_residual(merged, w_out[i].astype(BF16), xf)
        xf, h3 = moe_ffn(xf, norm2[i], router_group_w[i], router_group_b[i], router_expert_w[i],
                         router_expert_b[i], expert_w1[i], expert_w3[i], expert_w2[i], norm3[i])
        xf = mm_ple(h3, w_ple_gate[i].astype(BF16), p[i].reshape(m, -1).astype(BF16),
                    w_ple[i].astype(BF16), xf)
    return xf.reshape(b, l, d)
```

```python
import functools

import jax
import jax.numpy as jnp
from jax import lax
from jax.experimental import pallas as pl
from jax.experimental.pallas import tpu as pltpu

F32 = jnp.float32
BF16 = jnp.bfloat16

NORM_EPS = 1e-6
LANES = 128
VMEM_LIMIT = 52 * 1024 * 1024

HEADS = 16
KV_HEADS = 4
GROUP = HEADS // KV_HEADS
HEAD_DIM = 128
ROPE_THETA = 500000.0
ROPE_DIMS = HEAD_DIM // 4
ROPE_HALF = ROPE_DIMS // 2
CMP_LEN = 32
CMP_STRIDE = 16
SEL_LEN = 64
SEL_TOPN = 16
WINDOW = 512
N_BRANCH = 3
NEG = -1e30
ATT_SCALE = HEAD_DIM ** -0.5 * 1.4426950408889634
ATT_TQ = 128
ATT_TK = 512

GM_WIDTH = 2048
GM_GROUPS = 16
GM_CHUNK = 128

S5_WIDTH = 2048
S5_GROUP_DIM = 16
S5_GROUPS = S5_WIDTH // S5_GROUP_DIM
S5_STATE = 64
S5_TILE_GROUPS = LANES // S5_GROUP_DIM
S5_TILE_STATES = S5_TILE_GROUPS * S5_STATE
S5_T = 128
S5_LEVELS = 7

N_GROUPS = 4
PER_GROUP = 8
N_EXPERTS = N_GROUPS * PER_GROUP
TOP_K = 2
MOE_ROWS = 256
RANK_T = 512
MOE_TT = 64


def _cparams(sem):
    return pltpu.CompilerParams(dimension_semantics=sem, vmem_limit_bytes=VMEM_LIMIT)


def _dot(a, b):
    return jnp.dot(a, b, preferred_element_type=F32)


def _dot_nt(a, b):
    return lax.dot_general(a, b, (((1,), (1,)), ((), ())), preferred_element_type=F32)


def _rms_body(x_ref, g_ref, o_ref):
    x = x_ref[...]
    y = x * lax.rsqrt(jnp.mean(x * x, axis=-1, keepdims=True) + NORM_EPS)
    o_ref[...] = (y * g_ref[...]).astype(o_ref.dtype)


def rms_cast(x, g, tm=256):
    m, d = x.shape
    return pl.pallas_call(
        _rms_body,
        out_shape=jax.ShapeDtypeStruct((m, d), BF16),
        grid=(m // tm,),
        in_specs=[pl.BlockSpec((tm, d), lambda i: (i, 0)),
                  pl.BlockSpec((1, d), lambda i: (0, 0))],
        out_specs=pl.BlockSpec((tm, d), lambda i: (i, 0)),
        compiler_params=_cparams(("parallel",)),
        name="rms_cast",
    )(x, g.reshape(1, d))


def _mm_plain_body(a_ref, w_ref, o_ref):
    o_ref[...] = _dot(a_ref[...], w_ref[...]).astype(o_ref.dtype)


def _mm_glu_body(a_ref, w_ref, y_ref, b_ref, o_ref):
    acc = _dot(a_ref[...].astype(BF16), w_ref[...]) + b_ref[...]
    o_ref[...] = (y_ref[...] * jax.nn.sigmoid(acc)).astype(o_ref.dtype)


def _mm_res_body(a_ref, w_ref, x_ref, o_ref):
    o_ref[...] = x_ref[...] + _dot(a_ref[...], w_ref[...])


def _mm_merge_body(a0, a1, a2, w0, w1, w2, g0, g1, g2, o_ref):
    acc = jax.nn.sigmoid(g0[...]) * _dot(a0[...], w0[...])
    acc = acc + jax.nn.sigmoid(g1[...]) * _dot(a1[...], w1[...])
    acc = acc + jax.nn.sigmoid(g2[...]) * _dot(a2[...], w2[...])
    o_ref[...] = acc.astype(o_ref.dtype)


def _mm_ple_body(a_ref, w_ref, p_ref, wp_ref, x_ref, o_ref):
    gate = jax.nn.sigmoid(_dot(a_ref[...], w_ref[...]))
    o_ref[...] = x_ref[...] + _dot(p_ref[...], wp_ref[...]) * gate


def _lhs_spec(tm, k):
    return pl.BlockSpec((tm, k), lambda j, i: (i, 0))


def _rhs_spec(k, tn):
    return pl.BlockSpec((k, tn), lambda j, i: (0, j))


def _tile_spec(tm, tn, col_blocks=0):
    return pl.BlockSpec((tm, tn), lambda j, i: (i, j + col_blocks))


def _mm_call(body, args, in_specs, m, n, tm, tn, out_dtype, name):
    return pl.pallas_call(
        body,
        out_shape=jax.ShapeDtypeStruct((m, n), out_dtype),
        grid=(n // tn, m // tm),
        in_specs=in_specs,
        out_specs=pl.BlockSpec((tm, tn), lambda j, i: (i, j)),
        compiler_params=_cparams(("parallel", "parallel")),
        name=name,
    )(*args)


def _pick(n, prefs):
    for t in prefs:
        if n % t == 0:
            return t
    return n


def mm_plain(a, w, out_dtype=F32, name="mm"):
    m, k = a.shape
    n = w.shape[1]
    tm, tn = _pick(m, (512, 256, 128)), _pick(n, (1024, 512, 256, 128))
    return _mm_call(_mm_plain_body, (a, w), [_lhs_spec(tm, k), _rhs_spec(k, tn)],
                    m, n, tm, tn, out_dtype, name)


def mm_glu(y, w, b):
    m, k = y.shape
    n = w.shape[1]
    tm, tn = _pick(m, (512, 256, 128)), _pick(n, (1024, 512, 256, 128))
    specs = [_lhs_spec(tm, k), _rhs_spec(k, tn), _tile_spec(tm, tn),
             pl.BlockSpec((1, tn), lambda j, i: (0, j))]
    return _mm_call(_mm_glu_body, (y, w, y, b.reshape(1, n)), specs, m, n, tm, tn, BF16, "mm_glu")


def mm_residual(a, w, x):
    m, k = a.shape
    n = w.shape[1]
    tm, tn = _pick(m, (512, 256, 128)), _pick(n, (1024, 512, 256, 128))
    specs = [_lhs_spec(tm, k), _rhs_spec(k, tn), _tile_spec(tm, tn)]
    return _mm_call(_mm_res_body, (a, w, x), specs, m, n, tm, tn, F32, "mm_residual")


def mm_merge(o_list, w_list, mg_raw):
    m, k = o_list[0].shape
    n = w_list[0].shape[1]
    tm, tn = _pick(m, (512, 256, 128)), _pick(n, (512, 256, 128))
    specs = ([_lhs_spec(tm, k)] * 3 + [_rhs_spec(k, tn)] * 3
             + [_tile_spec(tm, tn, c * (n // tn)) for c in range(3)])
    return _mm_call(_mm_merge_body, (*o_list, *w_list, mg_raw, mg_raw, mg_raw), specs,
                    m, n, tm, tn, BF16, "mm_merge")


def mm_ple(h, wg, p, wp, x):
    m, k = h.shape
    n = wg.shape[1]
    kp = p.shape[1]
    tm, tn = _pick(m, (512, 256, 128)), _pick(n, (1024, 512, 256, 128))
    specs = [_lhs_spec(tm, k), _rhs_spec(k, tn), _lhs_spec(tm, kp), _rhs_spec(kp, tn),
             _tile_spec(tm, tn)]
    return _mm_call(_mm_ple_body, (h, wg, p, wp, x), specs, m, n, tm, tn, F32, "mm_ple")


def _gmlp_body(z_ref, gn_ref, w_ref, b_ref, o_ref, *, rows):
    z = jax.nn.gelu(z_ref[...])
    u = z[:, :GM_WIDTH]
    v = z[:, GM_WIDTH:]
    v = v * lax.rsqrt(jnp.mean(v * v, axis=-1, keepdims=True) + NORM_EPS) * gn_ref[...]
    vb = v.astype(BF16)
    for c in range(rows // GM_CHUNK):
        r0 = c * GM_CHUNK
        for g in range(GM_GROUPS):
            c0 = g * LANES
            s = _dot(w_ref[g], vb[r0:r0 + GM_CHUNK, c0:c0 + LANES]) + b_ref[:, c0:c0 + LANES]
            o_ref[r0:r0 + GM_CHUNK, c0:c0 + LANES] = (
                u[r0:r0 + GM_CHUNK, c0:c0 + LANES] * s).astype(o_ref.dtype)


def gmlp_mixer(z, gm_norm, ws, bs, rows=256):
    m = z.shape[0]
    tri = jnp.tril(jnp.ones((GM_CHUNK, GM_CHUNK), dtype=bool))
    w = jnp.where(tri[None], ws, 0.0).astype(BF16)
    bias = jnp.repeat(bs.T, LANES, axis=1)
    return pl.pallas_call(
        functools.partial(_gmlp_body, rows=rows),
        out_shape=jax.ShapeDtypeStruct((m, GM_WIDTH), BF16),
        grid=(m // rows,),
        in_specs=[pl.BlockSpec((rows, 2 * GM_WIDTH), lambda i: (i, 0)),
                  pl.BlockSpec((1, GM_WIDTH), lambda i: (0, 0)),
                  pl.BlockSpec((GM_GROUPS, GM_CHUNK, GM_CHUNK), lambda i: (0, 0, 0)),
                  pl.BlockSpec((GM_CHUNK, GM_WIDTH), lambda i: (0, 0))],
        out_specs=pl.BlockSpec((rows, GM_WIDTH), lambda i: (i, 0)),
        compiler_params=_cparams(("parallel",)),
        name="gmlp",
    )(z, gm_norm.reshape(1, GM_WIDTH), w, bias)


def _shift_rows(x, s):
    t = x.shape[0]
    if s % 8 == 0:
        return jnp.concatenate([jnp.zeros((s, x.shape[1]), x.dtype), x[:t - s]], axis=0)
    rolled = pltpu.roll(x, s, 0)
    row = lax.broadcasted_iota(jnp.int32, x.shape, 0)
    return jnp.where(row >= s, rolled, 0.0)


def _s5_body(u_ref, bre_ref, bim_ref, cre_ref, cim_ref, are_ref, aim_ref, d_ref, o_ref,
             hre_ref, him_ref):
    @pl.when(pl.program_id(2) == 0)
    def _():
        hre_ref[...] = jnp.zeros_like(hre_ref)
        him_ref[...] = jnp.zeros_like(him_ref)

    u = u_ref[...]
    ub = u.astype(BF16)
    xr = _dot(ub, bre_ref[0])
    xi = _dot(ub, bim_ref[0])
    are = are_ref[0]
    aim = aim_ref[0]
    hr = hre_ref[0:1, :]
    hi = him_ref[0:1, :]
    a_r = are[0:1, :]
    a_i = aim[0:1, :]
    row = lax.broadcasted_iota(jnp.int32, xr.shape, 0)
    first = row == 0
    xr = xr + jnp.where(first, a_r * hr - a_i * hi, 0.0)
    xi = xi + jnp.where(first, a_r * hi + a_i * hr, 0.0)
    for k in range(S5_LEVELS):
        s = 1 << k
        cr = are[k:k + 1, :]
        ci = aim[k:k + 1, :]
        sr = _shift_rows(xr, s)
        si = _shift_rows(xi, s)
        xr, xi = xr + cr * sr - ci * si, xi + cr * si + ci * sr
    t = xr.shape[0]
    hre_ref[0:1, :] = xr[t - 1:t, :]
    him_ref[0:1, :] = xi[t - 1:t, :]
    y = _dot(xr.astype(BF16), cre_ref[0]) - _dot(xi.astype(BF16), cim_ref[0])
    y = y + d_ref[...] * u
    o_ref[...] = jax.nn.gelu(y)


def _s5_discretize(lam_re, lam_im, log_dt, b_re, b_im):
    dt = jnp.exp(log_dt.astype(F32))[:, None]
    lre = jnp.minimum(lam_re.astype(F32), -1e-4)
    lim = lam_im.astype(F32)
    mag = jnp.exp(lre * dt)
    a_re = mag * jnp.cos(lim * dt)
    a_im = mag * jnp.sin(lim * dt)
    den = lre * lre + lim * lim
    nr = a_re - 1.0
    f_re = (nr * lre + a_im * lim) / den
    f_im = (a_im * lre - nr * lim) / den
    br = b_re.astype(F32)
    bi = b_im.astype(F32)
    bb_re = f_re[..., None] * br - f_im[..., None] * bi
    bb_im = f_re[..., None] * bi + f_im[..., None] * br
    return a_re, a_im, bb_re, bb_im


def _block_diag(x):
    nt, gl, r, c = x.shape
    eye = jnp.eye(gl, dtype=x.dtype)
    out = x[:, :, :, None, :] * eye[None, :, None, :, None]
    return out.reshape(nt, gl * r, gl * c)


def s5_mixer(u, batch, lam_re, lam_im, log_dt, b_re, b_im, c_re, c_im, d_skip, w_glu, b_glu):
    m = u.shape[0]
    seq = m // batch
    nt = S5_GROUPS // S5_TILE_GROUPS
    a_re, a_im, bb_re, bb_im = _s5_discretize(lam_re, lam_im, log_dt, b_re, b_im)
    tile = lambda x: x.reshape(nt, S5_TILE_GROUPS, *x.shape[1:])
    bblk_re = _block_diag(jnp.swapaxes(tile(bb_re), 2, 3)).astype(BF16)
    bblk_im = _block_diag(jnp.swapaxes(tile(bb_im), 2, 3)).astype(BF16)
    cblk_re = _block_diag(jnp.swapaxes(tile(c_re.astype(F32)), 2, 3)).astype(BF16)
    cblk_im = _block_diag(jnp.swapaxes(tile(c_im.astype(F32)), 2, 3)).astype(BF16)
    pr, pi = [a_re.reshape(-1)], [a_im.reshape(-1)]
    for _ in range(7):
        r, i = pr[-1], pi[-1]
        pr.append(r * r - i * i)
        pi.append(2.0 * r * i)
    apow_re = jnp.stack(pr, 0).reshape(8, nt, S5_TILE_STATES).transpose(1, 0, 2)
    apow_im = jnp.stack(pi, 0).reshape(8, nt, S5_TILE_STATES).transpose(1, 0, 2)

    nchunk = seq // S5_T
    wspec = lambda r, c: pl.BlockSpec((1, r, c), lambda b, j, t: (j, 0, 0))
    y = pl.pallas_call(
        _s5_body,
        out_shape=jax.ShapeDtypeStruct((m, S5_WIDTH), F32),
        grid=(batch, nt, nchunk),
        in_specs=[pl.BlockSpec((S5_T, LANES), lambda b, j, t: (b * nchunk + t, j)),
                  wspec(LANES, S5_TILE_STATES), wspec(LANES, S5_TILE_STATES),
                  wspec(S5_TILE_STATES, LANES), wspec(S5_TILE_STATES, LANES),
                  wspec(8, S5_TILE_STATES), wspec(8, S5_TILE_STATES),
                  pl.BlockSpec((1, LANES), lambda b, j, t: (0, j))],
        out_specs=pl.BlockSpec((S5_T, LANES), lambda b, j, t: (b * nchunk + t, j)),
        scratch_shapes=[pltpu.VMEM((8, S5_TILE_STATES), F32), pltpu.VMEM((8, S5_TILE_STATES), F32)],
        compiler_params=_cparams(("parallel", "parallel", "arbitrary")),
        name="s5_scan",
    )(u, bblk_re, bblk_im, cblk_re, cblk_im, apow_re, apow_im, d_skip.reshape(1, S5_WIDTH))
    return mm_glu(y, w_glu.astype(BF16), b_glu)


def _rope_tables(pos):
    inv_freq = ROPE_THETA ** (-jnp.arange(ROPE_HALF, dtype=F32) / ROPE_HALF)
    ang = pos.astype(F32)[:, None] * inv_freq[None, :]
    cos, sin = jnp.cos(ang), jnp.sin(ang)
    n = pos.shape[0]
    rest = HEAD_DIM - ROPE_DIMS
    c = jnp.concatenate([cos, cos, jnp.ones((n, rest), F32)], axis=1)
    s1 = jnp.concatenate([-sin, jnp.zeros((n, HEAD_DIM - ROPE_HALF), F32)], axis=1)
    s2 = jnp.concatenate([jnp.zeros((n, ROPE_HALF), F32), sin, jnp.zeros((n, rest), F32)], axis=1)
    return c, s1, s2


def _norm_rope(x, gain, c, s1, s2):
    x = x * lax.rsqrt(jnp.mean(x * x, axis=-1, keepdims=True) + NORM_EPS) * gain
    return (x * c + pltpu.roll(x, HEAD_DIM - ROPE_HALF, 1) * s1 + pltpu.roll(x, ROPE_HALF, 1) * s2)


def _kv_prep_body(ks_ref, vs_ref, kw_ref, vw_ref, kn_ref, c_ref, s1_ref, s2_ref,
                  oks_ref, ovs_ref, okw_ref, ovw_ref):
    c, s1, s2 = c_ref[...], s1_ref[...], s2_ref[...]
    oks_ref[...] = _norm_rope(ks_ref[...], kn_ref[1:2, :], c, s1, s2).astype(BF16)
    okw_ref[...] = _norm_rope(kw_ref[...], kn_ref[2:3, :], c, s1, s2).astype(BF16)
    ovs_ref[...] = vs_ref[...].T.astype(BF16)
    ovw_ref[...] = vw_ref[...].T.astype(BF16)


def _cmp_prep_body(k_ref, v_ref, pos_ref, w1_ref, w2_ref, kn_ref, c_ref, s1_ref, s2_ref,
                   ok_ref, ov_ref, *, ncp):
    half = CMP_LEN // 2
    outs = []
    for which, t_ref in enumerate((k_ref, v_ref)):
        lo = jnp.zeros((ncp, HEAD_DIM), F32)
        hi = jnp.zeros((ncp, HEAD_DIM), F32)
        for j in range(half):
            tj = t_ref[pl.ds(j, ncp, stride=CMP_STRIDE), :]
            a = (tj + pos_ref[which, j:j + 1, :]).astype(BF16)
            b = (tj + pos_ref[which, half + j:half + j + 1, :]).astype(BF16)
            lo = lo + _dot(a, w1_ref[which, j])
            hi = hi + _dot(b, w1_ref[which, half + j])
        pre = lo + pltpu.roll(hi, ncp - 1, 0)
        outs.append(_dot(jax.nn.gelu(pre).astype(BF16), w2_ref[which]))
    valid = lax.broadcasted_iota(jnp.int32, (ncp, HEAD_DIM), 0) < ncp - 1
    kc = _norm_rope(outs[0], kn_ref[0:1, :], c_ref[...], s1_ref[...], s2_ref[...])
    ok_ref[...] = jnp.where(valid, kc, 0.0).astype(BF16)
    ov_ref[...] = jnp.where(valid, outs[1], 0.0).T.astype(BF16)


def _attn_body(q_ref, gate_ref, c_ref, s1_ref, s2_ref, qn_ref, kc_ref, vct_ref,
               ks_ref, vst_ref, kw_ref, vwt_ref, ovt_ref, o_ref, bias_ref, *, tq, ncp):
    s0 = pl.program_id(2) * tq
    cols = GROUP * tq
    qpos = s0 + (lax.broadcasted_iota(jnp.int32, (1, cols), 1) & (tq - 1))
    qpos1 = qpos[:, :tq]
    c, s1, s2 = c_ref[...], s1_ref[...], s2_ref[...]
    qn = qn_ref[...]
    q_t = jnp.concatenate(
        [(_norm_rope(q_ref[:, g * HEAD_DIM:(g + 1) * HEAD_DIM], qn, c, s1, s2) * ATT_SCALE).T.astype(BF16)
         for g in range(GROUP)], axis=1)

    sc = _dot(kc_ref[...], q_t)
    n_id = lax.broadcasted_iota(jnp.int32, (ncp, 1), 0)
    cmask = n_id * CMP_STRIDE + (CMP_LEN - 1) <= qpos
    sc = jnp.where(cmask, sc, NEG)
    mx = jnp.max(sc, axis=0, keepdims=True)
    e = jnp.where(cmask, jnp.exp2(sc - mx), 0.0)
    p = e / jnp.maximum(jnp.sum(e, axis=0, keepdims=True), 1e-30)
    o_cmp = _dot(vct_ref[...], p.astype(BF16))

    pg = p[:, 0:tq]
    for g in range(1, GROUP):
        pg = pg + p[:, g * tq:(g + 1) * tq]
    p_hi = pg.astype(BF16)
    p_lo = (pg - p_hi.astype(F32)).astype(BF16)
    imp = _dot(ovt_ref[...], p_hi) + _dot(ovt_ref[...], p_lo)
    blk = lax.broadcasted_iota(jnp.int32, (LANES, tq), 0)
    cur = lax.shift_right_logical(qpos1, 6)
    forced = (blk == 0) | (blk == cur) | (blk == cur - 1)
    val = jnp.where(forced, 1e9, imp)
    val = jnp.where(blk <= cur, val, -1e9)
    blk_f = blk.astype(F32)
    sel = jnp.zeros((LANES, tq), F32)
    for _ in range(SEL_TOPN):
        top = jnp.max(val, axis=0, keepdims=True)
        idx = jnp.min(jnp.where(val == top, blk_f, float(LANES)), axis=0, keepdims=True)
        hit = blk_f == idx
        val = jnp.where(hit, -jnp.inf, val)
        sel = jnp.where(hit, 1.0, sel)
    bias_ref[...] = (sel - 1.0) * (-NEG)

    blocks_per_tile = ATT_TK // SEL_LEN

    def sel_tile(kt, carry, diagonal):
        m, l, acc = carry
        k0 = pl.multiple_of(kt * ATT_TK, ATT_TK)
        s = _dot(ks_ref[pl.ds(k0, ATT_TK), :], q_t)
        b8 = bias_ref[pl.ds(pl.multiple_of(kt * blocks_per_tile, blocks_per_tile), blocks_per_tile), :]
        bias = jnp.concatenate([jnp.broadcast_to(b8[j:j + 1, :], (SEL_LEN, tq))
                                for j in range(blocks_per_tile)], axis=0)
        if diagonal:
            kpos = k0 + lax.broadcasted_iota(jnp.int32, (ATT_TK, 1), 0)
            bias = jnp.where(kpos <= qpos1, bias, NEG)
        s = s + jnp.concatenate([bias] * GROUP, axis=1)
        m_new = jnp.maximum(m, jnp.max(s, axis=0, keepdims=True))
        alpha = jnp.exp2(m - m_new)
        pe = jnp.exp2(s - m_new)
        l = alpha * l + jnp.sum(pe, axis=0, keepdims=True)
        pv = _dot(vst_ref[:, pl.ds(k0, ATT_TK)], pe.astype(BF16))
        return m_new, l, alpha * acc + pv

    n_full = lax.div(s0, ATT_TK)
    init = (jnp.full((1, cols), NEG, F32), jnp.zeros((1, cols), F32), jnp.zeros((HEAD_DIM, cols), F32))
    carry = lax.fori_loop(0, n_full, lambda kt, cy: sel_tile(kt, cy, False), init)
    _, l_sel, acc_sel = sel_tile(n_full, carry, True)
    o_sel = acc_sel / l_sel

    span = WINDOW + tq
    w0 = pl.multiple_of(jnp.maximum(s0 - WINDOW, 0), tq)
    sw = _dot(kw_ref[pl.ds(w0, span), :], q_t)
    kp = w0 + lax.broadcasted_iota(jnp.int32, (span, 1), 0)
    wmask = (kp <= qpos) & (kp > qpos - WINDOW)
    sw = jnp.where(wmask, sw, NEG)
    mw = jnp.max(sw, axis=0, keepdims=True)
    ew = jnp.exp2(sw - mw)
    pw = ew / jnp.sum(ew, axis=0, keepdims=True)
    o_win = _dot(vwt_ref[:, pl.ds(w0, span)], pw.astype(BF16))

    gates = jax.nn.sigmoid(gate_ref[...]).T
    for g in range(GROUP):
        cs = slice(g * tq, (g + 1) * tq)
        r = g * N_BRANCH
        out = (gates[r:r + 1, :] * o_cmp[:, cs] + gates[r + 1:r + 2, :] * o_sel[:, cs]
               + gates[r + 2:r + 3, :] * o_win[:, cs])
        o_ref[:, g * HEAD_DIM:(g + 1) * HEAD_DIM] = out.T.astype(o_ref.dtype)


def nsa_mixer(q_raw, kv_raw, gate_raw, batch, q_norm, k_norm, cmp_pos, cmp_w1, cmp_w2):
    m = q_raw.shape[0]
    seq = m // batch
    ncp = seq // CMP_STRIDE
    nsel = seq // SEL_LEN
    assert seq % ATT_TK == 0 and nsel <= LANES and seq >= WINDOW + ATT_TQ
    pos = jnp.arange(seq)
    c, s1, s2 = _rope_tables(pos)
    cc, cs1, cs2 = _rope_tables(jnp.arange(ncp) * CMP_STRIDE + CMP_LEN - 1)
    col = lambda branch, kv: (branch * 2 + kv) * KV_HEADS

    tl = _pick(seq, (1024, 512))
    nl = seq // tl
    kvspec = lambda base: pl.BlockSpec((tl, HEAD_DIM), lambda b, h, t: (b * nl + t, base + h))
    tab = pl.BlockSpec((tl, HEAD_DIM), lambda b, h, t: (t, 0))
    outspec = pl.BlockSpec((None, None, tl, HEAD_DIM), lambda b, h, t: (b, h, t, 0))
    outspec_t = pl.BlockSpec((None, None, HEAD_DIM, tl), lambda b, h, t: (b, h, 0, t))
    kv_shape = jax.ShapeDtypeStruct((batch, KV_HEADS, seq, HEAD_DIM), BF16)
    kv_shape_t = jax.ShapeDtypeStruct((batch, KV_HEADS, HEAD_DIM, seq), BF16)
    k_s, v_s, k_w, v_w = pl.pallas_call(
        _kv_prep_body,
        out_shape=(kv_shape, kv_shape_t, kv_shape, kv_shape_t),
        grid=(batch, KV_HEADS, nl),
        in_specs=[kvspec(col(1, 0)), kvspec(col(1, 1)), kvspec(col(2, 0)), kvspec(col(2, 1)),
                  pl.BlockSpec((N_BRANCH, HEAD_DIM), lambda b, h, t: (0, 0)), tab, tab, tab],
        out_specs=(outspec, outspec_t, outspec, outspec_t),
        compiler_params=_cparams(("parallel", "parallel", "parallel")),
        name="nsa_kv_prep",
    )(kv_raw, kv_raw, kv_raw, kv_raw, k_norm, c, s1, s2)

    w1 = cmp_w1.reshape(2, CMP_LEN, HEAD_DIM, HEAD_DIM).astype(BF16)
    full = lambda shape: pl.BlockSpec(shape, lambda b, h: (0,) * len(shape))
    cshape = jax.ShapeDtypeStruct((batch, KV_HEADS, ncp, HEAD_DIM), BF16)
    cspec = pl.BlockSpec((None, None, ncp, HEAD_DIM), lambda b, h: (b, h, 0, 0))
    cshape_t = jax.ShapeDtypeStruct((batch, KV_HEADS, HEAD_DIM, ncp), BF16)
    cspec_t = pl.BlockSpec((None, None, HEAD_DIM, ncp), lambda b, h: (b, h, 0, 0))
    k_c, v_c = pl.pallas_call(
        functools.partial(_cmp_prep_body, ncp=ncp),
        out_shape=(cshape, cshape_t),
        grid=(batch, KV_HEADS),
        in_specs=[pl.BlockSpec((seq, HEAD_DIM), lambda b, h: (b, col(0, 0) + h)),
                  pl.BlockSpec((seq, HEAD_DIM), lambda b, h: (b, col(0, 1) + h)),
                  full((2, CMP_LEN, HEAD_DIM)), full((2, CMP_LEN, HEAD_DIM, HEAD_DIM)),
                  full((2, HEAD_DIM, HEAD_DIM)), full((N_BRANCH, HEAD_DIM)),
                  full((ncp, HEAD_DIM)), full((ncp, HEAD_DIM)), full((ncp, HEAD_DIM))],
        out_specs=(cspec, cspec_t),
        compiler_params=_cparams(("parallel", "parallel")),
        name="nsa_cmp_prep",
    )(kv_raw, kv_raw, cmp_pos, w1, cmp_w2.astype(BF16), k_norm, cc, cs1, cs2)

    cstart = jnp.arange(ncp)[:, None] * CMP_STRIDE
    sstart = jnp.arange(LANES)[None, :] * SEL_LEN
    overlap = jnp.maximum(jnp.minimum(cstart + CMP_LEN, sstart + SEL_LEN) - jnp.maximum(cstart, sstart), 0)
    overlap = jnp.where(jnp.arange(LANES)[None, :] < nsel, overlap, 0)
    overlap_t = (overlap.astype(F32) / CMP_STRIDE).astype(BF16).T

    tq = ATT_TQ
    nq = seq // tq
    qtab = pl.BlockSpec((tq, HEAD_DIM), lambda b, h, i: (i, 0))
    kvfull = lambda n: pl.BlockSpec((None, None, n, HEAD_DIM), lambda b, h, i: (b, h, 0, 0))
    kvfull_t = lambda n: pl.BlockSpec((None, None, HEAD_DIM, n), lambda b, h, i: (b, h, 0, 0))
    return pl.pallas_call(
        functools.partial(_attn_body, tq=tq, ncp=ncp),
        out_shape=jax.ShapeDtypeStruct((m, HEADS * HEAD_DIM), BF16),
        grid=(batch, KV_HEADS, nq),
        in_specs=[pl.BlockSpec((tq, GROUP * HEAD_DIM), lambda b, h, i: (b * nq + i, h)),
                  pl.BlockSpec((tq, LANES), lambda b, h, i: (b * nq + i, h)),
                  qtab, qtab, qtab,
                  pl.BlockSpec((1, HEAD_DIM), lambda b, h, i: (0, 0)),
                  kvfull(ncp), kvfull_t(ncp), kvfull(seq), kvfull_t(seq), kvfull(seq), kvfull_t(seq),
                  pl.BlockSpec((LANES, ncp), lambda b, h, i: (0, 0))],
        out_specs=pl.BlockSpec((tq, GROUP * HEAD_DIM), lambda b, h, i: (b * nq + i, h)),
        scratch_shapes=[pltpu.VMEM((LANES, tq), F32)],
        compiler_params=_cparams(("parallel", "parallel", "arbitrary")),
        name="nsa_attn",
    )(q_raw, gate_raw, c, s1, s2, q_norm.reshape(1, HEAD_DIM), k_c, v_c, k_s, v_s, k_w, v_w, overlap_t)


def _router_body(x_ref, g_ref, whi_ref, wlo_ref, b_ref, h_ref, r_ref):
    x = x_ref[...]
    h = x * lax.rsqrt(jnp.mean(x * x, axis=-1, keepdims=True) + NORM_EPS) * g_ref[...]
    h_ref[...] = h
    h_hi = h.astype(BF16)
    h_lo = (h - h_hi.astype(F32)).astype(BF16)
    logits = (_dot(h_hi, whi_ref[...]) + _dot(h_lo, whi_ref[...]) + _dot(h_hi, wlo_ref[...])
              + b_ref[...])
    lane = lax.broadcasted_iota(jnp.int32, logits.shape, 1)
    lane_f = lane.astype(F32)
    none = float(LANES)
    is_g = lane < N_GROUPS
    glog = jnp.where(is_g, logits, -jnp.inf)
    gmax = jnp.max(glog, axis=-1, keepdims=True)
    gidx = jnp.min(jnp.where(glog == gmax, lane_f, none), axis=-1, keepdims=True)
    g_p = 1.0 / jnp.sum(jnp.where(is_g, jnp.exp(logits - gmax), 0.0), axis=-1, keepdims=True)
    e_lo = N_GROUPS + gidx * PER_GROUP
    elog = jnp.where((lane_f >= e_lo) & (lane_f < e_lo + PER_GROUP), logits, -jnp.inf)
    m1 = jnp.max(elog, axis=-1, keepdims=True)
    i1 = jnp.min(jnp.where(elog == m1, lane_f, none), axis=-1, keepdims=True)
    elog = jnp.where(lane_f == i1, -jnp.inf, elog)
    m2 = jnp.max(elog, axis=-1, keepdims=True)
    i2 = jnp.min(jnp.where(elog == m2, lane_f, none), axis=-1, keepdims=True)
    t = jnp.exp(m2 - m1)
    w1 = g_p / (1.0 + t)
    w2 = g_p * t / (1.0 + t)
    r_ref[...] = jnp.where(lane == 0, i1 - N_GROUPS,
                           jnp.where(lane == 1, i2 - N_GROUPS,
                                     jnp.where(lane == 2, w1, jnp.where(lane == 3, w2, 0.0))))


def moe_router(x, norm, rg_w, rg_b, re_w, re_b, tm=256):
    m, d = x.shape
    pad = LANES - N_GROUPS - N_EXPERTS
    wr = jnp.pad(jnp.concatenate([rg_w, re_w], axis=1).astype(F32), ((0, 0), (0, pad)))
    whi = wr.astype(BF16)
    wlo = (wr - whi.astype(F32)).astype(BF16)
    bias = jnp.pad(jnp.concatenate([rg_b, re_b]).astype(F32), (0, pad)).reshape(1, LANES)
    const = lambda shape: pl.BlockSpec(shape, lambda i: (0, 0))
    return pl.pallas_call(
        _router_body,
        out_shape=(jax.ShapeDtypeStruct((m, d), F32), jax.ShapeDtypeStruct((m, LANES), F32)),
        grid=(m // tm,),
        in_specs=[pl.BlockSpec((tm, d), lambda i: (i, 0)), const((1, d)),
                  const((d, LANES)), const((d, LANES)), const((1, LANES))],
        out_specs=(pl.BlockSpec((tm, d), lambda i: (i, 0)), pl.BlockSpec((tm, LANES), lambda i: (i, 0))),
        compiler_params=_cparams(("parallel",)),
        name="moe_router",
    )(x, norm.reshape(1, d), whi, wlo, bias)


def _rank_body(e_ref, rank_ref, cnt_ref, carry_ref):
    @pl.when(pl.program_id(0) == 0)
    def _():
        carry_ref[...] = jnp.zeros_like(carry_ref)

    e = e_ref[0]
    sub = lax.broadcasted_iota(jnp.int32, (N_EXPERTS, RANK_T), 0)
    onehot = jnp.where(sub == e, 1.0, 0.0)
    r = lax.broadcasted_iota(jnp.int32, (RANK_T, RANK_T), 0)
    c = lax.broadcasted_iota(jnp.int32, (RANK_T, RANK_T), 1)
    before = jnp.where(r < c, 1.0, 0.0).astype(BF16)
    prefix = _dot(onehot.astype(BF16), before)
    carry = carry_ref[:, 0:1]
    rank = jnp.sum(onehot * (prefix + carry), axis=0, keepdims=True)
    rank_ref[0] = rank.astype(jnp.int32)
    carry_ref[...] = carry_ref[...] + jnp.sum(onehot, axis=1, keepdims=True)
    cnt_ref[...] = carry_ref[...]


def moe_rank(flat_e):
    a = flat_e.shape[0]
    nt = a // RANK_T
    rank, cnt = pl.pallas_call(
        _rank_body,
        out_shape=(jax.ShapeDtypeStruct((nt, 1, RANK_T), jnp.int32),
                   jax.ShapeDtypeStruct((N_EXPERTS, LANES), F32)),
        grid=(nt,),
        in_specs=[pl.BlockSpec((1, 1, RANK_T), lambda i: (i, 0, 0))],
        out_specs=(pl.BlockSpec((1, 1, RANK_T), lambda i: (i, 0, 0)),
                   pl.BlockSpec((N_EXPERTS, LANES), lambda i: (0, 0))),
        scratch_shapes=[pltpu.VMEM((N_EXPERTS, LANES), F32)],
        compiler_params=_cparams(("arbitrary",)),
        name="moe_rank",
    )(flat_e.reshape(nt, 1, RANK_T))
    return rank.reshape(a), cnt[:, 0].astype(jnp.int32)


def _row_copy(src, s, dst, d, sem):
    return pltpu.make_async_copy(src.at[pl.ds(s, 1), :], dst.at[pl.ds(d, 1), :], sem)


def _scatter_body(dest_ref, h_ref, xs_in_ref, xs_ref, sem):
    del xs_in_ref
    base = pl.program_id(0) * (MOE_TT * TOP_K)

    def start(r, carry):
        for k in range(TOP_K):
            _row_copy(h_ref, r, xs_ref, dest_ref[base + r * TOP_K + k], sem).start()
        return carry

    def wait(r, carry):
        for k in range(TOP_K):
            _row_copy(h_ref, 0, xs_ref, 0, sem).wait()
        return carry

    lax.fori_loop(0, MOE_TT, start, 0)
    lax.fori_loop(0, MOE_TT, wait, 0)


def moe_scatter(h, dest, n_rows):
    m, d = h.shape
    return pl.pallas_call(
        _scatter_body,
        out_shape=jax.ShapeDtypeStruct((n_rows, d), h.dtype),
        grid_spec=pltpu.PrefetchScalarGridSpec(
            num_scalar_prefetch=1,
            grid=(m // MOE_TT,),
            in_specs=[pl.BlockSpec((MOE_TT, d), lambda i, dest: (i, 0)),
                      pl.BlockSpec(memory_space=pl.ANY)],
            out_specs=pl.BlockSpec(memory_space=pl.ANY),
            scratch_shapes=[pltpu.SemaphoreType.DMA(())]),
        input_output_aliases={2: 0},
        compiler_params=_cparams(("arbitrary",)),
        name="moe_scatter",
    )(dest, h, jnp.zeros((n_rows, d), h.dtype))


def _moe_mm_body(blk_e_ref, nblk_ref, x_ref, w1_ref, w3_ref, w2_ref, y_ref):
    del blk_e_ref
    used = pl.program_id(0) < nblk_ref[0]

    @pl.when(used)
    def _():
        xb = x_ref[...].astype(BF16)
        hb = (jax.nn.silu(_dot(xb, w1_ref[0])) * _dot(xb, w3_ref[0])).astype(BF16)
        y_ref[...] = _dot(hb, w2_ref[0])

    @pl.when(jnp.logical_not(used))
    def _():
        y_ref[...] = jnp.zeros_like(y_ref)


def moe_experts(xs, blk_e, nblk, w1, w3, w2):
    n_rows, d = xs.shape
    ff = w1.shape[2]
    wspec = lambda r, c: pl.BlockSpec((1, r, c), lambda j, be, nb: (be[j], 0, 0))
    return pl.pallas_call(
        _moe_mm_body,
        out_shape=jax.ShapeDtypeStruct((n_rows, d), F32),
        grid_spec=pltpu.PrefetchScalarGridSpec(
            num_scalar_prefetch=2,
            grid=(n_rows // MOE_ROWS,),
            in_specs=[pl.BlockSpec((MOE_ROWS, d), lambda j, be, nb: (j, 0)),
                      wspec(d, ff), wspec(d, ff), wspec(ff, d)],
            out_specs=pl.BlockSpec((MOE_ROWS, d), lambda j, be, nb: (j, 0))),
        compiler_params=_cparams(("arbitrary",)),
        name="moe_experts",
    )(blk_e, nblk, xs, w1, w3, w2)


def _combine_body(dest_ref, x_ref, r_ref, g_ref, y_ref, xo_ref, h_ref, buf, sem):
    i = pl.program_id(0)
    n = pl.num_programs(0)

    def start_step(step, slot):
        base = step * (MOE_TT * TOP_K)

        def body(r, carry):
            for k in range(TOP_K):
                _row_copy(y_ref, dest_ref[base + r * TOP_K + k], buf.at[slot], k * MOE_TT + r,
                          sem.at[slot]).start()
            return carry

        lax.fori_loop(0, MOE_TT, body, 0)

    @pl.when(i == 0)
    def _():
        start_step(0, 0)

    @pl.when(i + 1 < n)
    def _():
        start_step(i + 1, (i + 1) % 2)

    slot = i % 2

    def wait(r, carry):
        for k in range(TOP_K):
            _row_copy(y_ref, 0, buf.at[slot], 0, sem.at[slot]).wait()
        return carry

    lax.fori_loop(0, MOE_TT, wait, 0)
    r = r_ref[...]
    moe = r[:, 2:3] * buf[slot, 0:MOE_TT, :] + r[:, 3:4] * buf[slot, MOE_TT:2 * MOE_TT, :]
    x = x_ref[...] + moe
    xo_ref[...] = x
    hn = x * lax.rsqrt(jnp.mean(x * x, axis=-1, keepdims=True) + NORM_EPS) * g_ref[...]
    h_ref[...] = hn.astype(h_ref.dtype)


def moe_combine(x, route, y, dest, norm):
    m, d = x.shape
    tok = lambda c: pl.BlockSpec((MOE_TT, c), lambda i, dest: (i, 0))
    return pl.pallas_call(
        _combine_body,
        out_shape=(jax.ShapeDtypeStruct((m, d), F32), jax.ShapeDtypeStruct((m, d), BF16)),
        grid_spec=pltpu.PrefetchScalarGridSpec(
            num_scalar_prefetch=1,
            grid=(m // MOE_TT,),
            in_specs=[tok(d), tok(LANES), pl.BlockSpec((1, d), lambda i, dest: (0, 0)),
                      pl.BlockSpec(memory_space=pl.ANY)],
            out_specs=(tok(d), tok(d)),
            scratch_shapes=[pltpu.VMEM((2, TOP_K * MOE_TT, d), F32),
                            pltpu.SemaphoreType.DMA((2,))]),
        compiler_params=_cparams(("arbitrary",)),
        name="moe_combine",
    )(dest, x, route, norm.reshape(1, d), y)


def moe_ffn(x, norm2, rg_w, rg_b, re_w, re_b, w1, w3, w2, norm3):
    m, d = x.shape
    h, route = moe_router(x, norm2, rg_w, rg_b, re_w, re_b)
    flat_e = route[:, :TOP_K].astype(jnp.int32).reshape(m * TOP_K)
    rank, counts = moe_rank(flat_e)
    padded = (counts + MOE_ROWS - 1) // MOE_ROWS * MOE_ROWS
    pend = jnp.cumsum(padded)
    pstart = pend - padded
    dest = (pstart[flat_e] + rank).astype(jnp.int32)
    n_blocks = (m * TOP_K + N_EXPERTS * (MOE_ROWS - 1) + MOE_ROWS - 1) // MOE_ROWS
    nblk = (pend[-1:] // MOE_ROWS).astype(jnp.int32)
    starts = jnp.arange(n_blocks) * MOE_ROWS
    blk_e = jnp.minimum(jnp.sum(pend[None, :] <= starts[:, None], axis=1), N_EXPERTS - 1).astype(jnp.int32)
    xs = moe_scatter(h, dest, n_blocks * MOE_ROWS)
    y = moe_experts(xs, blk_e, nblk, w1.astype(BF16), w3.astype(BF16), w2.astype(BF16))
    return moe_combine(x, route, y, dest, norm3)


def kernel(x, p, norm1, w_in, q_norm, k_norm, cmp_pos, cmp_w1, cmp_w2, gm_norm, gm_ws, gm_bs,
           s5_lambda_re, s5_lambda_im, s5_log_dt, s5_b_re, s5_b_im, s5_c_re, s5_c_im, s5_d,
           s5_w_glu, s5_b_glu, w_branch, w_out, norm2, router_group_w, router_group_b,
           router_expert_w, router_expert_b, expert_w1, expert_w3, expert_w2, norm3, w_ple,
           w_ple_gate):
    b, l, d = x.shape
    m = b * l
    xf = x.reshape(m, d)
    q_cols = HEADS * HEAD_DIM
    kv_cols = N_BRANCH * 2 * KV_HEADS * HEAD_DIM
    ng_cols = HEADS * N_BRANCH
    splits = [0, q_cols, q_cols + kv_cols, q_cols + kv_cols + ng_cols]
    splits.append(splits[-1] + 2 * GM_WIDTH)
    splits.append(splits[-1] + S5_WIDTH)
    splits.append(splits[-1] + 3 * d)
    per_head = GROUP * N_BRANCH
    for i in range(p.shape[0]):
        h = rms_cast(xf, norm1[i])
        w = w_in[i]
        seg = lambda k: w[:, splits[k]:splits[k + 1]].astype(BF16)
        q_raw = mm_plain(h, seg(0), name="proj_q")
        kv_raw = mm_plain(h, seg(1), name="proj_kv")
        w_ng = jnp.pad(w[:, splits[2]:splits[3]].reshape(d, KV_HEADS, per_head),
                       ((0, 0), (0, 0), (0, LANES - per_head))).reshape(d, KV_HEADS * LANES)
        ng_raw = mm_plain(h, w_ng.astype(BF16), name="proj_gate")
        gm_raw = mm_plain(h, seg(3), name="proj_gm")
        s5_raw = mm_plain(h, seg(4), name="proj_s5")
        mg_raw = mm_plain(h, seg(5), name="proj_merge")

        o_gm = gmlp_mixer(gm_raw, gm_norm[i], gm_ws[i], gm_bs[i])
        o_s5 = s5_mixer(s5_raw, b, s5_lambda_re[i], s5_lambda_im[i], s5_log_dt[i], s5_b_re[i],
                        s5_b_im[i], s5_c_re[i], s5_c_im[i], s5_d[i], s5_w_glu[i], s5_b_glu[i])
        o_att = nsa_mixer(q_raw, kv_raw, ng_raw, b, q_norm[i], k_norm[i], cmp_pos[i], cmp_w1[i],
                          cmp_w2[i])
        wb = w_branch[i].astype(BF16)
        merged = mm_merge([o_gm, o_s5, o_att], [wb[0], wb[1], wb[2]], mg_raw)
        xf = mm_residual(merged, w_out[i].astype(BF16), xf)
        xf, h3 = moe_ffn(xf, norm2[i], router_group_w[i], router_group_b[i], router_expert_w[i],
                         router_expert_b[i], expert_w1[i], expert_w3[i], expert_w2[i], norm3[i])
        xf = mm_ple(h3, w_ple_gate[i].astype(BF16), p[i].reshape(m, -1).astype(BF16),
                    w_ple[i].astype(BF16), xf)
    return xf.reshape(b, l, d)
```

```python
import functools

import jax
import jax.numpy as jnp
from jax import lax
from jax.experimental import pallas as pl
from jax.experimental.pallas import tpu as pltpu

F32 = jnp.float32
BF16 = jnp.bfloat16

NORM_EPS = 1e-6
LANES = 128
VMEM_LIMIT = 52 * 1024 * 1024

HEADS = 16
KV_HEADS = 4
GROUP = HEADS // KV_HEADS
HEAD_DIM = 128
ROPE_THETA = 500000.0
ROPE_DIMS = HEAD_DIM // 4
ROPE_HALF = ROPE_DIMS // 2
CMP_LEN = 32
CMP_STRIDE = 16
SEL_LEN = 64
SEL_TOPN = 16
WINDOW = 512
N_BRANCH = 3
NEG = -1e30
ATT_SCALE = HEAD_DIM ** -0.5 * 1.4426950408889634
ATT_TQ = 128
ATT_TK = 1024
ONES_ROWS = 16

GM_WIDTH = 2048
GM_GROUPS = 16
GM_CHUNK = 128

S5_WIDTH = 2048
S5_GROUP_DIM = 16
S5_GROUPS = S5_WIDTH // S5_GROUP_DIM
S5_STATE = 64
S5_TILE_GROUPS = LANES // S5_GROUP_DIM
S5_TILE_STATES = S5_TILE_GROUPS * S5_STATE
S5_BLOCK = 8
S5_ROWS = 512

N_GROUPS = 4
PER_GROUP = 8
N_EXPERTS = N_GROUPS * PER_GROUP
TOP_K = 2
MOE_ROWS = 256
RANK_T = 512
MOE_TT = 64


def _cparams(sem):
    return pltpu.CompilerParams(dimension_semantics=sem, vmem_limit_bytes=VMEM_LIMIT)


def _dot(a, b):
    return jnp.dot(a, b, preferred_element_type=F32)


def _dot_nt(a, b):
    return lax.dot_general(a, b, (((1,), (1,)), ((), ())), preferred_element_type=F32)


def _rms_body(x_ref, g_ref, o_ref):
    x = x_ref[...]
    y = x * lax.rsqrt(jnp.mean(x * x, axis=-1, keepdims=True) + NORM_EPS)
    o_ref[...] = (y * g_ref[...]).astype(o_ref.dtype)


def rms_cast(x, g, tm=256):
    m, d = x.shape
    return pl.pallas_call(
        _rms_body,
        out_shape=jax.ShapeDtypeStruct((m, d), BF16),
        grid=(m // tm,),
        in_specs=[pl.BlockSpec((tm, d), lambda i: (i, 0)),
                  pl.BlockSpec((1, d), lambda i: (0, 0))],
        out_specs=pl.BlockSpec((tm, d), lambda i: (i, 0)),
        compiler_params=_cparams(("parallel",)),
        name="rms_cast",
    )(x, g.reshape(1, d))


def _mm_plain_body(a_ref, w_ref, o_ref):
    o_ref[...] = _dot(a_ref[...], w_ref[...]).astype(o_ref.dtype)


def _mm_glu_body(a_ref, w_ref, y_ref, b_ref, o_ref):
    acc = _dot(a_ref[...].astype(BF16), w_ref[...]) + b_ref[...]
    o_ref[...] = (y_ref[...] * jax.nn.sigmoid(acc)).astype(o_ref.dtype)


def _mm_res_body(a_ref, w_ref, x_ref, o_ref):
    o_ref[...] = x_ref[...] + _dot(a_ref[...], w_ref[...])


def _mm_merge_body(a0, a1, a2, w0, w1, w2, g0, g1, g2, o_ref):
    acc = jax.nn.sigmoid(g0[...]) * _dot(a0[...], w0[...])
    acc = acc + jax.nn.sigmoid(g1[...]) * _dot(a1[...], w1[...])
    acc = acc + jax.nn.sigmoid(g2[...]) * _dot(a2[...], w2[...])
    o_ref[...] = acc.astype(o_ref.dtype)


def _mm_ple_body(a_ref, w_ref, p_ref, wp_ref, x_ref, o_ref):
    gate = jax.nn.sigmoid(_dot(a_ref[...], w_ref[...]))
    o_ref[...] = x_ref[...] + _dot(p_ref[...], wp_ref[...]) * gate


def _lhs_spec(tm, k):
    return pl.BlockSpec((tm, k), lambda j, i: (i, 0))


def _rhs_spec(k, tn):
    return pl.BlockSpec((k, tn), lambda j, i: (0, j))


def _tile_spec(tm, tn, col_blocks=0):
    return pl.BlockSpec((tm, tn), lambda j, i: (i, j + col_blocks))


def _mm_call(body, args, in_specs, m, n, tm, tn, out_dtype, name):
    return pl.pallas_call(
        body,
        out_shape=jax.ShapeDtypeStruct((m, n), out_dtype),
        grid=(n // tn, m // tm),
        in_specs=in_specs,
        out_specs=pl.BlockSpec((tm, tn), lambda j, i: (i, j)),
        compiler_params=_cparams(("parallel", "parallel")),
        name=name,
    )(*args)


def _pick(n, prefs):
    for t in prefs:
        if n % t == 0:
            return t
    return n


def mm_plain(a, w, out_dtype=F32, name="mm"):
    m, k = a.shape
    n = w.shape[1]
    tm, tn = _pick(m, (512, 256, 128)), _pick(n, (1024, 512, 256, 128))
    return _mm_call(_mm_plain_body, (a, w), [_lhs_spec(tm, k), _rhs_spec(k, tn)],
                    m, n, tm, tn, out_dtype, name)


def mm_glu(y, w, b):
    m, k = y.shape
    n = w.shape[1]
    tm, tn = _pick(m, (512, 256, 128)), _pick(n, (1024, 512, 256, 128))
    specs = [_lhs_spec(tm, k), _rhs_spec(k, tn), _tile_spec(tm, tn),
             pl.BlockSpec((1, tn), lambda j, i: (0, j))]
    return _mm_call(_mm_glu_body, (y, w, y, b.reshape(1, n)), specs, m, n, tm, tn, BF16, "mm_glu")


def mm_residual(a, w, x):
    m, k = a.shape
    n = w.shape[1]
    tm, tn = _pick(m, (512, 256, 128)), _pick(n, (1024, 512, 256, 128))
    specs = [_lhs_spec(tm, k), _rhs_spec(k, tn), _tile_spec(tm, tn)]
    return _mm_call(_mm_res_body, (a, w, x), specs, m, n, tm, tn, F32, "mm_residual")


def mm_merge(o_list, w_list, mg_raw):
    m, k = o_list[0].shape
    n = w_list[0].shape[1]
    tm, tn = _pick(m, (512, 256, 128)), _pick(n, (512, 256, 128))
    specs = ([_lhs_spec(tm, k)] * 3 + [_rhs_spec(k, tn)] * 3
             + [_tile_spec(tm, tn, c * (n // tn)) for c in range(3)])
    return _mm_call(_mm_merge_body, (*o_list, *w_list, mg_raw, mg_raw, mg_raw), specs,
                    m, n, tm, tn, BF16, "mm_merge")


def mm_ple(h, wg, p, wp, x):
    m, k = h.shape
    n = wg.shape[1]
    kp = p.shape[1]
    tm, tn = _pick(m, (512, 256, 128)), _pick(n, (1024, 512, 256, 128))
    specs = [_lhs_spec(tm, k), _rhs_spec(k, tn), _lhs_spec(tm, kp), _rhs_spec(kp, tn),
             _tile_spec(tm, tn)]
    return _mm_call(_mm_ple_body, (h, wg, p, wp, x), specs, m, n, tm, tn, F32, "mm_ple")


def _gmlp_body(z_ref, gn_ref, w_ref, b_ref, o_ref, *, rows):
    z = jax.nn.gelu(z_ref[...])
    u = z[:, :GM_WIDTH]
    v = z[:, GM_WIDTH:]
    v = v * lax.rsqrt(jnp.mean(v * v, axis=-1, keepdims=True) + NORM_EPS) * gn_ref[...]
    vb = v.astype(BF16)
    for c in range(rows // GM_CHUNK):
        r0 = c * GM_CHUNK
        for g in range(GM_GROUPS):
            c0 = g * LANES
            s = _dot(w_ref[g], vb[r0:r0 + GM_CHUNK, c0:c0 + LANES]) + b_ref[:, c0:c0 + LANES]
            o_ref[r0:r0 + GM_CHUNK, c0:c0 + LANES] = (
                u[r0:r0 + GM_CHUNK, c0:c0 + LANES] * s).astype(o_ref.dtype)


def gmlp_mixer(z, gm_norm, ws, bs, rows=256):
    m = z.shape[0]
    tri = jnp.tril(jnp.ones((GM_CHUNK, GM_CHUNK), dtype=bool))
    w = jnp.where(tri[None], ws, 0.0).astype(BF16)
    bias = jnp.repeat(bs.T, LANES, axis=1)
    return pl.pallas_call(
        functools.partial(_gmlp_body, rows=rows),
        out_shape=jax.ShapeDtypeStruct((m, GM_WIDTH), BF16),
        grid=(m // rows,),
        in_specs=[pl.BlockSpec((rows, 2 * GM_WIDTH), lambda i: (i, 0)),
                  pl.BlockSpec((1, GM_WIDTH), lambda i: (0, 0)),
                  pl.BlockSpec((GM_GROUPS, GM_CHUNK, GM_CHUNK), lambda i: (0, 0, 0)),
                  pl.BlockSpec((GM_CHUNK, GM_WIDTH), lambda i: (0, 0))],
        out_specs=pl.BlockSpec((rows, GM_WIDTH), lambda i: (i, 0)),
        compiler_params=_cparams(("parallel",)),
        name="gmlp",
    )(z, gm_norm.reshape(1, GM_WIDTH), w, bias)


def _shift_rows(x, s):
    t = x.shape[0]
    if s % 8 == 0:
        return jnp.concatenate([jnp.zeros((s, x.shape[1]), x.dtype), x[:t - s]], axis=0)
    rolled = pltpu.roll(x, s, 0)
    row = lax.broadcasted_iota(jnp.int32, x.shape, 0)
    return jnp.where(row >= s, rolled, 0.0)


def _s5_body(u_ref, w_ref, q_ref, r_ref, are_ref, aim_ref, d_ref, o_ref, hre_ref, him_ref,
             *, rows, levels):
    @pl.when(pl.program_id(2) == 0)
    def _():
        hre_ref[...] = jnp.zeros_like(hre_ref)
        him_ref[...] = jnp.zeros_like(him_ref)

    us = [u_ref[pl.ds(i, rows, stride=S5_BLOCK), :] for i in range(S5_BLOCK)]
    ucat = jnp.concatenate([x.astype(BF16) for x in us], axis=1)
    s = _dot(ucat, q_ref[0])
    xr = s[:, :S5_TILE_STATES]
    xi = s[:, S5_TILE_STATES:]
    are = are_ref[0]
    aim = aim_ref[0]
    hr = hre_ref[0:1, :]
    hi = him_ref[0:1, :]
    a_r = are[0:1, :]
    a_i = aim[0:1, :]
    first = lax.broadcasted_iota(jnp.int32, xr.shape, 0) == 0
    xr = xr + jnp.where(first, a_r * hr - a_i * hi, 0.0)
    xi = xi + jnp.where(first, a_r * hi + a_i * hr, 0.0)
    for k in range(levels):
        sh = 1 << k
        cr = are[k:k + 1, :]
        ci = aim[k:k + 1, :]
        sr = _shift_rows(xr, sh)
        si = _shift_rows(xi, sh)
        xr, xi = xr + cr * sr - ci * si, xi + cr * si + ci * sr
    hre_ref[0:1, :] = xr[rows - 1:rows, :]
    him_ref[0:1, :] = xi[rows - 1:rows, :]
    xpr = _shift_rows(xr, 1) + jnp.where(first, hr, 0.0)
    xpi = _shift_rows(xi, 1) + jnp.where(first, hi, 0.0)
    xprev = jnp.concatenate([xpr, xpi], axis=1).astype(BF16)
    y = _dot(ucat, w_ref[0]) + _dot(xprev, r_ref[0])
    d = d_ref[...]
    for i in range(S5_BLOCK):
        yi = y[:, i * LANES:(i + 1) * LANES] + d * us[i]
        o_ref[pl.ds(i, rows, stride=S5_BLOCK), :] = jax.nn.gelu(yi)


def _s5_discretize(lam_re, lam_im, log_dt, b_re, b_im):
    dt = jnp.exp(log_dt.astype(F32))[:, None]
    lre = jnp.minimum(lam_re.astype(F32), -1e-4)
    lim = lam_im.astype(F32)
    mag = jnp.exp(lre * dt)
    a_re = mag * jnp.cos(lim * dt)
    a_im = mag * jnp.sin(lim * dt)
    den = lre * lre + lim * lim
    nr = a_re - 1.0
    f_re = (nr * lre + a_im * lim) / den
    f_im = (a_im * lre - nr * lim) / den
    br = b_re.astype(F32)
    bi = b_im.astype(F32)
    bb_re = f_re[..., None] * br - f_im[..., None] * bi
    bb_im = f_re[..., None] * bi + f_im[..., None] * br
    return a_re, a_im, bb_re, bb_im


def _s5_block_weights(a_re, a_im, bb_re, bb_im, c_re, c_im, levels):
    hp = lax.Precision.HIGHEST
    ng, ns = a_re.shape
    tb, gl, hg = S5_BLOCK, S5_TILE_GROUPS, S5_GROUP_DIM
    nt = ng // gl
    pr, pi = [jnp.ones_like(a_re)], [jnp.zeros_like(a_re)]
    for _ in range(tb):
        r, i = pr[-1], pi[-1]
        pr.append(r * a_re - i * a_im)
        pi.append(r * a_im + i * a_re)
    pw_re, pw_im = jnp.stack(pr), jnp.stack(pi)
    ab_re = pw_re[:tb, :, :, None] * bb_re[None] - pw_im[:tb, :, :, None] * bb_im[None]
    ab_im = pw_re[:tb, :, :, None] * bb_im[None] + pw_im[:tb, :, :, None] * bb_re[None]
    cr, ci = c_re.astype(F32), c_im.astype(F32)
    kern = (jnp.einsum('gcp,dgpk->dgck', cr, ab_re, precision=hp)
            - jnp.einsum('gcp,dgpk->dgck', ci, ab_im, precision=hp))
    eye = jnp.eye(gl, dtype=F32)
    lag = jnp.arange(tb)[None, :] - jnp.arange(tb)[:, None]
    kfull = jnp.where((lag >= 0)[:, :, None, None, None], kern[jnp.maximum(lag, 0)], 0.0)
    kfull = kfull.reshape(tb, tb, nt, gl, hg, hg)
    w = jnp.einsum('abtgck,hg->tahkbgc', kfull, eye).reshape(nt, tb * LANES, tb * LANES)
    q = jnp.stack([ab_re[::-1], ab_im[::-1]]).reshape(2, tb, nt, gl, ns, hg)
    q = jnp.einsum('ratgpk,hg->tahkrgp', q, eye).reshape(nt, tb * LANES, 2 * gl * ns)
    nr, ni = pw_re[1:, :, None, :], pw_im[1:, :, None, :]
    r = jnp.stack([cr[None] * nr - ci[None] * ni, -(cr[None] * ni + ci[None] * nr)])
    r = r.reshape(2, tb, nt, gl, hg, ns)
    r = jnp.einsum('ritgcp,hg->trhpigc', r, eye).reshape(nt, 2 * gl * ns, tb * LANES)
    sr, si = [pw_re[tb].reshape(-1)], [pw_im[tb].reshape(-1)]
    for _ in range(15):
        x, y = sr[-1], si[-1]
        sr.append(x * x - y * y)
        si.append(2.0 * x * y)
    assert levels <= 16
    scan_re = jnp.stack(sr).reshape(16, nt, gl * ns).transpose(1, 0, 2)
    scan_im = jnp.stack(si).reshape(16, nt, gl * ns).transpose(1, 0, 2)
    return w.astype(BF16), q.astype(BF16), r.astype(BF16), scan_re, scan_im


def s5_mixer(u, batch, lam_re, lam_im, log_dt, b_re, b_im, c_re, c_im, d_skip, w_glu, b_glu):
    m = u.shape[0]
    seq = m // batch
    nt = S5_GROUPS // S5_TILE_GROUPS
    nblk = seq // S5_BLOCK
    rows = min(S5_ROWS, nblk)
    levels = rows.bit_length() - 1
    assert rows == 1 << levels and nblk % rows == 0
    a_re, a_im, bb_re, bb_im = _s5_discretize(lam_re, lam_im, log_dt, b_re, b_im)
    w, q, r, scan_re, scan_im = _s5_block_weights(a_re, a_im, bb_re, bb_im, c_re, c_im, levels)

    nchunk = nblk // rows
    tok = rows * S5_BLOCK
    wspec = lambda rr, cc: pl.BlockSpec((1, rr, cc), lambda b, j, t: (j, 0, 0))
    kin = S5_BLOCK * LANES
    y = pl.pallas_call(
        functools.partial(_s5_body, rows=rows, levels=levels),
        out_shape=jax.ShapeDtypeStruct((m, S5_WIDTH), F32),
        grid=(batch, nt, nchunk),
        in_specs=[pl.BlockSpec((tok, LANES), lambda b, j, t: (b * nchunk + t, j)),
                  wspec(kin, kin), wspec(kin, 2 * S5_TILE_STATES), wspec(2 * S5_TILE_STATES, kin),
                  wspec(16, S5_TILE_STATES), wspec(16, S5_TILE_STATES),
                  pl.BlockSpec((1, LANES), lambda b, j, t: (0, j))],
        out_specs=pl.BlockSpec((tok, LANES), lambda b, j, t: (b * nchunk + t, j)),
        scratch_shapes=[pltpu.VMEM((8, S5_TILE_STATES), F32), pltpu.VMEM((8, S5_TILE_STATES), F32)],
        compiler_params=_cparams(("parallel", "parallel", "arbitrary")),
        name="s5_scan",
    )(u, w, q, r, scan_re, scan_im, d_skip.reshape(1, S5_WIDTH))
    return mm_glu(y, w_glu.astype(BF16), b_glu)


def _rope_tables(pos):
    inv_freq = ROPE_THETA ** (-jnp.arange(ROPE_HALF, dtype=F32) / ROPE_HALF)
    ang = pos.astype(F32)[:, None] * inv_freq[None, :]
    cos, sin = jnp.cos(ang), jnp.sin(ang)
    n = pos.shape[0]
    rest = HEAD_DIM - ROPE_DIMS
    c = jnp.concatenate([cos, cos, jnp.ones((n, rest), F32)], axis=1)
    s1 = jnp.concatenate([-sin, jnp.zeros((n, HEAD_DIM - ROPE_HALF), F32)], axis=1)
    s2 = jnp.concatenate([jnp.zeros((n, ROPE_HALF), F32), sin, jnp.zeros((n, rest), F32)], axis=1)
    return c, s1, s2


def _norm_rope(x, gain, c, s1, s2):
    x = x * lax.rsqrt(jnp.mean(x * x, axis=-1, keepdims=True) + NORM_EPS) * gain
    return (x * c + pltpu.roll(x, HEAD_DIM - ROPE_HALF, 1) * s1 + pltpu.roll(x, ROPE_HALF, 1) * s2)


def _kv_prep_body(ks_ref, vs_ref, kw_ref, vw_ref, kn_ref, c_ref, s1_ref, s2_ref,
                  oks_ref, ovs_ref, okw_ref, ovw_ref):
    c, s1, s2 = c_ref[...], s1_ref[...], s2_ref[...]
    oks_ref[...] = _norm_rope(ks_ref[...], kn_ref[1:2, :], c, s1, s2).astype(BF16)
    okw_ref[...] = _norm_rope(kw_ref[...], kn_ref[2:3, :], c, s1, s2).astype(BF16)
    vt = vs_ref[...].T
    ones = jnp.ones((ONES_ROWS, vt.shape[1]), F32)
    ovs_ref[...] = jnp.concatenate([vt, ones], axis=0).astype(BF16)
    ovw_ref[...] = jnp.concatenate([vw_ref[...].T, ones], axis=0).astype(BF16)


def _cmp_prep_body(k_ref, v_ref, pos_ref, w1_ref, w2_ref, kn_ref, c_ref, s1_ref, s2_ref,
                   ok_ref, ov_ref, *, ncp):
    half = CMP_LEN // 2
    outs = []
    for which, t_ref in enumerate((k_ref, v_ref)):
        lo = jnp.zeros((ncp, HEAD_DIM), F32)
        hi = jnp.zeros((ncp, HEAD_DIM), F32)
        for j in range(half):
            tj = t_ref[pl.ds(j, ncp, stride=CMP_STRIDE), :]
            a = (tj + pos_ref[which, j:j + 1, :]).astype(BF16)
            b = (tj + pos_ref[which, half + j:half + j + 1, :]).astype(BF16)
            lo = lo + _dot(a, w1_ref[which, j])
            hi = hi + _dot(b, w1_ref[which, half + j])
        pre = lo + pltpu.roll(hi, ncp - 1, 0)
        outs.append(_dot(jax.nn.gelu(pre).astype(BF16), w2_ref[which]))
    valid = lax.broadcasted_iota(jnp.int32, (ncp, HEAD_DIM), 0) < ncp - 1
    kc = _norm_rope(outs[0], kn_ref[0:1, :], c_ref[...], s1_ref[...], s2_ref[...])
    ok_ref[...] = jnp.where(valid, kc, 0.0).astype(BF16)
    ov_ref[...] = jnp.where(valid, outs[1], 0.0).T.astype(BF16)


def _attn_body(q_ref, gate_ref, c_ref, s1_ref, s2_ref, qn_ref, kc_ref, vct_ref,
               ks_ref, vst_ref, kw_ref, vwt_ref, ovt_ref, o_ref, bias_ref, s_ref, *, tq, ncp):
    s0 = pl.program_id(2) * tq
    cols = GROUP * tq
    qpos = s0 + (lax.broadcasted_iota(jnp.int32, (1, cols), 1) & (tq - 1))
    qpos1 = qpos[:, :tq]
    c, s1, s2 = c_ref[...], s1_ref[...], s2_ref[...]
    qn = qn_ref[...]
    q_t = jnp.concatenate(
        [(_norm_rope(q_ref[:, g * HEAD_DIM:(g + 1) * HEAD_DIM], qn, c, s1, s2) * ATT_SCALE).T.astype(BF16)
         for g in range(GROUP)], axis=1)

    n_id = lax.broadcasted_iota(jnp.int32, (ncp, 1), 0)
    cbias = jnp.where(n_id * CMP_STRIDE + (CMP_LEN - 1) <= qpos1, 0.0, NEG)
    sc = _dot(kc_ref[...], q_t) + jnp.concatenate([cbias] * GROUP, axis=1)
    mx = jnp.max(sc, axis=0, keepdims=True)
    e = jnp.exp2(sc - mx)
    inv = jnp.where(qpos >= CMP_LEN - 1, 1.0 / jnp.sum(e, axis=0, keepdims=True), 0.0)
    p = e * inv
    o_cmp = _dot(vct_ref[...], p.astype(BF16))

    pg = p[:, 0:tq]
    for g in range(1, GROUP):
        pg = pg + p[:, g * tq:(g + 1) * tq]
    p_hi = pg.astype(BF16)
    p_lo = (pg - p_hi.astype(F32)).astype(BF16)
    imp = _dot(ovt_ref[...], p_hi) + _dot(ovt_ref[...], p_lo)
    blk = lax.broadcasted_iota(jnp.int32, (LANES, tq), 0)
    cur = lax.shift_right_logical(qpos1, 6)
    forced = (blk == 0) | (blk == cur) | (blk == cur - 1)
    val = jnp.where(forced, -jnp.inf, jnp.where(blk <= cur, imp, -1e9))
    blk_f = blk.astype(F32)
    sel = jnp.where(forced, 1.0, 0.0)
    for _ in range(SEL_TOPN - 3):
        top = jnp.max(val, axis=0, keepdims=True)
        idx = jnp.min(jnp.where(val == top, blk_f, float(LANES)), axis=0, keepdims=True)
        hit = blk_f == idx
        val = jnp.where(hit, -jnp.inf, val)
        sel = jnp.where(hit, 1.0, sel)
    bias_ref[...] = (sel - 1.0) * (-NEG)

    blocks_per_tile = ATT_TK // SEL_LEN

    def score_tile(kt, m8, diagonal):
        k0 = pl.multiple_of(kt * ATT_TK, ATT_TK)
        s = _dot(ks_ref[pl.ds(k0, ATT_TK), :], q_t)
        b8 = bias_ref[pl.ds(pl.multiple_of(kt * blocks_per_tile, blocks_per_tile), blocks_per_tile), :]
        bias = jnp.concatenate([jnp.broadcast_to(b8[j:j + 1, :], (SEL_LEN, tq))
                                for j in range(blocks_per_tile)], axis=0)
        if diagonal:
            kpos = k0 + lax.broadcasted_iota(jnp.int32, (ATT_TK, 1), 0)
            bias = jnp.where(kpos <= qpos1, bias, NEG)
        s = s + jnp.concatenate([bias] * GROUP, axis=1)
        s_ref[pl.ds(k0, ATT_TK), :] = s
        return jnp.maximum(m8, jnp.max(s.reshape(ATT_TK // 8, 8, cols), axis=0))

    n_full = lax.div(s0, ATT_TK)
    m8 = lax.fori_loop(0, n_full, lambda kt, mm: score_tile(kt, mm, False), jnp.full((8, cols), NEG, F32))
    m_sel = jnp.max(score_tile(n_full, m8, True), axis=0, keepdims=True)

    def value_tile(kt, acc):
        k0 = pl.multiple_of(kt * ATT_TK, ATT_TK)
        pe = jnp.exp2(s_ref[pl.ds(k0, ATT_TK), :] - m_sel).astype(BF16)
        return acc + _dot(vst_ref[:, pl.ds(k0, ATT_TK)], pe)

    acc_sel = lax.fori_loop(0, n_full + 1, value_tile, jnp.zeros((HEAD_DIM + ONES_ROWS, cols), F32))
    o_sel = acc_sel[:HEAD_DIM] / acc_sel[HEAD_DIM:HEAD_DIM + 1]

    span = WINDOW + tq
    w0 = pl.multiple_of(jnp.maximum(s0 - WINDOW, 0), tq)
    kp = w0 + lax.broadcasted_iota(jnp.int32, (span, 1), 0)
    wbias = jnp.where((kp <= qpos1) & (kp > qpos1 - WINDOW), 0.0, NEG)
    sw = _dot(kw_ref[pl.ds(w0, span), :], q_t) + jnp.concatenate([wbias] * GROUP, axis=1)
    ew = jnp.exp2(sw - jnp.max(sw, axis=0, keepdims=True))
    acc_win = _dot(vwt_ref[:, pl.ds(w0, span)], ew.astype(BF16))
    o_win = acc_win[:HEAD_DIM] / acc_win[HEAD_DIM:HEAD_DIM + 1]

    gates = jax.nn.sigmoid(gate_ref[...]).T
    for g in range(GROUP):
        cs = slice(g * tq, (g + 1) * tq)
        r = g * N_BRANCH
        out = (gates[r:r + 1, :] * o_cmp[:, cs] + gates[r + 1:r + 2, :] * o_sel[:, cs]
               + gates[r + 2:r + 3, :] * o_win[:, cs])
        o_ref[:, g * HEAD_DIM:(g + 1) * HEAD_DIM] = out.T.astype(o_ref.dtype)


def nsa_mixer(q_raw, kv_raw, gate_raw, batch, q_norm, k_norm, cmp_pos, cmp_w1, cmp_w2):
    m = q_raw.shape[0]
    seq = m // batch
    ncp = seq // CMP_STRIDE
    nsel = seq // SEL_LEN
    assert seq % ATT_TK == 0 and nsel <= LANES and seq >= WINDOW + ATT_TQ
    pos = jnp.arange(seq)
    c, s1, s2 = _rope_tables(pos)
    cc, cs1, cs2 = _rope_tables(jnp.arange(ncp) * CMP_STRIDE + CMP_LEN - 1)
    col = lambda branch, kv: (branch * 2 + kv) * KV_HEADS

    tl = _pick(seq, (1024, 512))
    nl = seq // tl
    kvspec = lambda base: pl.BlockSpec((tl, HEAD_DIM), lambda b, h, t: (b * nl + t, base + h))
    tab = pl.BlockSpec((tl, HEAD_DIM), lambda b, h, t: (t, 0))
    outspec = pl.BlockSpec((None, None, tl, HEAD_DIM), lambda b, h, t: (b, h, t, 0))
    outspec_t = pl.BlockSpec((None, None, HEAD_DIM, tl), lambda b, h, t: (b, h, 0, t))
    kv_shape = jax.ShapeDtypeStruct((batch, KV_HEADS, seq, HEAD_DIM), BF16)
    kv_shape_t = jax.ShapeDtypeStruct((batch, KV_HEADS, HEAD_DIM, seq), BF16)
    vrows = HEAD_DIM + ONES_ROWS
    outspec_t1 = pl.BlockSpec((None, None, vrows, tl), lambda b, h, t: (b, h, 0, t))
    kv_shape_t1 = jax.ShapeDtypeStruct((batch, KV_HEADS, vrows, seq), BF16)
    k_s, v_s, k_w, v_w = pl.pallas_call(
        _kv_prep_body,
        out_shape=(kv_shape, kv_shape_t1, kv_shape, kv_shape_t1),
        grid=(batch, KV_HEADS, nl),
        in_specs=[kvspec(col(1, 0)), kvspec(col(1, 1)), kvspec(col(2, 0)), kvspec(col(2, 1)),
                  pl.BlockSpec((N_BRANCH, HEAD_DIM), lambda b, h, t: (0, 0)), tab, tab, tab],
        out_specs=(outspec, outspec_t1, outspec, outspec_t1),
        compiler_params=_cparams(("parallel", "parallel", "parallel")),
        name="nsa_kv_prep",
    )(kv_raw, kv_raw, kv_raw, kv_raw, k_norm, c, s1, s2)

    w1 = cmp_w1.reshape(2, CMP_LEN, HEAD_DIM, HEAD_DIM).astype(BF16)
    full = lambda shape: pl.BlockSpec(shape, lambda b, h: (0,) * len(shape))
    cshape = jax.ShapeDtypeStruct((batch, KV_HEADS, ncp, HEAD_DIM), BF16)
    cspec = pl.BlockSpec((None, None, ncp, HEAD_DIM), lambda b, h: (b, h, 0, 0))
    cshape_t = jax.ShapeDtypeStruct((batch, KV_HEADS, HEAD_DIM, ncp), BF16)
    cspec_t = pl.BlockSpec((None, None, HEAD_DIM, ncp), lambda b, h: (b, h, 0, 0))
    k_c, v_c = pl.pallas_call(
        functools.partial(_cmp_prep_body, ncp=ncp),
        out_shape=(cshape, cshape_t),
        grid=(batch, KV_HEADS),
        in_specs=[pl.BlockSpec((seq, HEAD_DIM), lambda b, h: (b, col(0, 0) + h)),
                  pl.BlockSpec((seq, HEAD_DIM), lambda b, h: (b, col(0, 1) + h)),
                  full((2, CMP_LEN, HEAD_DIM)), full((2, CMP_LEN, HEAD_DIM, HEAD_DIM)),
                  full((2, HEAD_DIM, HEAD_DIM)), full((N_BRANCH, HEAD_DIM)),
                  full((ncp, HEAD_DIM)), full((ncp, HEAD_DIM)), full((ncp, HEAD_DIM))],
        out_specs=(cspec, cspec_t),
        compiler_params=_cparams(("parallel", "parallel")),
        name="nsa_cmp_prep",
    )(kv_raw, kv_raw, cmp_pos, w1, cmp_w2.astype(BF16), k_norm, cc, cs1, cs2)

    cstart = jnp.arange(ncp)[:, None] * CMP_STRIDE
    sstart = jnp.arange(LANES)[None, :] * SEL_LEN
    overlap = jnp.maximum(jnp.minimum(cstart + CMP_LEN, sstart + SEL_LEN) - jnp.maximum(cstart, sstart), 0)
    overlap = jnp.where(jnp.arange(LANES)[None, :] < nsel, overlap, 0)
    overlap_t = (overlap.astype(F32) / CMP_STRIDE).astype(BF16).T

    tq = ATT_TQ
    nq = seq // tq
    qtab = pl.BlockSpec((tq, HEAD_DIM), lambda b, h, i: (i, 0))
    kvfull = lambda n: pl.BlockSpec((None, None, n, HEAD_DIM), lambda b, h, i: (b, h, 0, 0))
    kvfull_t = lambda n: pl.BlockSpec((None, None, HEAD_DIM, n), lambda b, h, i: (b, h, 0, 0))
    vfull_t = pl.BlockSpec((None, None, vrows, seq), lambda b, h, i: (b, h, 0, 0))
    return pl.pallas_call(
        functools.partial(_attn_body, tq=tq, ncp=ncp),
        out_shape=jax.ShapeDtypeStruct((m, HEADS * HEAD_DIM), BF16),
        grid=(batch, KV_HEADS, nq),
        in_specs=[pl.BlockSpec((tq, GROUP * HEAD_DIM), lambda b, h, i: (b * nq + i, h)),
                  pl.BlockSpec((tq, LANES), lambda b, h, i: (b * nq + i, h)),
                  qtab, qtab, qtab,
                  pl.BlockSpec((1, HEAD_DIM), lambda b, h, i: (0, 0)),
                  kvfull(ncp), kvfull_t(ncp), kvfull(seq), vfull_t, kvfull(seq), vfull_t,
                  pl.BlockSpec((LANES, ncp), lambda b, h, i: (0, 0))],
        out_specs=pl.BlockSpec((tq, GROUP * HEAD_DIM), lambda b, h, i: (b * nq + i, h)),
        scratch_shapes=[pltpu.VMEM((LANES, tq), F32), pltpu.VMEM((seq, GROUP * tq), F32)],
        compiler_params=_cparams(("parallel", "parallel", "arbitrary")),
        name="nsa_attn",
    )(q_raw, gate_raw, c, s1, s2, q_norm.reshape(1, HEAD_DIM), k_c, v_c, k_s, v_s, k_w, v_w, overlap_t)


def _router_body(x_ref, g_ref, whi_ref, wlo_ref, b_ref, h_ref, r_ref):
    x = x_ref[...]
    h = x * lax.rsqrt(jnp.mean(x * x, axis=-1, keepdims=True) + NORM_EPS) * g_ref[...]
    h_ref[...] = h
    h_hi = h.astype(BF16)
    h_lo = (h - h_hi.astype(F32)).astype(BF16)
    logits = (_dot(h_hi, whi_ref[...]) + _dot(h_lo, whi_ref[...]) + _dot(h_hi, wlo_ref[...])
              + b_ref[...])
    lane = lax.broadcasted_iota(jnp.int32, logits.shape, 1)
    lane_f = lane.astype(F32)
    none = float(LANES)
    is_g = lane < N_GROUPS
    glog = jnp.where(is_g, logits, -jnp.inf)
    gmax = jnp.max(glog, axis=-1, keepdims=True)
    gidx = jnp.min(jnp.where(glog == gmax, lane_f, none), axis=-1, keepdims=True)
    g_p = 1.0 / jnp.sum(jnp.where(is_g, jnp.exp(logits - gmax), 0.0), axis=-1, keepdims=True)
    e_lo = N_GROUPS + gidx * PER_GROUP
    elog = jnp.where((lane_f >= e_lo) & (lane_f < e_lo + PER_GROUP), logits, -jnp.inf)
    m1 = jnp.max(elog, axis=-1, keepdims=True)
    i1 = jnp.min(jnp.where(elog == m1, lane_f, none), axis=-1, keepdims=True)
    elog = jnp.where(lane_f == i1, -jnp.inf, elog)
    m2 = jnp.max(elog, axis=-1, keepdims=True)
    i2 = jnp.min(jnp.where(elog == m2, lane_f, none), axis=-1, keepdims=True)
    t = jnp.exp(m2 - m1)
    w1 = g_p / (1.0 + t)
    w2 = g_p * t / (1.0 + t)
    r_ref[...] = jnp.where(lane == 0, i1 - N_GROUPS,
                           jnp.where(lane == 1, i2 - N_GROUPS,
                                     jnp.where(lane == 2, w1, jnp.where(lane == 3, w2, 0.0))))


def moe_router(x, norm, rg_w, rg_b, re_w, re_b, tm=256):
    m, d = x.shape
    pad = LANES - N_GROUPS - N_EXPERTS
    wr = jnp.pad(jnp.concatenate([rg_w, re_w], axis=1).astype(F32), ((0, 0), (0, pad)))
    whi = wr.astype(BF16)
    wlo = (wr - whi.astype(F32)).astype(BF16)
    bias = jnp.pad(jnp.concatenate([rg_b, re_b]).astype(F32), (0, pad)).reshape(1, LANES)
    const = lambda shape: pl.BlockSpec(shape, lambda i: (0, 0))
    return pl.pallas_call(
        _router_body,
        out_shape=(jax.ShapeDtypeStruct((m, d), F32), jax.ShapeDtypeStruct((m, LANES), F32)),
        grid=(m // tm,),
        in_specs=[pl.BlockSpec((tm, d), lambda i: (i, 0)), const((1, d)),
                  const((d, LANES)), const((d, LANES)), const((1, LANES))],
        out_specs=(pl.BlockSpec((tm, d), lambda i: (i, 0)), pl.BlockSpec((tm, LANES), lambda i: (i, 0))),
        compiler_params=_cparams(("parallel",)),
        name="moe_router",
    )(x, norm.reshape(1, d), whi, wlo, bias)


def _rank_body(e_ref, rank_ref, cnt_ref, carry_ref):
    @pl.when(pl.program_id(0) == 0)
    def _():
        carry_ref[...] = jnp.zeros_like(carry_ref)

    e = e_ref[0]
    sub = lax.broadcasted_iota(jnp.int32, (N_EXPERTS, RANK_T), 0)
    onehot = jnp.where(sub == e, 1.0, 0.0)
    r = lax.broadcasted_iota(jnp.int32, (RANK_T, RANK_T), 0)
    c = lax.broadcasted_iota(jnp.int32, (RANK_T, RANK_T), 1)
    before = jnp.where(r < c, 1.0, 0.0).astype(BF16)
    prefix = _dot(onehot.astype(BF16), before)
    carry = carry_ref[:, 0:1]
    rank = jnp.sum(onehot * (prefix + carry), axis=0, keepdims=True)
    rank_ref[0] = rank.astype(jnp.int32)
    carry_ref[...] = carry_ref[...] + jnp.sum(onehot, axis=1, keepdims=True)
    cnt_ref[...] = carry_ref[...]


def moe_rank(flat_e):
    a = flat_e.shape[0]
    nt = a // RANK_T
    rank, cnt = pl.pallas_call(
        _rank_body,
        out_shape=(jax.ShapeDtypeStruct((nt, 1, RANK_T), jnp.int32),
                   jax.ShapeDtypeStruct((N_EXPERTS, LANES), F32)),
        grid=(nt,),
        in_specs=[pl.BlockSpec((1, 1, RANK_T), lambda i: (i, 0, 0))],
        out_specs=(pl.BlockSpec((1, 1, RANK_T), lambda i: (i, 0, 0)),
                   pl.BlockSpec((N_EXPERTS, LANES), lambda i: (0, 0))),
        scratch_shapes=[pltpu.VMEM((N_EXPERTS, LANES), F32)],
        compiler_params=_cparams(("arbitrary",)),
        name="moe_rank",
    )(flat_e.reshape(nt, 1, RANK_T))
    return rank.reshape(a), cnt[:, 0].astype(jnp.int32)


def _row_copy(src, s, dst, d, sem):
    return pltpu.make_async_copy(src.at[pl.ds(s, 1), :], dst.at[pl.ds(d, 1), :], sem)


def _scatter_body(dest_ref, h_ref, xs_in_ref, xs_ref, sem):
    del xs_in_ref
    base = pl.program_id(0) * (MOE_TT * TOP_K)

    def start(r, carry):
        for k in range(TOP_K):
            _row_copy(h_ref, r, xs_ref, dest_ref[base + r * TOP_K + k], sem).start()
        return carry

    def wait(r, carry):
        for k in range(TOP_K):
            _row_copy(h_ref, 0, xs_ref, 0, sem).wait()
        return carry

    lax.fori_loop(0, MOE_TT, start, 0)
    lax.fori_loop(0, MOE_TT, wait, 0)


def moe_scatter(h, dest, n_rows):
    m, d = h.shape
    return pl.pallas_call(
        _scatter_body,
        out_shape=jax.ShapeDtypeStruct((n_rows, d), h.dtype),
        grid_spec=pltpu.PrefetchScalarGridSpec(
            num_scalar_prefetch=1,
            grid=(m // MOE_TT,),
            in_specs=[pl.BlockSpec((MOE_TT, d), lambda i, dest: (i, 0)),
                      pl.BlockSpec(memory_space=pl.ANY)],
            out_specs=pl.BlockSpec(memory_space=pl.ANY),
            scratch_shapes=[pltpu.SemaphoreType.DMA(())]),
        input_output_aliases={2: 0},
        compiler_params=_cparams(("arbitrary",)),
        name="moe_scatter",
    )(dest, h, jnp.zeros((n_rows, d), h.dtype))


def _moe_mm_body(blk_e_ref, nblk_ref, x_ref, w1_ref, w3_ref, w2_ref, y_ref):
    del blk_e_ref
    used = pl.program_id(0) < nblk_ref[0]

    @pl.when(used)
    def _():
        xb = x_ref[...].astype(BF16)
        hb = (jax.nn.silu(_dot(xb, w1_ref[0])) * _dot(xb, w3_ref[0])).astype(BF16)
        y_ref[...] = _dot(hb, w2_ref[0])

    @pl.when(jnp.logical_not(used))
    def _():
        y_ref[...] = jnp.zeros_like(y_ref)


def moe_experts(xs, blk_e, nblk, w1, w3, w2):
    n_rows, d = xs.shape
    ff = w1.shape[2]
    wspec = lambda r, c: pl.BlockSpec((1, r, c), lambda j, be, nb: (be[j], 0, 0))
    return pl.pallas_call(
        _moe_mm_body,
        out_shape=jax.ShapeDtypeStruct((n_rows, d), F32),
        grid_spec=pltpu.PrefetchScalarGridSpec(
            num_scalar_prefetch=2,
            grid=(n_rows // MOE_ROWS,),
            in_specs=[pl.BlockSpec((MOE_ROWS, d), lambda j, be, nb: (j, 0)),
                      wspec(d, ff), wspec(d, ff), wspec(ff, d)],
            out_specs=pl.BlockSpec((MOE_ROWS, d), lambda j, be, nb: (j, 0))),
        compiler_params=_cparams(("arbitrary",)),
        name="moe_experts",
    )(blk_e, nblk, xs, w1, w3, w2)


def _combine_body(dest_ref, x_ref, r_ref, g_ref, y_ref, xo_ref, h_ref, buf, sem):
    i = pl.program_id(0)
    n = pl.num_programs(0)

    def start_step(step, slot):
        base = step * (MOE_TT * TOP_K)

        def body(r, carry):
            for k in range(TOP_K):
                _row_copy(y_ref, dest_ref[base + r * TOP_K + k], buf.at[slot], k * MOE_TT + r,
                          sem.at[slot]).start()
            return carry

        lax.fori_loop(0, MOE_TT, body, 0)

    @pl.when(i == 0)
    def _():
        start_step(0, 0)

    @pl.when(i + 1 < n)
    def _():
        start_step(i + 1, (i + 1) % 2)

    slot = i % 2

    def wait(r, carry):
        for k in range(TOP_K):
            _row_copy(y_ref, 0, buf.at[slot], 0, sem.at[slot]).wait()
        return carry

    lax.fori_loop(0, MOE_TT, wait, 0)
    r = r_ref[...]
    moe = r[:, 2:3] * buf[slot, 0:MOE_TT, :] + r[:, 3:4] * buf[slot, MOE_TT:2 * MOE_TT, :]
    x = x_ref[...] + moe
    xo_ref[...] = x
    hn = x * lax.rsqrt(jnp.mean(x * x, axis=-1, keepdims=True) + NORM_EPS) * g_ref[...]
    h_ref[...] = hn.astype(h_ref.dtype)


def moe_combine(x, route, y, dest, norm):
    m, d = x.shape
    tok = lambda c: pl.BlockSpec((MOE_TT, c), lambda i, dest: (i, 0))
    return pl.pallas_call(
        _combine_body,
        out_shape=(jax.ShapeDtypeStruct((m, d), F32), jax.ShapeDtypeStruct((m, d), BF16)),
        grid_spec=pltpu.PrefetchScalarGridSpec(
            num_scalar_prefetch=1,
            grid=(m // MOE_TT,),
            in_specs=[tok(d), tok(LANES), pl.BlockSpec((1, d), lambda i, dest: (0, 0)),
                      pl.BlockSpec(memory_space=pl.ANY)],
            out_specs=(tok(d), tok(d)),
            scratch_shapes=[pltpu.VMEM((2, TOP_K * MOE_TT, d), F32),
                            pltpu.SemaphoreType.DMA((2,))]),
        compiler_params=_cparams(("arbitrary",)),
        name="moe_combine",
    )(dest, x, route, norm.reshape(1, d), y)


def moe_ffn(x, norm2, rg_w, rg_b, re_w, re_b, w1, w3, w2, norm3):
    m, d = x.shape
    h, route = moe_router(x, norm2, rg_w, rg_b, re_w, re_b)
    flat_e = route[:, :TOP_K].astype(jnp.int32).reshape(m * TOP_K)
    rank, counts = moe_rank(flat_e)
    padded = (counts + MOE_ROWS - 1) // MOE_ROWS * MOE_ROWS
    pend = jnp.cumsum(padded)
    pstart = pend - padded
    dest = (pstart[flat_e] + rank).astype(jnp.int32)
    n_blocks = (m * TOP_K + N_EXPERTS * (MOE_ROWS - 1) + MOE_ROWS - 1) // MOE_ROWS
    nblk = (pend[-1:] // MOE_ROWS).astype(jnp.int32)
    starts = jnp.arange(n_blocks) * MOE_ROWS
    blk_e = jnp.minimum(jnp.sum(pend[None, :] <= starts[:, None], axis=1), N_EXPERTS - 1).astype(jnp.int32)
    xs = moe_scatter(h, dest, n_blocks * MOE_ROWS)
    y = moe_experts(xs, blk_e, nblk, w1.astype(BF16), w3.astype(BF16), w2.astype(BF16))
    return moe_combine(x, route, y, dest, norm3)


def kernel(x, p, norm1, w_in, q_norm, k_norm, cmp_pos, cmp_w1, cmp_w2, gm_norm, gm_ws, gm_bs,
           s5_lambda_re, s5_lambda_im, s5_log_dt, s5_b_re, s5_b_im, s5_c_re, s5_c_im, s5_d,
           s5_w_glu, s5_b_glu, w_branch, w_out, norm2, router_group_w, router_group_b,
           router_expert_w, router_expert_b, expert_w1, expert_w3, expert_w2, norm3, w_ple,
           w_ple_gate):
    b, l, d = x.shape
    m = b * l
    xf = x.reshape(m, d)
    q_cols = HEADS * HEAD_DIM
    kv_cols = N_BRANCH * 2 * KV_HEADS * HEAD_DIM
    ng_cols = HEADS * N_BRANCH
    splits = [0, q_cols, q_cols + kv_cols, q_cols + kv_cols + ng_cols]
    splits.append(splits[-1] + 2 * GM_WIDTH)
    splits.append(splits[-1] + S5_WIDTH)
    splits.append(splits[-1] + 3 * d)
    per_head = GROUP * N_BRANCH
    for i in range(p.shape[0]):
        h = rms_cast(xf, norm1[i])
        w = w_in[i]
        seg = lambda k: w[:, splits[k]:splits[k + 1]].astype(BF16)
        q_raw = mm_plain(h, seg(0), name="proj_q")
        kv_raw = mm_plain(h, seg(1), name="proj_kv")
        w_ng = jnp.pad(w[:, splits[2]:splits[3]].reshape(d, KV_HEADS, per_head),
                       ((0, 0), (0, 0), (0, LANES - per_head))).reshape(d, KV_HEADS * LANES)
        ng_raw = mm_plain(h, w_ng.astype(BF16), name="proj_gate")
        gm_raw = mm_plain(h, seg(3), name="proj_gm")
        s5_raw = mm_plain(h, seg(4), name="proj_s5")
        mg_raw = mm_plain(h, seg(5), name="proj_merge")

        o_gm = gmlp_mixer(gm_raw, gm_norm[i], gm_ws[i], gm_bs[i])
        o_s5 = s5_mixer(s5_raw, b, s5_lambda_re[i], s5_lambda_im[i], s5_log_dt[i], s5_b_re[i],
                        s5_b_im[i], s5_c_re[i], s5_c_im[i], s5_d[i], s5_w_glu[i], s5_b_glu[i])
        o_att = nsa_mixer(q_raw, kv_raw, ng_raw, b, q_norm[i], k_norm[i], cmp_pos[i], cmp_w1[i],
                          cmp_w2[i])
        wb = w_branch[i].astype(BF16)
        merged = mm_merge([o_gm, o_s5, o_att], [wb[0], wb[1], wb[2]], mg_raw)
        xf = mm_residual(merged, w_out[i].astype(BF16), xf)
        xf, h3 = moe_ffn(xf, norm2[i], router_group_w[i], router_group_b[i], router_expert_w[i],
                         router_expert_b[i], expert_w1[i], expert_w3[i], expert_w2[i], norm3[i])
        xf = mm_ple(h3, w_ple_gate[i].astype(BF16), p[i].reshape(m, -1).astype(BF16),
                    w_ple[i].astype(BF16), xf)
    return xf.reshape(b, l, d)
```

```python
import functools

import jax
import jax.numpy as jnp
from jax import lax
from jax.experimental import pallas as pl
from jax.experimental.pallas import tpu as pltpu

F32 = jnp.float32
BF16 = jnp.bfloat16

NORM_EPS = 1e-6
LANES = 128
VMEM_LIMIT = 52 * 1024 * 1024

HEADS = 16
KV_HEADS = 4
GROUP = HEADS // KV_HEADS
HEAD_DIM = 128
ROPE_THETA = 500000.0
ROPE_DIMS = HEAD_DIM // 4
ROPE_HALF = ROPE_DIMS // 2
CMP_LEN = 32
CMP_STRIDE = 16
SEL_LEN = 64
SEL_TOPN = 16
WINDOW = 512
N_BRANCH = 3
NEG = -1e30
ATT_SCALE = HEAD_DIM ** -0.5 * 1.4426950408889634
ATT_TQ = 128
ATT_TK = 1024
ONES_ROWS = 16

GM_WIDTH = 2048
GM_GROUPS = 16
GM_CHUNK = 128

S5_WIDTH = 2048
S5_GROUP_DIM = 16
S5_GROUPS = S5_WIDTH // S5_GROUP_DIM
S5_STATE = 64
S5_TILE_GROUPS = LANES // S5_GROUP_DIM
S5_TILE_STATES = S5_TILE_GROUPS * S5_STATE
S5_BLOCK = 8
S5_ROWS = 512

N_GROUPS = 4
PER_GROUP = 8
N_EXPERTS = N_GROUPS * PER_GROUP
TOP_K = 2
MOE_ROWS = 256
RANK_T = 512
MOE_TT = 128


def _cparams(sem):
    return pltpu.CompilerParams(dimension_semantics=sem, vmem_limit_bytes=VMEM_LIMIT)


def _dot(a, b):
    return jnp.dot(a, b, preferred_element_type=F32)


def _dot_nt(a, b):
    return lax.dot_general(a, b, (((1,), (1,)), ((), ())), preferred_element_type=F32)


def _rms_body(x_ref, g_ref, o_ref):
    x = x_ref[...]
    y = x * lax.rsqrt(jnp.mean(x * x, axis=-1, keepdims=True) + NORM_EPS)
    o_ref[...] = (y * g_ref[...]).astype(o_ref.dtype)


def rms_cast(x, g, tm=256):
    m, d = x.shape
    return pl.pallas_call(
        _rms_body,
        out_shape=jax.ShapeDtypeStruct((m, d), BF16),
        grid=(m // tm,),
        in_specs=[pl.BlockSpec((tm, d), lambda i: (i, 0)),
                  pl.BlockSpec((1, d), lambda i: (0, 0))],
        out_specs=pl.BlockSpec((tm, d), lambda i: (i, 0)),
        compiler_params=_cparams(("parallel",)),
        name="rms_cast",
    )(x, g.reshape(1, d))


def _mm_plain_body(a_ref, w_ref, o_ref):
    o_ref[...] = _dot(a_ref[...], w_ref[...]).astype(o_ref.dtype)


def _mm_glu_body(a_ref, w_ref, y_ref, b_ref, o_ref):
    acc = _dot(a_ref[...].astype(BF16), w_ref[...]) + b_ref[...]
    o_ref[...] = (y_ref[...] * jax.nn.sigmoid(acc)).astype(o_ref.dtype)


def _mm_res_body(a_ref, w_ref, x_ref, o_ref):
    o_ref[...] = x_ref[...] + _dot(a_ref[...], w_ref[...])


def _mm_merge_body(a0, a1, a2, w0, w1, w2, g0, g1, g2, o_ref):
    acc = jax.nn.sigmoid(g0[...]) * _dot(a0[...], w0[...])
    acc = acc + jax.nn.sigmoid(g1[...]) * _dot(a1[...], w1[...])
    acc = acc + jax.nn.sigmoid(g2[...]) * _dot(a2[...], w2[...])
    o_ref[...] = acc.astype(o_ref.dtype)


def _mm_ple_body(a_ref, w_ref, p_ref, wp_ref, x_ref, o_ref):
    gate = jax.nn.sigmoid(_dot(a_ref[...], w_ref[...]))
    o_ref[...] = x_ref[...] + _dot(p_ref[...], wp_ref[...]) * gate


def _lhs_spec(tm, k):
    return pl.BlockSpec((tm, k), lambda j, i: (i, 0))


def _rhs_spec(k, tn):
    return pl.BlockSpec((k, tn), lambda j, i: (0, j))


def _tile_spec(tm, tn, col_blocks=0):
    return pl.BlockSpec((tm, tn), lambda j, i: (i, j + col_blocks))


def _mm_call(body, args, in_specs, m, n, tm, tn, out_dtype, name):
    return pl.pallas_call(
        body,
        out_shape=jax.ShapeDtypeStruct((m, n), out_dtype),
        grid=(n // tn, m // tm),
        in_specs=in_specs,
        out_specs=pl.BlockSpec((tm, tn), lambda j, i: (i, j)),
        compiler_params=_cparams(("parallel", "parallel")),
        name=name,
    )(*args)


def _pick(n, prefs):
    for t in prefs:
        if n % t == 0:
            return t
    return n


def mm_plain(a, w, col_off=0, n=None, out_dtype=F32, name="mm"):
    m, k = a.shape
    n = w.shape[1] if n is None else n
    tm, tn = _pick(m, (512, 256, 128)), _pick(n, (1024, 512, 256, 128))
    assert col_off % tn == 0
    rhs = pl.BlockSpec((k, tn), lambda j, i: (0, j + col_off // tn))
    return _mm_call(_mm_plain_body, (a, w), [_lhs_spec(tm, k), rhs], m, n, tm, tn, out_dtype, name)


def _cast_body(x_ref, o_ref):
    o_ref[...] = x_ref[...].astype(o_ref.dtype)


def cast_bf16(w, block_bytes=8 * 1024 * 1024):
    cols = w.shape[-1]
    x = w.reshape(-1, cols)
    rows = x.shape[0]
    tr = next(t for t in (4096, 2048, 1024, 512, 256, 128, 64, 32, 16)
              if rows % t == 0 and t * cols * 4 <= block_bytes)
    out = pl.pallas_call(
        _cast_body,
        out_shape=jax.ShapeDtypeStruct((rows, cols), BF16),
        grid=(rows // tr,),
        in_specs=[pl.BlockSpec((tr, cols), lambda i: (i, 0))],
        out_specs=pl.BlockSpec((tr, cols), lambda i: (i, 0)),
        compiler_params=_cparams(("parallel",)),
        name="cast_bf16",
    )(x)
    return out.reshape(w.shape)


def mm_glu(y, w, b):
    m, k = y.shape
    n = w.shape[1]
    tm, tn = _pick(m, (512, 256, 128)), _pick(n, (1024, 512, 256, 128))
    specs = [_lhs_spec(tm, k), _rhs_spec(k, tn), _tile_spec(tm, tn),
             pl.BlockSpec((1, tn), lambda j, i: (0, j))]
    return _mm_call(_mm_glu_body, (y, w, y, b.reshape(1, n)), specs, m, n, tm, tn, BF16, "mm_glu")


def mm_residual(a, w, x):
    m, k = a.shape
    n = w.shape[1]
    tm, tn = _pick(m, (512, 256, 128)), _pick(n, (1024, 512, 256, 128))
    specs = [_lhs_spec(tm, k), _rhs_spec(k, tn), _tile_spec(tm, tn)]
    return _mm_call(_mm_res_body, (a, w, x), specs, m, n, tm, tn, F32, "mm_residual")


def mm_merge(o_list, w_list, mg_raw):
    m, k = o_list[0].shape
    n = w_list[0].shape[1]
    tm, tn = _pick(m, (512, 256, 128)), _pick(n, (512, 256, 128))
    specs = ([_lhs_spec(tm, k)] * 3 + [_rhs_spec(k, tn)] * 3
             + [_tile_spec(tm, tn, c * (n // tn)) for c in range(3)])
    return _mm_call(_mm_merge_body, (*o_list, *w_list, mg_raw, mg_raw, mg_raw), specs,
                    m, n, tm, tn, BF16, "mm_merge")


def mm_ple(h, wg, p, wp, x):
    m, k = h.shape
    n = wg.shape[1]
    kp = p.shape[1]
    tm, tn = _pick(m, (512, 256, 128)), _pick(n, (1024, 512, 256, 128))
    specs = [_lhs_spec(tm, k), _rhs_spec(k, tn), _lhs_spec(tm, kp), _rhs_spec(kp, tn),
             _tile_spec(tm, tn)]
    return _mm_call(_mm_ple_body, (h, wg, p, wp, x), specs, m, n, tm, tn, F32, "mm_ple")


def _gmlp_body(z_ref, gn_ref, w_ref, b_ref, o_ref, *, rows):
    z = jax.nn.gelu(z_ref[...])
    u = z[:, :GM_WIDTH]
    v = z[:, GM_WIDTH:]
    v = v * lax.rsqrt(jnp.mean(v * v, axis=-1, keepdims=True) + NORM_EPS) * gn_ref[...]
    vb = v.astype(BF16)
    for c in range(rows // GM_CHUNK):
        r0 = c * GM_CHUNK
        for g in range(GM_GROUPS):
            c0 = g * LANES
            s = _dot(w_ref[g], vb[r0:r0 + GM_CHUNK, c0:c0 + LANES]) + b_ref[:, c0:c0 + LANES]
            o_ref[r0:r0 + GM_CHUNK, c0:c0 + LANES] = (
                u[r0:r0 + GM_CHUNK, c0:c0 + LANES] * s).astype(o_ref.dtype)


def gmlp_mixer(z, gm_norm, ws, bs, rows=256):
    m = z.shape[0]
    tri = jnp.tril(jnp.ones((GM_CHUNK, GM_CHUNK), dtype=bool))
    w = jnp.where(tri[None], ws, 0.0).astype(BF16)
    bias = jnp.repeat(bs.T, LANES, axis=1)
    return pl.pallas_call(
        functools.partial(_gmlp_body, rows=rows),
        out_shape=jax.ShapeDtypeStruct((m, GM_WIDTH), BF16),
        grid=(m // rows,),
        in_specs=[pl.BlockSpec((rows, 2 * GM_WIDTH), lambda i: (i, 0)),
                  pl.BlockSpec((1, GM_WIDTH), lambda i: (0, 0)),
                  pl.BlockSpec((GM_GROUPS, GM_CHUNK, GM_CHUNK), lambda i: (0, 0, 0)),
                  pl.BlockSpec((GM_CHUNK, GM_WIDTH), lambda i: (0, 0))],
        out_specs=pl.BlockSpec((rows, GM_WIDTH), lambda i: (i, 0)),
        compiler_params=_cparams(("parallel",)),
        name="gmlp",
    )(z, gm_norm.reshape(1, GM_WIDTH), w, bias)


def _shift_rows(x, s):
    t = x.shape[0]
    if s % 8 == 0:
        return jnp.concatenate([jnp.zeros((s, x.shape[1]), x.dtype), x[:t - s]], axis=0)
    rolled = pltpu.roll(x, s, 0)
    row = lax.broadcasted_iota(jnp.int32, x.shape, 0)
    return jnp.where(row >= s, rolled, 0.0)


def _group_diag(table, rows_per_group, lanes_per_group):
    full = jnp.concatenate([table] * S5_TILE_GROUPS, axis=0)
    rg = lax.shift_right_logical(lax.broadcasted_iota(jnp.int32, full.shape, 0),
                                 rows_per_group.bit_length() - 1)
    lg = lax.shift_right_logical(lax.broadcasted_iota(jnp.int32, full.shape, 1),
                                 lanes_per_group.bit_length() - 1)
    return jnp.where(rg == lg, full, 0.0).astype(BF16)


def _s5_body(u_ref, kt_ref, qt_ref, rt_ref, are_ref, aim_ref, d_ref, o_ref,
             w_ref, q_ref, r_ref, hre_ref, him_ref, *, rows, levels):
    tb, hg, ns = S5_BLOCK, S5_GROUP_DIM, S5_STATE

    @pl.when(pl.program_id(2) == 0)
    def _():
        hre_ref[...] = jnp.zeros_like(hre_ref)
        him_ref[...] = jnp.zeros_like(him_ref)
        lag_blocks = [_group_diag(kt_ref[0, d], hg, hg) for d in range(tb)]
        zero = jnp.zeros((LANES, LANES), BF16)
        for ip in range(tb):
            for i in range(tb):
                w_ref[ip * LANES:(ip + 1) * LANES, i * LANES:(i + 1) * LANES] = (
                    lag_blocks[i - ip] if i >= ip else zero)
        for ip in range(tb):
            for ri in range(2):
                q_ref[ip * LANES:(ip + 1) * LANES, ri * S5_TILE_STATES:(ri + 1) * S5_TILE_STATES] = (
                    _group_diag(qt_ref[0, ip, ri], hg, ns))
        for ri in range(2):
            for i in range(tb):
                r_ref[ri * S5_TILE_STATES:(ri + 1) * S5_TILE_STATES, i * LANES:(i + 1) * LANES] = (
                    _group_diag(rt_ref[0, ri, i], ns, hg))

    us = [u_ref[pl.ds(i, rows, stride=S5_BLOCK), :] for i in range(S5_BLOCK)]
    ucat = jnp.concatenate([x.astype(BF16) for x in us], axis=1)
    s = _dot(ucat, q_ref[...])
    xr = s[:, :S5_TILE_STATES]
    xi = s[:, S5_TILE_STATES:]
    are = are_ref[0]
    aim = aim_ref[0]
    hr = hre_ref[0:1, :]
    hi = him_ref[0:1, :]
    a_r = are[0:1, :]
    a_i = aim[0:1, :]
    first = lax.broadcasted_iota(jnp.int32, xr.shape, 0) == 0
    xr = xr + jnp.where(first, a_r * hr - a_i * hi, 0.0)
    xi = xi + jnp.where(first, a_r * hi + a_i * hr, 0.0)
    for k in range(levels):
        sh = 1 << k
        cr = are[k:k + 1, :]
        ci = aim[k:k + 1, :]
        sr = _shift_rows(xr, sh)
        si = _shift_rows(xi, sh)
        xr, xi = xr + cr * sr - ci * si, xi + cr * si + ci * sr
    hre_ref[0:1, :] = xr[rows - 1:rows, :]
    him_ref[0:1, :] = xi[rows - 1:rows, :]
    xpr = _shift_rows(xr, 1) + jnp.where(first, hr, 0.0)
    xpi = _shift_rows(xi, 1) + jnp.where(first, hi, 0.0)
    xprev = jnp.concatenate([xpr, xpi], axis=1).astype(BF16)
    y = _dot(ucat, w_ref[...]) + _dot(xprev, r_ref[...])
    d = d_ref[...]
    for i in range(S5_BLOCK):
        yi = y[:, i * LANES:(i + 1) * LANES] + d * us[i]
        o_ref[pl.ds(i, rows, stride=S5_BLOCK), :] = jax.nn.gelu(yi)


def _s5_discretize(lam_re, lam_im, log_dt, b_re, b_im):
    dt = jnp.exp(log_dt.astype(F32))[:, None]
    lre = jnp.minimum(lam_re.astype(F32), -1e-4)
    lim = lam_im.astype(F32)
    mag = jnp.exp(lre * dt)
    a_re = mag * jnp.cos(lim * dt)
    a_im = mag * jnp.sin(lim * dt)
    den = lre * lre + lim * lim
    nr = a_re - 1.0
    f_re = (nr * lre + a_im * lim) / den
    f_im = (a_im * lre - nr * lim) / den
    br = b_re.astype(F32)
    bi = b_im.astype(F32)
    bb_re = f_re[..., None] * br - f_im[..., None] * bi
    bb_im = f_re[..., None] * bi + f_im[..., None] * br
    return a_re, a_im, bb_re, bb_im


def _s5_block_weights(a_re, a_im, bb_re, bb_im, c_re, c_im, levels):
    hp = lax.Precision.HIGHEST
    ng, ns = a_re.shape
    tb, gl, hg = S5_BLOCK, S5_TILE_GROUPS, S5_GROUP_DIM
    nt = ng // gl
    pr, pi = [jnp.ones_like(a_re)], [jnp.zeros_like(a_re)]
    for _ in range(tb):
        r, i = pr[-1], pi[-1]
        pr.append(r * a_re - i * a_im)
        pi.append(r * a_im + i * a_re)
    pw_re, pw_im = jnp.stack(pr), jnp.stack(pi)
    ab_re = pw_re[:tb, :, :, None] * bb_re[None] - pw_im[:tb, :, :, None] * bb_im[None]
    ab_im = pw_re[:tb, :, :, None] * bb_im[None] + pw_im[:tb, :, :, None] * bb_re[None]
    cr, ci = c_re.astype(F32), c_im.astype(F32)
    kern = (jnp.einsum('gcp,dgpk->dgck', cr, ab_re, precision=hp)
            - jnp.einsum('gcp,dgpk->dgck', ci, ab_im, precision=hp))
    kt = kern.reshape(tb, nt, gl, hg, hg).transpose(1, 0, 4, 2, 3).reshape(nt, tb, hg, gl * hg)
    q = jnp.stack([ab_re[::-1], ab_im[::-1]], axis=1).reshape(tb, 2, nt, gl, ns, hg)
    qt = q.transpose(2, 0, 1, 5, 3, 4).reshape(nt, tb, 2, hg, gl * ns)
    nr, ni = pw_re[1:, :, None, :], pw_im[1:, :, None, :]
    r = jnp.stack([cr[None] * nr - ci[None] * ni, -(cr[None] * ni + ci[None] * nr)])
    r = r.reshape(2, tb, nt, gl, hg, ns)
    rt = r.transpose(2, 0, 1, 5, 3, 4).reshape(nt, 2, tb, ns, gl * hg)
    sr, si = [pw_re[tb].reshape(-1)], [pw_im[tb].reshape(-1)]
    for _ in range(15):
        x, y = sr[-1], si[-1]
        sr.append(x * x - y * y)
        si.append(2.0 * x * y)
    assert levels <= 16
    scan_re = jnp.stack(sr).reshape(16, nt, gl * ns).transpose(1, 0, 2)
    scan_im = jnp.stack(si).reshape(16, nt, gl * ns).transpose(1, 0, 2)
    return kt, qt, rt, scan_re, scan_im


def s5_mixer(u, batch, lam_re, lam_im, log_dt, b_re, b_im, c_re, c_im, d_skip, w_glu, b_glu):
    m = u.shape[0]
    seq = m // batch
    nt = S5_GROUPS // S5_TILE_GROUPS
    nblk = seq // S5_BLOCK
    rows = min(S5_ROWS, nblk)
    levels = rows.bit_length() - 1
    assert rows == 1 << levels and nblk % rows == 0
    a_re, a_im, bb_re, bb_im = _s5_discretize(lam_re, lam_im, log_dt, b_re, b_im)
    kt, qt, rt, scan_re, scan_im = _s5_block_weights(a_re, a_im, bb_re, bb_im, c_re, c_im, levels)

    nchunk = nblk // rows
    tok = rows * S5_BLOCK
    tspec = lambda shape: pl.BlockSpec((1,) + shape, lambda b, j, t: (j,) + (0,) * len(shape))
    kin = S5_BLOCK * LANES
    y = pl.pallas_call(
        functools.partial(_s5_body, rows=rows, levels=levels),
        out_shape=jax.ShapeDtypeStruct((m, S5_WIDTH), F32),
        grid=(batch, nt, nchunk),
        in_specs=[pl.BlockSpec((tok, LANES), lambda b, j, t: (b * nchunk + t, j)),
                  tspec(kt.shape[1:]), tspec(qt.shape[1:]), tspec(rt.shape[1:]),
                  tspec((16, S5_TILE_STATES)), tspec((16, S5_TILE_STATES)),
                  pl.BlockSpec((1, LANES), lambda b, j, t: (0, j))],
        out_specs=pl.BlockSpec((tok, LANES), lambda b, j, t: (b * nchunk + t, j)),
        scratch_shapes=[pltpu.VMEM((kin, kin), BF16), pltpu.VMEM((kin, 2 * S5_TILE_STATES), BF16),
                        pltpu.VMEM((2 * S5_TILE_STATES, kin), BF16),
                        pltpu.VMEM((8, S5_TILE_STATES), F32), pltpu.VMEM((8, S5_TILE_STATES), F32)],
        compiler_params=_cparams(("parallel", "parallel", "arbitrary")),
        name="s5_scan",
    )(u, kt, qt, rt, scan_re, scan_im, d_skip.reshape(1, S5_WIDTH))
    return mm_glu(y, w_glu.astype(BF16), b_glu)


def _rope_tables(pos):
    inv_freq = ROPE_THETA ** (-jnp.arange(ROPE_HALF, dtype=F32) / ROPE_HALF)
    ang = pos.astype(F32)[:, None] * inv_freq[None, :]
    cos, sin = jnp.cos(ang), jnp.sin(ang)
    n = pos.shape[0]
    rest = HEAD_DIM - ROPE_DIMS
    c = jnp.concatenate([cos, cos, jnp.ones((n, rest), F32)], axis=1)
    s1 = jnp.concatenate([-sin, jnp.zeros((n, HEAD_DIM - ROPE_HALF), F32)], axis=1)
    s2 = jnp.concatenate([jnp.zeros((n, ROPE_HALF), F32), sin, jnp.zeros((n, rest), F32)], axis=1)
    return c, s1, s2


def _norm_rope(x, gain, c, s1, s2):
    x = x * lax.rsqrt(jnp.mean(x * x, axis=-1, keepdims=True) + NORM_EPS) * gain
    return (x * c + pltpu.roll(x, HEAD_DIM - ROPE_HALF, 1) * s1 + pltpu.roll(x, ROPE_HALF, 1) * s2)


def _kv_prep_body(ks_ref, vs_ref, kw_ref, vw_ref, kn_ref, c_ref, s1_ref, s2_ref,
                  oks_ref, ovs_ref, okw_ref, ovw_ref):
    c, s1, s2 = c_ref[...], s1_ref[...], s2_ref[...]
    oks_ref[...] = _norm_rope(ks_ref[...], kn_ref[1:2, :], c, s1, s2).astype(BF16)
    okw_ref[...] = _norm_rope(kw_ref[...], kn_ref[2:3, :], c, s1, s2).astype(BF16)
    vt = vs_ref[...].T
    ones = jnp.ones((ONES_ROWS, vt.shape[1]), F32)
    ovs_ref[...] = jnp.concatenate([vt, ones], axis=0).astype(BF16)
    ovw_ref[...] = jnp.concatenate([vw_ref[...].T, ones], axis=0).astype(BF16)


def _cmp_prep_body(k_ref, v_ref, pos_ref, w1_ref, w2_ref, kn_ref, c_ref, s1_ref, s2_ref,
                   ok_ref, ov_ref, *, ncp):
    half = CMP_LEN // 2
    outs = []
    for which, t_ref in enumerate((k_ref, v_ref)):
        lo = jnp.zeros((ncp, HEAD_DIM), F32)
        hi = jnp.zeros((ncp, HEAD_DIM), F32)
        for j in range(half):
            tj = t_ref[pl.ds(j, ncp, stride=CMP_STRIDE), :]
            a = (tj + pos_ref[which, j:j + 1, :]).astype(BF16)
            b = (tj + pos_ref[which, half + j:half + j + 1, :]).astype(BF16)
            lo = lo + _dot(a, w1_ref[which, j])
            hi = hi + _dot(b, w1_ref[which, half + j])
        pre = lo + pltpu.roll(hi, ncp - 1, 0)
        outs.append(_dot(jax.nn.gelu(pre).astype(BF16), w2_ref[which]))
    valid = lax.broadcasted_iota(jnp.int32, (ncp, HEAD_DIM), 0) < ncp - 1
    kc = _norm_rope(outs[0], kn_ref[0:1, :], c_ref[...], s1_ref[...], s2_ref[...])
    ok_ref[...] = jnp.where(valid, kc, 0.0).astype(BF16)
    ov_ref[...] = jnp.where(valid, outs[1], 0.0).T.astype(BF16)


def _attn_body(q_ref, gate_ref, c_ref, s1_ref, s2_ref, qn_ref, kc_ref, vct_ref,
               ks_ref, vst_ref, kw_ref, vwt_ref, ovt_ref, o_ref, bias_ref, s_ref, *, tq, ncp):
    s0 = pl.program_id(2) * tq
    cols = GROUP * tq
    qpos = s0 + (lax.broadcasted_iota(jnp.int32, (1, cols), 1) & (tq - 1))
    qpos1 = qpos[:, :tq]
    c, s1, s2 = c_ref[...], s1_ref[...], s2_ref[...]
    qn = qn_ref[...]
    q_t = jnp.concatenate(
        [(_norm_rope(q_ref[:, g * HEAD_DIM:(g + 1) * HEAD_DIM], qn, c, s1, s2) * ATT_SCALE).T.astype(BF16)
         for g in range(GROUP)], axis=1)

    n_id = lax.broadcasted_iota(jnp.int32, (ncp, 1), 0)
    cbias = jnp.where(n_id * CMP_STRIDE + (CMP_LEN - 1) <= qpos1, 0.0, NEG)
    sc = _dot(kc_ref[...], q_t) + jnp.concatenate([cbias] * GROUP, axis=1)
    mx = jnp.max(sc, axis=0, keepdims=True)
    e = jnp.exp2(sc - mx)
    inv = jnp.where(qpos >= CMP_LEN - 1, 1.0 / jnp.sum(e, axis=0, keepdims=True), 0.0)
    p = e * inv
    o_cmp = _dot(vct_ref[...], p.astype(BF16))

    pg = p[:, 0:tq]
    for g in range(1, GROUP):
        pg = pg + p[:, g * tq:(g + 1) * tq]
    p_hi = pg.astype(BF16)
    p_lo = (pg - p_hi.astype(F32)).astype(BF16)
    imp = _dot(ovt_ref[...], p_hi) + _dot(ovt_ref[...], p_lo)
    blk = lax.broadcasted_iota(jnp.int32, (LANES, tq), 0)
    cur = lax.shift_right_logical(qpos1, 6)
    forced = (blk == 0) | (blk == cur) | (blk == cur - 1)
    val = jnp.where(forced, -jnp.inf, jnp.where(blk <= cur, imp, -1e9))
    blk_f = blk.astype(F32)
    sel = jnp.where(forced, 1.0, 0.0)
    for _ in range(SEL_TOPN - 3):
        top = jnp.max(val, axis=0, keepdims=True)
        idx = jnp.min(jnp.where(val == top, blk_f, float(LANES)), axis=0, keepdims=True)
        hit = blk_f == idx
        val = jnp.where(hit, -jnp.inf, val)
        sel = jnp.where(hit, 1.0, sel)
    bias_ref[...] = (sel - 1.0) * (-NEG)

    blocks_per_tile = ATT_TK // SEL_LEN

    def score_tile(kt, m8, diagonal):
        k0 = pl.multiple_of(kt * ATT_TK, ATT_TK)
        s = _dot(ks_ref[pl.ds(k0, ATT_TK), :], q_t)
        b8 = bias_ref[pl.ds(pl.multiple_of(kt * blocks_per_tile, blocks_per_tile), blocks_per_tile), :]
        bias = jnp.concatenate([jnp.broadcast_to(b8[j:j + 1, :], (SEL_LEN, tq))
                                for j in range(blocks_per_tile)], axis=0)
        if diagonal:
            kpos = k0 + lax.broadcasted_iota(jnp.int32, (ATT_TK, 1), 0)
            bias = jnp.where(kpos <= qpos1, bias, NEG)
        s = s + jnp.concatenate([bias] * GROUP, axis=1)
        s_ref[pl.ds(k0, ATT_TK), :] = s
        return jnp.maximum(m8, jnp.max(s.reshape(ATT_TK // 8, 8, cols), axis=0))

    n_full = lax.div(s0, ATT_TK)
    m8 = lax.fori_loop(0, n_full, lambda kt, mm: score_tile(kt, mm, False), jnp.full((8, cols), NEG, F32))
    m_sel = jnp.max(score_tile(n_full, m8, True), axis=0, keepdims=True)

    def value_tile(kt, acc):
        k0 = pl.multiple_of(kt * ATT_TK, ATT_TK)
        pe = jnp.exp2(s_ref[pl.ds(k0, ATT_TK), :] - m_sel).astype(BF16)
        return acc + _dot(vst_ref[:, pl.ds(k0, ATT_TK)], pe)

    acc_sel = lax.fori_loop(0, n_full + 1, value_tile, jnp.zeros((HEAD_DIM + ONES_ROWS, cols), F32))
    o_sel = acc_sel[:HEAD_DIM] / acc_sel[HEAD_DIM:HEAD_DIM + 1]

    span = WINDOW + tq
    w0 = pl.multiple_of(jnp.maximum(s0 - WINDOW, 0), tq)
    kp = w0 + lax.broadcasted_iota(jnp.int32, (span, 1), 0)
    wbias = jnp.where((kp <= qpos1) & (kp > qpos1 - WINDOW), 0.0, NEG)
    sw = _dot(kw_ref[pl.ds(w0, span), :], q_t) + jnp.concatenate([wbias] * GROUP, axis=1)
    ew = jnp.exp2(sw - jnp.max(sw, axis=0, keepdims=True))
    acc_win = _dot(vwt_ref[:, pl.ds(w0, span)], ew.astype(BF16))
    o_win = acc_win[:HEAD_DIM] / acc_win[HEAD_DIM:HEAD_DIM + 1]

    gates = jax.nn.sigmoid(gate_ref[...]).T
    for g in range(GROUP):
        cs = slice(g * tq, (g + 1) * tq)
        r = g * N_BRANCH
        out = (gates[r:r + 1, :] * o_cmp[:, cs] + gates[r + 1:r + 2, :] * o_sel[:, cs]
               + gates[r + 2:r + 3, :] * o_win[:, cs])
        o_ref[:, g * HEAD_DIM:(g + 1) * HEAD_DIM] = out.T.astype(o_ref.dtype)


def nsa_mixer(q_raw, kv_raw, gate_raw, batch, q_norm, k_norm, cmp_pos, cmp_w1, cmp_w2):
    m = q_raw.shape[0]
    seq = m // batch
    ncp = seq // CMP_STRIDE
    nsel = seq // SEL_LEN
    assert seq % ATT_TK == 0 and nsel <= LANES and seq >= WINDOW + ATT_TQ
    pos = jnp.arange(seq)
    c, s1, s2 = _rope_tables(pos)
    cc, cs1, cs2 = _rope_tables(jnp.arange(ncp) * CMP_STRIDE + CMP_LEN - 1)
    col = lambda branch, kv: (branch * 2 + kv) * KV_HEADS

    tl = _pick(seq, (1024, 512))
    nl = seq // tl
    kvspec = lambda base: pl.BlockSpec((tl, HEAD_DIM), lambda b, h, t: (b * nl + t, base + h))
    tab = pl.BlockSpec((tl, HEAD_DIM), lambda b, h, t: (t, 0))
    outspec = pl.BlockSpec((None, None, tl, HEAD_DIM), lambda b, h, t: (b, h, t, 0))
    outspec_t = pl.BlockSpec((None, None, HEAD_DIM, tl), lambda b, h, t: (b, h, 0, t))
    kv_shape = jax.ShapeDtypeStruct((batch, KV_HEADS, seq, HEAD_DIM), BF16)
    kv_shape_t = jax.ShapeDtypeStruct((batch, KV_HEADS, HEAD_DIM, seq), BF16)
    vrows = HEAD_DIM + ONES_ROWS
    outspec_t1 = pl.BlockSpec((None, None, vrows, tl), lambda b, h, t: (b, h, 0, t))
    kv_shape_t1 = jax.ShapeDtypeStruct((batch, KV_HEADS, vrows, seq), BF16)
    k_s, v_s, k_w, v_w = pl.pallas_call(
        _kv_prep_body,
        out_shape=(kv_shape, kv_shape_t1, kv_shape, kv_shape_t1),
        grid=(batch, KV_HEADS, nl),
        in_specs=[kvspec(col(1, 0)), kvspec(col(1, 1)), kvspec(col(2, 0)), kvspec(col(2, 1)),
                  pl.BlockSpec((N_BRANCH, HEAD_DIM), lambda b, h, t: (0, 0)), tab, tab, tab],
        out_specs=(outspec, outspec_t1, outspec, outspec_t1),
        compiler_params=_cparams(("parallel", "parallel", "parallel")),
        name="nsa_kv_prep",
    )(kv_raw, kv_raw, kv_raw, kv_raw, k_norm, c, s1, s2)

    w1 = cmp_w1.reshape(2, CMP_LEN, HEAD_DIM, HEAD_DIM).astype(BF16)
    full = lambda shape: pl.BlockSpec(shape, lambda b, h: (0,) * len(shape))
    cshape = jax.ShapeDtypeStruct((batch, KV_HEADS, ncp, HEAD_DIM), BF16)
    cspec = pl.BlockSpec((None, None, ncp, HEAD_DIM), lambda b, h: (b, h, 0, 0))
    cshape_t = jax.ShapeDtypeStruct((batch, KV_HEADS, HEAD_DIM, ncp), BF16)
    cspec_t = pl.BlockSpec((None, None, HEAD_DIM, ncp), lambda b, h: (b, h, 0, 0))
    k_c, v_c = pl.pallas_call(
        functools.partial(_cmp_prep_body, ncp=ncp),
        out_shape=(cshape, cshape_t),
        grid=(batch, KV_HEADS),
        in_specs=[pl.BlockSpec((seq, HEAD_DIM), lambda b, h: (b, col(0, 0) + h)),
                  pl.BlockSpec((seq, HEAD_DIM), lambda b, h: (b, col(0, 1) + h)),
                  full((2, CMP_LEN, HEAD_DIM)), full((2, CMP_LEN, HEAD_DIM, HEAD_DIM)),
                  full((2, HEAD_DIM, HEAD_DIM)), full((N_BRANCH, HEAD_DIM)),
                  full((ncp, HEAD_DIM)), full((ncp, HEAD_DIM)), full((ncp, HEAD_DIM))],
        out_specs=(cspec, cspec_t),
        compiler_params=_cparams(("parallel", "parallel")),
        name="nsa_cmp_prep",
    )(kv_raw, kv_raw, cmp_pos, w1, cmp_w2.astype(BF16), k_norm, cc, cs1, cs2)

    cstart = jnp.arange(ncp)[:, None] * CMP_STRIDE
    sstart = jnp.arange(LANES)[None, :] * SEL_LEN
    overlap = jnp.maximum(jnp.minimum(cstart + CMP_LEN, sstart + SEL_LEN) - jnp.maximum(cstart, sstart), 0)
    overlap = jnp.where(jnp.arange(LANES)[None, :] < nsel, overlap, 0)
    overlap_t = (overlap.astype(F32) / CMP_STRIDE).astype(BF16).T

    tq = ATT_TQ
    nq = seq // tq
    qtab = pl.BlockSpec((tq, HEAD_DIM), lambda b, h, i: (i, 0))
    kvfull = lambda n: pl.BlockSpec((None, None, n, HEAD_DIM), lambda b, h, i: (b, h, 0, 0))
    kvfull_t = lambda n: pl.BlockSpec((None, None, HEAD_DIM, n), lambda b, h, i: (b, h, 0, 0))
    vfull_t = pl.BlockSpec((None, None, vrows, seq), lambda b, h, i: (b, h, 0, 0))
    return pl.pallas_call(
        functools.partial(_attn_body, tq=tq, ncp=ncp),
        out_shape=jax.ShapeDtypeStruct((m, HEADS * HEAD_DIM), BF16),
        grid=(batch, KV_HEADS, nq),
        in_specs=[pl.BlockSpec((tq, GROUP * HEAD_DIM), lambda b, h, i: (b * nq + i, h)),
                  pl.BlockSpec((tq, LANES), lambda b, h, i: (b * nq + i, h)),
                  qtab, qtab, qtab,
                  pl.BlockSpec((1, HEAD_DIM), lambda b, h, i: (0, 0)),
                  kvfull(ncp), kvfull_t(ncp), kvfull(seq), vfull_t, kvfull(seq), vfull_t,
                  pl.BlockSpec((LANES, ncp), lambda b, h, i: (0, 0))],
        out_specs=pl.BlockSpec((tq, GROUP * HEAD_DIM), lambda b, h, i: (b * nq + i, h)),
        scratch_shapes=[pltpu.VMEM((LANES, tq), F32), pltpu.VMEM((seq, GROUP * tq), F32)],
        compiler_params=_cparams(("parallel", "parallel", "arbitrary")),
        name="nsa_attn",
    )(q_raw, gate_raw, c, s1, s2, q_norm.reshape(1, HEAD_DIM), k_c, v_c, k_s, v_s, k_w, v_w, overlap_t)


def _router_body(x_ref, g_ref, whi_ref, wlo_ref, b_ref, h_ref, r_ref):
    x = x_ref[...]
    h = x * lax.rsqrt(jnp.mean(x * x, axis=-1, keepdims=True) + NORM_EPS) * g_ref[...]
    h_ref[...] = h
    h_hi = h.astype(BF16)
    h_lo = (h - h_hi.astype(F32)).astype(BF16)
    logits = (_dot(h_hi, whi_ref[...]) + _dot(h_lo, whi_ref[...]) + _dot(h_hi, wlo_ref[...])
              + b_ref[...])
    lane = lax.broadcasted_iota(jnp.int32, logits.shape, 1)
    lane_f = lane.astype(F32)
    none = float(LANES)
    is_g = lane < N_GROUPS
    glog = jnp.where(is_g, logits, -jnp.inf)
    gmax = jnp.max(glog, axis=-1, keepdims=True)
    gidx = jnp.min(jnp.where(glog == gmax, lane_f, none), axis=-1, keepdims=True)
    g_p = 1.0 / jnp.sum(jnp.where(is_g, jnp.exp(logits - gmax), 0.0), axis=-1, keepdims=True)
    e_lo = N_GROUPS + gidx * PER_GROUP
    elog = jnp.where((lane_f >= e_lo) & (lane_f < e_lo + PER_GROUP), logits, -jnp.inf)
    m1 = jnp.max(elog, axis=-1, keepdims=True)
    i1 = jnp.min(jnp.where(elog == m1, lane_f, none), axis=-1, keepdims=True)
    elog = jnp.where(lane_f == i1, -jnp.inf, elog)
    m2 = jnp.max(elog, axis=-1, keepdims=True)
    i2 = jnp.min(jnp.where(elog == m2, lane_f, none), axis=-1, keepdims=True)
    t = jnp.exp(m2 - m1)
    w1 = g_p / (1.0 + t)
    w2 = g_p * t / (1.0 + t)
    r_ref[...] = jnp.where(lane == 0, i1 - N_GROUPS,
                           jnp.where(lane == 1, i2 - N_GROUPS,
                                     jnp.where(lane == 2, w1, jnp.where(lane == 3, w2, 0.0))))


def moe_router(x, norm, rg_w, rg_b, re_w, re_b, tm=256):
    m, d = x.shape
    pad = LANES - N_GROUPS - N_EXPERTS
    wr = jnp.pad(jnp.concatenate([rg_w, re_w], axis=1).astype(F32), ((0, 0), (0, pad)))
    whi = wr.astype(BF16)
    wlo = (wr - whi.astype(F32)).astype(BF16)
    bias = jnp.pad(jnp.concatenate([rg_b, re_b]).astype(F32), (0, pad)).reshape(1, LANES)
    const = lambda shape: pl.BlockSpec(shape, lambda i: (0, 0))
    return pl.pallas_call(
        _router_body,
        out_shape=(jax.ShapeDtypeStruct((m, d), F32), jax.ShapeDtypeStruct((m, LANES), F32)),
        grid=(m // tm,),
        in_specs=[pl.BlockSpec((tm, d), lambda i: (i, 0)), const((1, d)),
                  const((d, LANES)), const((d, LANES)), const((1, LANES))],
        out_specs=(pl.BlockSpec((tm, d), lambda i: (i, 0)), pl.BlockSpec((tm, LANES), lambda i: (i, 0))),
        compiler_params=_cparams(("parallel",)),
        name="moe_router",
    )(x, norm.reshape(1, d), whi, wlo, bias)


def _rank_body(e_ref, rank_ref, cnt_ref, carry_ref):
    @pl.when(pl.program_id(0) == 0)
    def _():
        carry_ref[...] = jnp.zeros_like(carry_ref)

    e = e_ref[0]
    sub = lax.broadcasted_iota(jnp.int32, (N_EXPERTS, RANK_T), 0)
    onehot = jnp.where(sub == e, 1.0, 0.0)
    r = lax.broadcasted_iota(jnp.int32, (RANK_T, RANK_T), 0)
    c = lax.broadcasted_iota(jnp.int32, (RANK_T, RANK_T), 1)
    before = jnp.where(r < c, 1.0, 0.0).astype(BF16)
    prefix = _dot(onehot.astype(BF16), before)
    carry = carry_ref[:, 0:1]
    rank = jnp.sum(onehot * (prefix + carry), axis=0, keepdims=True)
    rank_ref[0] = rank.astype(jnp.int32)
    carry_ref[...] = carry_ref[...] + jnp.sum(onehot, axis=1, keepdims=True)
    cnt_ref[...] = carry_ref[...]


def moe_rank(flat_e):
    a = flat_e.shape[0]
    nt = a // RANK_T
    rank, cnt = pl.pallas_call(
        _rank_body,
        out_shape=(jax.ShapeDtypeStruct((nt, 1, RANK_T), jnp.int32),
                   jax.ShapeDtypeStruct((N_EXPERTS, LANES), F32)),
        grid=(nt,),
        in_specs=[pl.BlockSpec((1, 1, RANK_T), lambda i: (i, 0, 0))],
        out_specs=(pl.BlockSpec((1, 1, RANK_T), lambda i: (i, 0, 0)),
                   pl.BlockSpec((N_EXPERTS, LANES), lambda i: (0, 0))),
        scratch_shapes=[pltpu.VMEM((N_EXPERTS, LANES), F32)],
        compiler_params=_cparams(("arbitrary",)),
        name="moe_rank",
    )(flat_e.reshape(nt, 1, RANK_T))
    return rank.reshape(a), cnt[:, 0].astype(jnp.int32)


def _row_copy(src, s, dst, d, sem):
    return pltpu.make_async_copy(src.at[pl.ds(s, 1), :], dst.at[pl.ds(d, 1), :], sem)


def _scatter_body(dest_ref, h_ref, xs_in_ref, xs_ref, sem):
    del xs_in_ref
    base = pl.program_id(0) * (MOE_TT * TOP_K)

    def start(r, carry):
        for k in range(TOP_K):
            _row_copy(h_ref, r, xs_ref, dest_ref[base + r * TOP_K + k], sem).start()
        return carry

    def wait(r, carry):
        for k in range(TOP_K):
            _row_copy(h_ref, 0, xs_ref, 0, sem).wait()
        return carry

    lax.fori_loop(0, MOE_TT, start, 0)
    lax.fori_loop(0, MOE_TT, wait, 0)


def moe_scatter(h, dest, xs_init):
    m, d = h.shape
    n_rows = xs_init.shape[0]
    return pl.pallas_call(
        _scatter_body,
        out_shape=jax.ShapeDtypeStruct((n_rows, d), h.dtype),
        grid_spec=pltpu.PrefetchScalarGridSpec(
            num_scalar_prefetch=1,
            grid=(m // MOE_TT,),
            in_specs=[pl.BlockSpec((MOE_TT, d), lambda i, dest: (i, 0)),
                      pl.BlockSpec(memory_space=pl.ANY)],
            out_specs=pl.BlockSpec(memory_space=pl.ANY),
            scratch_shapes=[pltpu.SemaphoreType.DMA(())]),
        input_output_aliases={2: 0},
        compiler_params=_cparams(("arbitrary",)),
        name="moe_scatter",
    )(dest, h, xs_init)


def _moe_mm_body(blk_e_ref, nblk_ref, x_ref, w1_ref, w3_ref, w2_ref, y_ref):
    del blk_e_ref
    used = pl.program_id(0) < nblk_ref[0]

    @pl.when(used)
    def _():
        xb = x_ref[...].astype(BF16)
        hb = (jax.nn.silu(_dot(xb, w1_ref[0])) * _dot(xb, w3_ref[0])).astype(BF16)
        y_ref[...] = _dot(hb, w2_ref[0])

    @pl.when(jnp.logical_not(used))
    def _():
        y_ref[...] = jnp.zeros_like(y_ref)


def moe_experts(xs, blk_e, nblk, w1, w3, w2):
    n_rows, d = xs.shape
    ff = w1.shape[2]
    wspec = lambda r, c: pl.BlockSpec((1, r, c), lambda j, be, nb: (be[j], 0, 0))
    return pl.pallas_call(
        _moe_mm_body,
        out_shape=jax.ShapeDtypeStruct((n_rows, d), F32),
        grid_spec=pltpu.PrefetchScalarGridSpec(
            num_scalar_prefetch=2,
            grid=(n_rows // MOE_ROWS,),
            in_specs=[pl.BlockSpec((MOE_ROWS, d), lambda j, be, nb: (jnp.minimum(j, nb[0] - 1), 0)),
                      wspec(d, ff), wspec(d, ff), wspec(ff, d)],
            out_specs=pl.BlockSpec((MOE_ROWS, d), lambda j, be, nb: (j, 0))),
        compiler_params=_cparams(("arbitrary",)),
        name="moe_experts",
    )(blk_e, nblk, xs, w1, w3, w2)


def _combine_body(dest_ref, x_ref, r_ref, g_ref, y_ref, xo_ref, h_ref, buf, sem):
    i = pl.program_id(0)
    n = pl.num_programs(0)

    def start_step(step, slot):
        base = step * (MOE_TT * TOP_K)

        def body(r, carry):
            for k in range(TOP_K):
                _row_copy(y_ref, dest_ref[base + r * TOP_K + k], buf.at[slot], k * MOE_TT + r,
                          sem.at[slot]).start()
            return carry

        lax.fori_loop(0, MOE_TT, body, 0)

    @pl.when(i == 0)
    def _():
        start_step(0, 0)

    @pl.when(i + 1 < n)
    def _():
        start_step(i + 1, (i + 1) % 2)

    slot = i % 2

    def wait(r, carry):
        for k in range(TOP_K):
            _row_copy(y_ref, 0, buf.at[slot], 0, sem.at[slot]).wait()
        return carry

    lax.fori_loop(0, MOE_TT, wait, 0)
    r = r_ref[...]
    moe = r[:, 2:3] * buf[slot, 0:MOE_TT, :] + r[:, 3:4] * buf[slot, MOE_TT:2 * MOE_TT, :]
    x = x_ref[...] + moe
    xo_ref[...] = x
    hn = x * lax.rsqrt(jnp.mean(x * x, axis=-1, keepdims=True) + NORM_EPS) * g_ref[...]
    h_ref[...] = hn.astype(h_ref.dtype)


def moe_combine(x, route, y, dest, norm):
    m, d = x.shape
    tok = lambda c: pl.BlockSpec((MOE_TT, c), lambda i, dest: (i, 0))
    return pl.pallas_call(
        _combine_body,
        out_shape=(jax.ShapeDtypeStruct((m, d), F32), jax.ShapeDtypeStruct((m, d), BF16)),
        grid_spec=pltpu.PrefetchScalarGridSpec(
            num_scalar_prefetch=1,
            grid=(m // MOE_TT,),
            in_specs=[tok(d), tok(LANES), pl.BlockSpec((1, d), lambda i, dest: (0, 0)),
                      pl.BlockSpec(memory_space=pl.ANY)],
            out_specs=(tok(d), tok(d)),
            scratch_shapes=[pltpu.VMEM((2, TOP_K * MOE_TT, d), F32),
                            pltpu.SemaphoreType.DMA((2,))]),
        compiler_params=_cparams(("arbitrary",)),
        name="moe_combine",
    )(dest, x, route, norm.reshape(1, d), y)


def moe_ffn(x, norm2, rg_w, rg_b, re_w, re_b, w1, w3, w2, norm3):
    m, d = x.shape
    h, route = moe_router(x, norm2, rg_w, rg_b, re_w, re_b)
    flat_e = route[:, :TOP_K].astype(jnp.int32).reshape(m * TOP_K)
    rank, counts = moe_rank(flat_e)
    padded = (counts + MOE_ROWS - 1) // MOE_ROWS * MOE_ROWS
    pend = jnp.cumsum(padded)
    pstart = pend - padded
    dest = (pstart[flat_e] + rank).astype(jnp.int32)
    n_blocks = (m * TOP_K + N_EXPERTS * (MOE_ROWS - 1) + MOE_ROWS - 1) // MOE_ROWS
    nblk = (pend[-1:] // MOE_ROWS).astype(jnp.int32)
    starts = jnp.arange(n_blocks) * MOE_ROWS
    blk_e = jnp.minimum(jnp.sum(pend[None, :] <= starts[:, None], axis=1), N_EXPERTS - 1).astype(jnp.int32)
    xs = moe_scatter(h, dest, jnp.zeros((n_blocks * MOE_ROWS, d), h.dtype))
    y = moe_experts(xs, blk_e, nblk, cast_bf16(w1), cast_bf16(w3), cast_bf16(w2))
    return moe_combine(x, route, y, dest, norm3)


def kernel(x, p, norm1, w_in, q_norm, k_norm, cmp_pos, cmp_w1, cmp_w2, gm_norm, gm_ws, gm_bs,
           s5_lambda_re, s5_lambda_im, s5_log_dt, s5_b_re, s5_b_im, s5_c_re, s5_c_im, s5_d,
           s5_w_glu, s5_b_glu, w_branch, w_out, norm2, router_group_w, router_group_b,
           router_expert_w, router_expert_b, expert_w1, expert_w3, expert_w2, norm3, w_ple,
           w_ple_gate):
    b, l, d = x.shape
    m = b * l
    xf = x.reshape(m, d)
    q_cols = HEADS * HEAD_DIM
    kv_cols = N_BRANCH * 2 * KV_HEADS * HEAD_DIM
    ng_cols = HEADS * N_BRANCH
    splits = [0, q_cols, q_cols + kv_cols, q_cols + kv_cols + ng_cols]
    splits.append(splits[-1] + 2 * GM_WIDTH)
    splits.append(splits[-1] + S5_WIDTH)
    splits.append(splits[-1] + 3 * d)
    per_head = GROUP * N_BRANCH
    for i in range(p.shape[0]):
        h = rms_cast(xf, norm1[i])
        w = w_in[i]
        w_ng = jnp.pad(w[:, splits[2]:splits[3]].reshape(d, KV_HEADS, per_head),
                       ((0, 0), (0, 0), (0, LANES - per_head))).reshape(d, KV_HEADS * LANES)
        w_all = jnp.concatenate([w[:, :splits[2]], w[:, splits[3]:], w_ng], axis=1).astype(BF16)
        widths = [q_cols, kv_cols, 2 * GM_WIDTH, S5_WIDTH, 3 * d, KV_HEADS * LANES]
        offs = [sum(widths[:k]) for k in range(len(widths))]
        proj = lambda k, name: mm_plain(h, w_all, col_off=offs[k], n=widths[k], name=name)
        q_raw = proj(0, "proj_q")
        kv_raw = proj(1, "proj_kv")
        gm_raw = proj(2, "proj_gm")
        s5_raw = proj(3, "proj_s5")
        mg_raw = proj(4, "proj_merge")
        ng_raw = proj(5, "proj_gate")

        o_gm = gmlp_mixer(gm_raw, gm_norm[i], gm_ws[i], gm_bs[i])
        o_s5 = s5_mixer(s5_raw, b, s5_lambda_re[i], s5_lambda_im[i], s5_log_dt[i], s5_b_re[i],
                        s5_b_im[i], s5_c_re[i], s5_c_im[i], s5_d[i], s5_w_glu[i], s5_b_glu[i])
        o_att = nsa_mixer(q_raw, kv_raw, ng_raw, b, q_norm[i], k_norm[i], cmp_pos[i], cmp_w1[i],
                          cmp_w2[i])
        wb = cast_bf16(w_branch[i])
        merged = mm_merge([o_gm, o_s5, o_att], [wb[0], wb[1], wb[2]], mg_raw)
        xf = mm_residual(merged, cast_bf16(w_out[i]), xf)
        xf, h3 = moe_ffn(xf, norm2[i], router_group_w[i], router_group_b[i], router_expert_w[i],
                         router_expert_b[i], expert_w1[i], expert_w3[i], expert_w2[i], norm3[i])
        xf = mm_ple(h3, cast_bf16(w_ple_gate[i]), p[i].reshape(m, -1).astype(BF16),
                    w_ple[i].astype(BF16), xf)
    return xf.reshape(b, l, d)
```

```python
import functools

import jax
import jax.numpy as jnp
from jax import lax
from jax.experimental import pallas as pl
from jax.experimental.pallas import tpu as pltpu

F32 = jnp.float32
BF16 = jnp.bfloat16

NORM_EPS = 1e-6
LANES = 128
VMEM_LIMIT = 52 * 1024 * 1024

HEADS = 16
KV_HEADS = 4
GROUP = HEADS // KV_HEADS
HEAD_DIM = 128
ROPE_THETA = 500000.0
ROPE_DIMS = HEAD_DIM // 4
ROPE_HALF = ROPE_DIMS // 2
CMP_LEN = 32
CMP_STRIDE = 16
SEL_LEN = 64
SEL_TOPN = 16
WINDOW = 512
N_BRANCH = 3
NEG = -1e30
ATT_SCALE = HEAD_DIM ** -0.5 * 1.4426950408889634
ATT_TQ = 128
ATT_TK = 1024
ONES_ROWS = 16

GM_WIDTH = 2048
GM_GROUPS = 16
GM_CHUNK = 128

S5_WIDTH = 2048
S5_GROUP_DIM = 16
S5_GROUPS = S5_WIDTH // S5_GROUP_DIM
S5_STATE = 64
S5_TILE_GROUPS = LANES // S5_GROUP_DIM
S5_TILE_STATES = S5_TILE_GROUPS * S5_STATE
S5_BLOCK = 8
S5_ROWS = 512

N_GROUPS = 4
PER_GROUP = 8
N_EXPERTS = N_GROUPS * PER_GROUP
TOP_K = 2
MOE_ROWS = 256
RANK_T = 512
MOE_TT = 128


def _cparams(sem):
    return pltpu.CompilerParams(dimension_semantics=sem, vmem_limit_bytes=VMEM_LIMIT)


def _dot(a, b):
    return jnp.dot(a, b, preferred_element_type=F32)


def _dot_nt(a, b):
    return lax.dot_general(a, b, (((1,), (1,)), ((), ())), preferred_element_type=F32)


def _rms_body(x_ref, g_ref, o_ref):
    x = x_ref[...]
    y = x * lax.rsqrt(jnp.mean(x * x, axis=-1, keepdims=True) + NORM_EPS)
    o_ref[...] = (y * g_ref[...]).astype(o_ref.dtype)


def rms_cast(x, g, tm=256):
    m, d = x.shape
    return pl.pallas_call(
        _rms_body,
        out_shape=jax.ShapeDtypeStruct((m, d), BF16),
        grid=(m // tm,),
        in_specs=[pl.BlockSpec((tm, d), lambda i: (i, 0)),
                  pl.BlockSpec((1, d), lambda i: (0, 0))],
        out_specs=pl.BlockSpec((tm, d), lambda i: (i, 0)),
        compiler_params=_cparams(("parallel",)),
        name="rms_cast",
    )(x, g.reshape(1, d))


def _mm_plain_body(a_ref, w_ref, o_ref):
    o_ref[...] = _dot(a_ref[...], w_ref[...]).astype(o_ref.dtype)


def _mm_glu_body(a_ref, w_ref, y_ref, b_ref, o_ref):
    acc = _dot(a_ref[...].astype(BF16), w_ref[...]) + b_ref[...]
    o_ref[...] = (y_ref[...] * jax.nn.sigmoid(acc)).astype(o_ref.dtype)


def _mm_res_body(a_ref, w_ref, x_ref, o_ref):
    o_ref[...] = x_ref[...] + _dot(a_ref[...], w_ref[...])


def _mm_merge_body(a0, a1, a2, w0, w1, w2, g0, g1, g2, o_ref):
    acc = jax.nn.sigmoid(g0[...]) * _dot(a0[...], w0[...])
    acc = acc + jax.nn.sigmoid(g1[...]) * _dot(a1[...], w1[...])
    acc = acc + jax.nn.sigmoid(g2[...]) * _dot(a2[...], w2[...])
    o_ref[...] = acc.astype(o_ref.dtype)


def _mm_ple_body(a_ref, w_ref, p_ref, wp_ref, x_ref, o_ref):
    gate = jax.nn.sigmoid(_dot(a_ref[...], w_ref[...]))
    o_ref[...] = x_ref[...] + _dot(p_ref[...], wp_ref[...]) * gate


def _lhs_spec(tm, k):
    return pl.BlockSpec((tm, k), lambda j, i: (i, 0))


def _rhs_spec(k, tn):
    return pl.BlockSpec((k, tn), lambda j, i: (0, j))


def _tile_spec(tm, tn, col_blocks=0):
    return pl.BlockSpec((tm, tn), lambda j, i: (i, j + col_blocks))


def _mm_call(body, args, in_specs, m, n, tm, tn, out_dtype, name):
    return pl.pallas_call(
        body,
        out_shape=jax.ShapeDtypeStruct((m, n), out_dtype),
        grid=(n // tn, m // tm),
        in_specs=in_specs,
        out_specs=pl.BlockSpec((tm, tn), lambda j, i: (i, j)),
        compiler_params=_cparams(("parallel", "parallel")),
        name=name,
    )(*args)


def _pick(n, prefs):
    for t in prefs:
        if n % t == 0:
            return t
    return n


def mm_plain(a, w, col_off=0, n=None, out_dtype=F32, name="mm"):
    m, k = a.shape
    n = w.shape[1] if n is None else n
    tm, tn = _pick(m, (512, 256, 128)), _pick(n, (1024, 512, 256, 128))
    assert col_off % tn == 0
    rhs = pl.BlockSpec((k, tn), lambda j, i: (0, j + col_off // tn))
    return _mm_call(_mm_plain_body, (a, w), [_lhs_spec(tm, k), rhs], m, n, tm, tn, out_dtype, name)


def _drop_cols_body(x_ref, o_ref, *, cut, skip):
    x = x_ref[...]
    o_ref[:, :cut] = x[:, :cut].astype(o_ref.dtype)
    o_ref[:, cut:] = x[:, cut + skip:].astype(o_ref.dtype)


def cast_drop_cols(w, cut, skip, tr=64):
    rows, cols = w.shape
    return pl.pallas_call(
        functools.partial(_drop_cols_body, cut=cut, skip=skip),
        out_shape=jax.ShapeDtypeStruct((rows, cols - skip), BF16),
        grid=(rows // tr,),
        in_specs=[pl.BlockSpec((tr, cols), lambda i: (i, 0))],
        out_specs=pl.BlockSpec((tr, cols - skip), lambda i: (i, 0)),
        compiler_params=_cparams(("parallel",)),
        name="cast_drop_cols",
    )(w)


def _cast_body(x_ref, o_ref):
    o_ref[...] = x_ref[...].astype(o_ref.dtype)


def cast_bf16(w, block_bytes=8 * 1024 * 1024):
    cols = w.shape[-1]
    x = w.reshape(-1, cols)
    rows = x.shape[0]
    tr = next(t for t in (4096, 2048, 1024, 512, 256, 128, 64, 32, 16)
              if rows % t == 0 and t * cols * 4 <= block_bytes)
    out = pl.pallas_call(
        _cast_body,
        out_shape=jax.ShapeDtypeStruct((rows, cols), BF16),
        grid=(rows // tr,),
        in_specs=[pl.BlockSpec((tr, cols), lambda i: (i, 0))],
        out_specs=pl.BlockSpec((tr, cols), lambda i: (i, 0)),
        compiler_params=_cparams(("parallel",)),
        name="cast_bf16",
    )(x)
    return out.reshape(w.shape)


def mm_glu(y, w, b):
    m, k = y.shape
    n = w.shape[1]
    tm, tn = _pick(m, (512, 256, 128)), _pick(n, (1024, 512, 256, 128))
    specs = [_lhs_spec(tm, k), _rhs_spec(k, tn), _tile_spec(tm, tn),
             pl.BlockSpec((1, tn), lambda j, i: (0, j))]
    return _mm_call(_mm_glu_body, (y, w, y, b.reshape(1, n)), specs, m, n, tm, tn, BF16, "mm_glu")


def mm_residual(a, w, x):
    m, k = a.shape
    n = w.shape[1]
    tm, tn = _pick(m, (512, 256, 128)), _pick(n, (1024, 512, 256, 128))
    specs = [_lhs_spec(tm, k), _rhs_spec(k, tn), _tile_spec(tm, tn)]
    return _mm_call(_mm_res_body, (a, w, x), specs, m, n, tm, tn, F32, "mm_residual")


def mm_merge(o_list, w_list, mg_raw):
    m, k = o_list[0].shape
    n = w_list[0].shape[1]
    tm, tn = _pick(m, (512, 256, 128)), _pick(n, (512, 256, 128))
    specs = ([_lhs_spec(tm, k)] * 3 + [_rhs_spec(k, tn)] * 3
             + [_tile_spec(tm, tn, c * (n // tn)) for c in range(3)])
    return _mm_call(_mm_merge_body, (*o_list, *w_list, mg_raw, mg_raw, mg_raw), specs,
                    m, n, tm, tn, BF16, "mm_merge")


def mm_ple(h, wg, p, wp, x):
    m, k = h.shape
    n = wg.shape[1]
    kp = p.shape[1]
    tm, tn = _pick(m, (512, 256, 128)), _pick(n, (1024, 512, 256, 128))
    specs = [_lhs_spec(tm, k), _rhs_spec(k, tn), _lhs_spec(tm, kp), _rhs_spec(kp, tn),
             _tile_spec(tm, tn)]
    return _mm_call(_mm_ple_body, (h, wg, p, wp, x), specs, m, n, tm, tn, F32, "mm_ple")


def _gmlp_body(z_ref, gn_ref, w_ref, b_ref, o_ref, *, rows):
    z = jax.nn.gelu(z_ref[...])
    u = z[:, :GM_WIDTH]
    v = z[:, GM_WIDTH:]
    v = v * lax.rsqrt(jnp.mean(v * v, axis=-1, keepdims=True) + NORM_EPS) * gn_ref[...]
    vb = v.astype(BF16)
    for c in range(rows // GM_CHUNK):
        r0 = c * GM_CHUNK
        for g in range(GM_GROUPS):
            c0 = g * LANES
            s = _dot(w_ref[g], vb[r0:r0 + GM_CHUNK, c0:c0 + LANES]) + b_ref[:, c0:c0 + LANES]
            o_ref[r0:r0 + GM_CHUNK, c0:c0 + LANES] = (
                u[r0:r0 + GM_CHUNK, c0:c0 + LANES] * s).astype(o_ref.dtype)


def gmlp_mixer(z, gm_norm, ws, bs, rows=256):
    m = z.shape[0]
    tri = jnp.tril(jnp.ones((GM_CHUNK, GM_CHUNK), dtype=bool))
    w = jnp.where(tri[None], ws, 0.0).astype(BF16)
    bias = jnp.repeat(bs.T, LANES, axis=1)
    return pl.pallas_call(
        functools.partial(_gmlp_body, rows=rows),
        out_shape=jax.ShapeDtypeStruct((m, GM_WIDTH), BF16),
        grid=(m // rows,),
        in_specs=[pl.BlockSpec((rows, 2 * GM_WIDTH), lambda i: (i, 0)),
                  pl.BlockSpec((1, GM_WIDTH), lambda i: (0, 0)),
                  pl.BlockSpec((GM_GROUPS, GM_CHUNK, GM_CHUNK), lambda i: (0, 0, 0)),
                  pl.BlockSpec((GM_CHUNK, GM_WIDTH), lambda i: (0, 0))],
        out_specs=pl.BlockSpec((rows, GM_WIDTH), lambda i: (i, 0)),
        compiler_params=_cparams(("parallel",)),
        name="gmlp",
    )(z, gm_norm.reshape(1, GM_WIDTH), w, bias)


def _shift_rows(x, s):
    t = x.shape[0]
    if s % 8 == 0:
        return jnp.concatenate([jnp.zeros((s, x.shape[1]), x.dtype), x[:t - s]], axis=0)
    rolled = pltpu.roll(x, s, 0)
    row = lax.broadcasted_iota(jnp.int32, x.shape, 0)
    return jnp.where(row >= s, rolled, 0.0)


def _group_diag(table, rows_per_group, lanes_per_group):
    full = jnp.concatenate([table] * S5_TILE_GROUPS, axis=0)
    rg = lax.shift_right_logical(lax.broadcasted_iota(jnp.int32, full.shape, 0),
                                 rows_per_group.bit_length() - 1)
    lg = lax.shift_right_logical(lax.broadcasted_iota(jnp.int32, full.shape, 1),
                                 lanes_per_group.bit_length() - 1)
    return jnp.where(rg == lg, full, 0.0).astype(BF16)


def _s5_body(u_ref, kt_ref, qt_ref, rt_ref, are_ref, aim_ref, d_ref, o_ref,
             w_ref, q_ref, r_ref, hre_ref, him_ref, *, rows, levels):
    tb, hg, ns = S5_BLOCK, S5_GROUP_DIM, S5_STATE

    @pl.when(pl.program_id(2) == 0)
    def _():
        hre_ref[...] = jnp.zeros_like(hre_ref)
        him_ref[...] = jnp.zeros_like(him_ref)
        lag_blocks = [_group_diag(kt_ref[0, d], hg, hg) for d in range(tb)]
        zero = jnp.zeros((LANES, LANES), BF16)
        for ip in range(tb):
            for i in range(tb):
                w_ref[ip * LANES:(ip + 1) * LANES, i * LANES:(i + 1) * LANES] = (
                    lag_blocks[i - ip] if i >= ip else zero)
        for ip in range(tb):
            for ri in range(2):
                q_ref[ip * LANES:(ip + 1) * LANES, ri * S5_TILE_STATES:(ri + 1) * S5_TILE_STATES] = (
                    _group_diag(qt_ref[0, ip, ri], hg, ns))
        for ri in range(2):
            for i in range(tb):
                r_ref[ri * S5_TILE_STATES:(ri + 1) * S5_TILE_STATES, i * LANES:(i + 1) * LANES] = (
                    _group_diag(rt_ref[0, ri, i], ns, hg))

    us = [u_ref[pl.ds(i, rows, stride=S5_BLOCK), :] for i in range(S5_BLOCK)]
    ucat = jnp.concatenate([x.astype(BF16) for x in us], axis=1)
    s = _dot(ucat, q_ref[...])
    xr = s[:, :S5_TILE_STATES]
    xi = s[:, S5_TILE_STATES:]
    are = are_ref[0]
    aim = aim_ref[0]
    hr = hre_ref[0:1, :]
    hi = him_ref[0:1, :]
    a_r = are[0:1, :]
    a_i = aim[0:1, :]
    first = lax.broadcasted_iota(jnp.int32, xr.shape, 0) == 0
    xr = xr + jnp.where(first, a_r * hr - a_i * hi, 0.0)
    xi = xi + jnp.where(first, a_r * hi + a_i * hr, 0.0)
    for k in range(levels):
        sh = 1 << k
        cr = are[k:k + 1, :]
        ci = aim[k:k + 1, :]
        sr = _shift_rows(xr, sh)
        si = _shift_rows(xi, sh)
        xr, xi = xr + cr * sr - ci * si, xi + cr * si + ci * sr
    hre_ref[0:1, :] = xr[rows - 1:rows, :]
    him_ref[0:1, :] = xi[rows - 1:rows, :]
    xpr = _shift_rows(xr, 1) + jnp.where(first, hr, 0.0)
    xpi = _shift_rows(xi, 1) + jnp.where(first, hi, 0.0)
    xprev = jnp.concatenate([xpr, xpi], axis=1).astype(BF16)
    y = _dot(ucat, w_ref[...]) + _dot(xprev, r_ref[...])
    d = d_ref[...]
    for i in range(S5_BLOCK):
        yi = y[:, i * LANES:(i + 1) * LANES] + d * us[i]
        o_ref[pl.ds(i, rows, stride=S5_BLOCK), :] = jax.nn.gelu(yi)


def _s5_discretize(lam_re, lam_im, log_dt, b_re, b_im):
    dt = jnp.exp(log_dt.astype(F32))[:, None]
    lre = jnp.minimum(lam_re.astype(F32), -1e-4)
    lim = lam_im.astype(F32)
    mag = jnp.exp(lre * dt)
    a_re = mag * jnp.cos(lim * dt)
    a_im = mag * jnp.sin(lim * dt)
    den = lre * lre + lim * lim
    nr = a_re - 1.0
    f_re = (nr * lre + a_im * lim) / den
    f_im = (a_im * lre - nr * lim) / den
    br = b_re.astype(F32)
    bi = b_im.astype(F32)
    bb_re = f_re[..., None] * br - f_im[..., None] * bi
    bb_im = f_re[..., None] * bi + f_im[..., None] * br
    return a_re, a_im, bb_re, bb_im


def _s5_block_weights(a_re, a_im, bb_re, bb_im, c_re, c_im, levels):
    hp = lax.Precision.HIGHEST
    ng, ns = a_re.shape
    tb, gl, hg = S5_BLOCK, S5_TILE_GROUPS, S5_GROUP_DIM
    nt = ng // gl
    pr, pi = [jnp.ones_like(a_re)], [jnp.zeros_like(a_re)]
    for _ in range(tb):
        r, i = pr[-1], pi[-1]
        pr.append(r * a_re - i * a_im)
        pi.append(r * a_im + i * a_re)
    pw_re, pw_im = jnp.stack(pr), jnp.stack(pi)
    ab_re = pw_re[:tb, :, :, None] * bb_re[None] - pw_im[:tb, :, :, None] * bb_im[None]
    ab_im = pw_re[:tb, :, :, None] * bb_im[None] + pw_im[:tb, :, :, None] * bb_re[None]
    cr, ci = c_re.astype(F32), c_im.astype(F32)
    kern = (jnp.einsum('gcp,dgpk->dgck', cr, ab_re, precision=hp)
            - jnp.einsum('gcp,dgpk->dgck', ci, ab_im, precision=hp))
    kt = kern.reshape(tb, nt, gl, hg, hg).transpose(1, 0, 4, 2, 3).reshape(nt, tb, hg, gl * hg)
    q = jnp.stack([ab_re[::-1], ab_im[::-1]], axis=1).reshape(tb, 2, nt, gl, ns, hg)
    qt = q.transpose(2, 0, 1, 5, 3, 4).reshape(nt, tb, 2, hg, gl * ns)
    nr, ni = pw_re[1:, :, None, :], pw_im[1:, :, None, :]
    r = jnp.stack([cr[None] * nr - ci[None] * ni, -(cr[None] * ni + ci[None] * nr)])
    r = r.reshape(2, tb, nt, gl, hg, ns)
    rt = r.transpose(2, 0, 1, 5, 3, 4).reshape(nt, 2, tb, ns, gl * hg)
    sr, si = [pw_re[tb].reshape(-1)], [pw_im[tb].reshape(-1)]
    for _ in range(15):
        x, y = sr[-1], si[-1]
        sr.append(x * x - y * y)
        si.append(2.0 * x * y)
    assert levels <= 16
    scan_re = jnp.stack(sr).reshape(16, nt, gl * ns).transpose(1, 0, 2)
    scan_im = jnp.stack(si).reshape(16, nt, gl * ns).transpose(1, 0, 2)
    return kt, qt, rt, scan_re, scan_im


def s5_mixer(u, batch, lam_re, lam_im, log_dt, b_re, b_im, c_re, c_im, d_skip, w_glu, b_glu):
    m = u.shape[0]
    seq = m // batch
    nt = S5_GROUPS // S5_TILE_GROUPS
    nblk = seq // S5_BLOCK
    rows = min(S5_ROWS, nblk)
    levels = rows.bit_length() - 1
    assert rows == 1 << levels and nblk % rows == 0
    a_re, a_im, bb_re, bb_im = _s5_discretize(lam_re, lam_im, log_dt, b_re, b_im)
    kt, qt, rt, scan_re, scan_im = _s5_block_weights(a_re, a_im, bb_re, bb_im, c_re, c_im, levels)

    nchunk = nblk // rows
    tok = rows * S5_BLOCK
    tspec = lambda shape: pl.BlockSpec((1,) + shape, lambda b, j, t: (j,) + (0,) * len(shape))
    kin = S5_BLOCK * LANES
    y = pl.pallas_call(
        functools.partial(_s5_body, rows=rows, levels=levels),
        out_shape=jax.ShapeDtypeStruct((m, S5_WIDTH), F32),
        grid=(batch, nt, nchunk),
        in_specs=[pl.BlockSpec((tok, LANES), lambda b, j, t: (b * nchunk + t, j)),
                  tspec(kt.shape[1:]), tspec(qt.shape[1:]), tspec(rt.shape[1:]),
                  tspec((16, S5_TILE_STATES)), tspec((16, S5_TILE_STATES)),
                  pl.BlockSpec((1, LANES), lambda b, j, t: (0, j))],
        out_specs=pl.BlockSpec((tok, LANES), lambda b, j, t: (b * nchunk + t, j)),
        scratch_shapes=[pltpu.VMEM((kin, kin), BF16), pltpu.VMEM((kin, 2 * S5_TILE_STATES), BF16),
                        pltpu.VMEM((2 * S5_TILE_STATES, kin), BF16),
                        pltpu.VMEM((8, S5_TILE_STATES), F32), pltpu.VMEM((8, S5_TILE_STATES), F32)],
        compiler_params=_cparams(("parallel", "parallel", "arbitrary")),
        name="s5_scan",
    )(u, kt, qt, rt, scan_re, scan_im, d_skip.reshape(1, S5_WIDTH))
    return mm_glu(y, w_glu.astype(BF16), b_glu)


def _rope_tables(pos):
    inv_freq = ROPE_THETA ** (-jnp.arange(ROPE_HALF, dtype=F32) / ROPE_HALF)
    ang = pos.astype(F32)[:, None] * inv_freq[None, :]
    cos, sin = jnp.cos(ang), jnp.sin(ang)
    n = pos.shape[0]
    rest = HEAD_DIM - ROPE_DIMS
    c = jnp.concatenate([cos, cos, jnp.ones((n, rest), F32)], axis=1)
    s1 = jnp.concatenate([-sin, jnp.zeros((n, HEAD_DIM - ROPE_HALF), F32)], axis=1)
    s2 = jnp.concatenate([jnp.zeros((n, ROPE_HALF), F32), sin, jnp.zeros((n, rest), F32)], axis=1)
    return c, s1, s2


def _norm_rope(x, gain, c, s1, s2):
    x = x * lax.rsqrt(jnp.mean(x * x, axis=-1, keepdims=True) + NORM_EPS) * gain
    return (x * c + pltpu.roll(x, HEAD_DIM - ROPE_HALF, 1) * s1 + pltpu.roll(x, ROPE_HALF, 1) * s2)


def _kv_prep_body(ks_ref, vs_ref, kw_ref, vw_ref, kn_ref, c_ref, s1_ref, s2_ref,
                  oks_ref, ovs_ref, okw_ref, ovw_ref):
    c, s1, s2 = c_ref[...], s1_ref[...], s2_ref[...]
    oks_ref[...] = _norm_rope(ks_ref[...], kn_ref[1:2, :], c, s1, s2).astype(BF16)
    okw_ref[...] = _norm_rope(kw_ref[...], kn_ref[2:3, :], c, s1, s2).astype(BF16)
    vt = vs_ref[...].T
    ones = jnp.ones((ONES_ROWS, vt.shape[1]), F32)
    ovs_ref[...] = jnp.concatenate([vt, ones], axis=0).astype(BF16)
    ovw_ref[...] = jnp.concatenate([vw_ref[...].T, ones], axis=0).astype(BF16)


def _cmp_prep_body(k_ref, v_ref, pos_ref, w1_ref, w2_ref, kn_ref, c_ref, s1_ref, s2_ref,
                   ok_ref, ov_ref, *, ncp):
    half = CMP_LEN // 2
    outs = []
    for which, t_ref in enumerate((k_ref, v_ref)):
        lo = jnp.zeros((ncp, HEAD_DIM), F32)
        hi = jnp.zeros((ncp, HEAD_DIM), F32)
        for j in range(half):
            tj = t_ref[pl.ds(j, ncp, stride=CMP_STRIDE), :]
            a = (tj + pos_ref[which, j:j + 1, :]).astype(BF16)
            b = (tj + pos_ref[which, half + j:half + j + 1, :]).astype(BF16)
            lo = lo + _dot(a, w1_ref[which, j])
            hi = hi + _dot(b, w1_ref[which, half + j])
        pre = lo + pltpu.roll(hi, ncp - 1, 0)
        outs.append(_dot(jax.nn.gelu(pre).astype(BF16), w2_ref[which]))
    valid = lax.broadcasted_iota(jnp.int32, (ncp, HEAD_DIM), 0) < ncp - 1
    kc = _norm_rope(outs[0], kn_ref[0:1, :], c_ref[...], s1_ref[...], s2_ref[...])
    ok_ref[...] = jnp.where(valid, kc, 0.0).astype(BF16)
    ov_ref[...] = jnp.where(valid, outs[1], 0.0).T.astype(BF16)


def _attn_body(q_ref, gate_ref, c_ref, s1_ref, s2_ref, qn_ref, kc_ref, vct_ref,
               ks_ref, vst_ref, kw_ref, vwt_ref, ovt_ref, o_ref, bias_ref, s_ref, *, tq, ncp):
    s0 = pl.program_id(2) * tq
    cols = GROUP * tq
    qpos = s0 + (lax.broadcasted_iota(jnp.int32, (1, cols), 1) & (tq - 1))
    qpos1 = qpos[:, :tq]
    c, s1, s2 = c_ref[...], s1_ref[...], s2_ref[...]
    qn = qn_ref[...]
    q_t = jnp.concatenate(
        [(_norm_rope(q_ref[:, g * HEAD_DIM:(g + 1) * HEAD_DIM], qn, c, s1, s2) * ATT_SCALE).T.astype(BF16)
         for g in range(GROUP)], axis=1)

    n_id = lax.broadcasted_iota(jnp.int32, (ncp, 1), 0)
    cbias = jnp.where(n_id * CMP_STRIDE + (CMP_LEN - 1) <= qpos1, 0.0, NEG)
    sc = _dot(kc_ref[...], q_t) + jnp.concatenate([cbias] * GROUP, axis=1)
    mx = jnp.max(sc, axis=0, keepdims=True)
    e = jnp.exp2(sc - mx)
    inv = jnp.where(qpos >= CMP_LEN - 1, 1.0 / jnp.sum(e, axis=0, keepdims=True), 0.0)
    p = e * inv
    o_cmp = _dot(vct_ref[...], p.astype(BF16))

    span = WINDOW + tq
    w0 = pl.multiple_of(jnp.maximum(s0 - WINDOW, 0), tq)
    kp = w0 + lax.broadcasted_iota(jnp.int32, (span, 1), 0)
    wbias = jnp.where((kp <= qpos1) & (kp > qpos1 - WINDOW), 0.0, NEG)
    sw = _dot(kw_ref[pl.ds(w0, span), :], q_t) + jnp.concatenate([wbias] * GROUP, axis=1)
    ew = jnp.exp2(sw - jnp.max(sw, axis=0, keepdims=True))
    acc_win = _dot(vwt_ref[:, pl.ds(w0, span)], ew.astype(BF16))
    o_win = acc_win[:HEAD_DIM] / acc_win[HEAD_DIM:HEAD_DIM + 1]

    pg = p[:, 0:tq]
    for g in range(1, GROUP):
        pg = pg + p[:, g * tq:(g + 1) * tq]
    p_hi = pg.astype(BF16)
    p_lo = (pg - p_hi.astype(F32)).astype(BF16)
    imp = _dot(ovt_ref[...], p_hi) + _dot(ovt_ref[...], p_lo)
    blk = lax.broadcasted_iota(jnp.int32, (LANES, tq), 0)
    cur = lax.shift_right_logical(qpos1, 6)
    forced = (blk == 0) | (blk == cur) | (blk == cur - 1)
    val = jnp.where(forced, -jnp.inf, jnp.where(blk <= cur, imp, -1e9))
    blk_f = blk.astype(F32)
    sel = jnp.where(forced, 1.0, 0.0)
    for _ in range(SEL_TOPN - 3):
        top = jnp.max(val, axis=0, keepdims=True)
        idx = jnp.min(jnp.where(val == top, blk_f, float(LANES)), axis=0, keepdims=True)
        hit = blk_f == idx
        val = jnp.where(hit, -jnp.inf, val)
        sel = jnp.where(hit, 1.0, sel)
    bias_ref[...] = (sel - 1.0) * (-NEG)

    blocks_per_tile = ATT_TK // SEL_LEN

    def score_tile(kt, m8, diagonal):
        k0 = pl.multiple_of(kt * ATT_TK, ATT_TK)
        s = _dot(ks_ref[pl.ds(k0, ATT_TK), :], q_t)
        b8 = bias_ref[pl.ds(pl.multiple_of(kt * blocks_per_tile, blocks_per_tile), blocks_per_tile), :]
        bias = jnp.concatenate([jnp.broadcast_to(b8[j:j + 1, :], (SEL_LEN, tq))
                                for j in range(blocks_per_tile)], axis=0)
        if diagonal:
            kpos = k0 + lax.broadcasted_iota(jnp.int32, (ATT_TK, 1), 0)
            bias = jnp.where(kpos <= qpos1, bias, NEG)
        s = s + jnp.concatenate([bias] * GROUP, axis=1)
        s_ref[pl.ds(k0, ATT_TK), :] = s
        return jnp.maximum(m8, jnp.max(s.reshape(ATT_TK // 8, 8, cols), axis=0))

    n_full = lax.div(s0, ATT_TK)
    m8 = lax.fori_loop(0, n_full, lambda kt, mm: score_tile(kt, mm, False), jnp.full((8, cols), NEG, F32))
    m_sel = jnp.max(score_tile(n_full, m8, True), axis=0, keepdims=True)

    def value_tile(kt, acc):
        k0 = pl.multiple_of(kt * ATT_TK, ATT_TK)
        pe = jnp.exp2(s_ref[pl.ds(k0, ATT_TK), :] - m_sel).astype(BF16)
        return acc + _dot(vst_ref[:, pl.ds(k0, ATT_TK)], pe)

    acc_sel = lax.fori_loop(0, n_full + 1, value_tile, jnp.zeros((HEAD_DIM + ONES_ROWS, cols), F32))
    o_sel = acc_sel[:HEAD_DIM] / acc_sel[HEAD_DIM:HEAD_DIM + 1]

    gates = jax.nn.sigmoid(gate_ref[...]).T
    for g in range(GROUP):
        cs = slice(g * tq, (g + 1) * tq)
        r = g * N_BRANCH
        out = (gates[r:r + 1, :] * o_cmp[:, cs] + gates[r + 1:r + 2, :] * o_sel[:, cs]
               + gates[r + 2:r + 3, :] * o_win[:, cs])
        o_ref[:, g * HEAD_DIM:(g + 1) * HEAD_DIM] = out.T.astype(o_ref.dtype)


def nsa_mixer(q_raw, kv_raw, gate_raw, batch, q_norm, k_norm, cmp_pos, cmp_w1, cmp_w2):
    m = q_raw.shape[0]
    seq = m // batch
    ncp = seq // CMP_STRIDE
    nsel = seq // SEL_LEN
    assert seq % ATT_TK == 0 and nsel <= LANES and seq >= WINDOW + ATT_TQ
    pos = jnp.arange(seq)
    c, s1, s2 = _rope_tables(pos)
    cc, cs1, cs2 = _rope_tables(jnp.arange(ncp) * CMP_STRIDE + CMP_LEN - 1)
    col = lambda branch, kv: (branch * 2 + kv) * KV_HEADS

    tl = _pick(seq, (1024, 512))
    nl = seq // tl
    kvspec = lambda base: pl.BlockSpec((tl, HEAD_DIM), lambda b, h, t: (b * nl + t, base + h))
    tab = pl.BlockSpec((tl, HEAD_DIM), lambda b, h, t: (t, 0))
    outspec = pl.BlockSpec((None, None, tl, HEAD_DIM), lambda b, h, t: (b, h, t, 0))
    outspec_t = pl.BlockSpec((None, None, HEAD_DIM, tl), lambda b, h, t: (b, h, 0, t))
    kv_shape = jax.ShapeDtypeStruct((batch, KV_HEADS, seq, HEAD_DIM), BF16)
    kv_shape_t = jax.ShapeDtypeStruct((batch, KV_HEADS, HEAD_DIM, seq), BF16)
    vrows = HEAD_DIM + ONES_ROWS
    outspec_t1 = pl.BlockSpec((None, None, vrows, tl), lambda b, h, t: (b, h, 0, t))
    kv_shape_t1 = jax.ShapeDtypeStruct((batch, KV_HEADS, vrows, seq), BF16)
    k_s, v_s, k_w, v_w = pl.pallas_call(
        _kv_prep_body,
        out_shape=(kv_shape, kv_shape_t1, kv_shape, kv_shape_t1),
        grid=(batch, KV_HEADS, nl),
        in_specs=[kvspec(col(1, 0)), kvspec(col(1, 1)), kvspec(col(2, 0)), kvspec(col(2, 1)),
                  pl.BlockSpec((N_BRANCH, HEAD_DIM), lambda b, h, t: (0, 0)), tab, tab, tab],
        out_specs=(outspec, outspec_t1, outspec, outspec_t1),
        compiler_params=_cparams(("parallel", "parallel", "parallel")),
        name="nsa_kv_prep",
    )(kv_raw, kv_raw, kv_raw, kv_raw, k_norm, c, s1, s2)

    w1 = cmp_w1.reshape(2, CMP_LEN, HEAD_DIM, HEAD_DIM).astype(BF16)
    full = lambda shape: pl.BlockSpec(shape, lambda b, h: (0,) * len(shape))
    cshape = jax.ShapeDtypeStruct((batch, KV_HEADS, ncp, HEAD_DIM), BF16)
    cspec = pl.BlockSpec((None, None, ncp, HEAD_DIM), lambda b, h: (b, h, 0, 0))
    cshape_t = jax.ShapeDtypeStruct((batch, KV_HEADS, HEAD_DIM, ncp), BF16)
    cspec_t = pl.BlockSpec((None, None, HEAD_DIM, ncp), lambda b, h: (b, h, 0, 0))
    k_c, v_c = pl.pallas_call(
        functools.partial(_cmp_prep_body, ncp=ncp),
        out_shape=(cshape, cshape_t),
        grid=(batch, KV_HEADS),
        in_specs=[pl.BlockSpec((seq, HEAD_DIM), lambda b, h: (b, col(0, 0) + h)),
                  pl.BlockSpec((seq, HEAD_DIM), lambda b, h: (b, col(0, 1) + h)),
                  full((2, CMP_LEN, HEAD_DIM)), full((2, CMP_LEN, HEAD_DIM, HEAD_DIM)),
                  full((2, HEAD_DIM, HEAD_DIM)), full((N_BRANCH, HEAD_DIM)),
                  full((ncp, HEAD_DIM)), full((ncp, HEAD_DIM)), full((ncp, HEAD_DIM))],
        out_specs=(cspec, cspec_t),
        compiler_params=_cparams(("parallel", "parallel")),
        name="nsa_cmp_prep",
    )(kv_raw, kv_raw, cmp_pos, w1, cmp_w2.astype(BF16), k_norm, cc, cs1, cs2)

    cstart = jnp.arange(ncp)[:, None] * CMP_STRIDE
    sstart = jnp.arange(LANES)[None, :] * SEL_LEN
    overlap = jnp.maximum(jnp.minimum(cstart + CMP_LEN, sstart + SEL_LEN) - jnp.maximum(cstart, sstart), 0)
    overlap = jnp.where(jnp.arange(LANES)[None, :] < nsel, overlap, 0)
    overlap_t = (overlap.astype(F32) / CMP_STRIDE).astype(BF16).T

    tq = ATT_TQ
    nq = seq // tq
    qtab = pl.BlockSpec((tq, HEAD_DIM), lambda b, h, i: (i, 0))
    kvfull = lambda n: pl.BlockSpec((None, None, n, HEAD_DIM), lambda b, h, i: (b, h, 0, 0))
    kvfull_t = lambda n: pl.BlockSpec((None, None, HEAD_DIM, n), lambda b, h, i: (b, h, 0, 0))
    vfull_t = pl.BlockSpec((None, None, vrows, seq), lambda b, h, i: (b, h, 0, 0))
    return pl.pallas_call(
        functools.partial(_attn_body, tq=tq, ncp=ncp),
        out_shape=jax.ShapeDtypeStruct((m, HEADS * HEAD_DIM), BF16),
        grid=(batch, KV_HEADS, nq),
        in_specs=[pl.BlockSpec((tq, GROUP * HEAD_DIM), lambda b, h, i: (b * nq + i, h)),
                  pl.BlockSpec((tq, LANES), lambda b, h, i: (b * nq + i, h)),
                  qtab, qtab, qtab,
                  pl.BlockSpec((1, HEAD_DIM), lambda b, h, i: (0, 0)),
                  kvfull(ncp), kvfull_t(ncp), kvfull(seq), vfull_t, kvfull(seq), vfull_t,
                  pl.BlockSpec((LANES, ncp), lambda b, h, i: (0, 0))],
        out_specs=pl.BlockSpec((tq, GROUP * HEAD_DIM), lambda b, h, i: (b * nq + i, h)),
        scratch_shapes=[pltpu.VMEM((LANES, tq), F32), pltpu.VMEM((seq, GROUP * tq), F32)],
        compiler_params=_cparams(("parallel", "parallel", "arbitrary")),
        name="nsa_attn",
    )(q_raw, gate_raw, c, s1, s2, q_norm.reshape(1, HEAD_DIM), k_c, v_c, k_s, v_s, k_w, v_w, overlap_t)


def _router_body(x_ref, g_ref, whi_ref, wlo_ref, b_ref, h_ref, r_ref):
    x = x_ref[...]
    h = x * lax.rsqrt(jnp.mean(x * x, axis=-1, keepdims=True) + NORM_EPS) * g_ref[...]
    h_ref[...] = h
    h_hi = h.astype(BF16)
    h_lo = (h - h_hi.astype(F32)).astype(BF16)
    logits = (_dot(h_hi, whi_ref[...]) + _dot(h_lo, whi_ref[...]) + _dot(h_hi, wlo_ref[...])
              + b_ref[...])
    lane = lax.broadcasted_iota(jnp.int32, logits.shape, 1)
    lane_f = lane.astype(F32)
    none = float(LANES)
    is_g = lane < N_GROUPS
    glog = jnp.where(is_g, logits, -jnp.inf)
    gmax = jnp.max(glog, axis=-1, keepdims=True)
    gidx = jnp.min(jnp.where(glog == gmax, lane_f, none), axis=-1, keepdims=True)
    g_p = 1.0 / jnp.sum(jnp.where(is_g, jnp.exp(logits - gmax), 0.0), axis=-1, keepdims=True)
    e_lo = N_GROUPS + gidx * PER_GROUP
    elog = jnp.where((lane_f >= e_lo) & (lane_f < e_lo + PER_GROUP), logits, -jnp.inf)
    m1 = jnp.max(elog, axis=-1, keepdims=True)
    i1 = jnp.min(jnp.where(elog == m1, lane_f, none), axis=-1, keepdims=True)
    elog = jnp.where(lane_f == i1, -jnp.inf, elog)
    m2 = jnp.max(elog, axis=-1, keepdims=True)
    i2 = jnp.min(jnp.where(elog == m2, lane_f, none), axis=-1, keepdims=True)
    t = jnp.exp(m2 - m1)
    w1 = g_p / (1.0 + t)
    w2 = g_p * t / (1.0 + t)
    r_ref[...] = jnp.where(lane == 0, i1 - N_GROUPS,
                           jnp.where(lane == 1, i2 - N_GROUPS,
                                     jnp.where(lane == 2, w1, jnp.where(lane == 3, w2, 0.0))))


def moe_router(x, norm, rg_w, rg_b, re_w, re_b, tm=256):
    m, d = x.shape
    pad = LANES - N_GROUPS - N_EXPERTS
    wr = jnp.pad(jnp.concatenate([rg_w, re_w], axis=1).astype(F32), ((0, 0), (0, pad)))
    whi = wr.astype(BF16)
    wlo = (wr - whi.astype(F32)).astype(BF16)
    bias = jnp.pad(jnp.concatenate([rg_b, re_b]).astype(F32), (0, pad)).reshape(1, LANES)
    const = lambda shape: pl.BlockSpec(shape, lambda i: (0, 0))
    return pl.pallas_call(
        _router_body,
        out_shape=(jax.ShapeDtypeStruct((m, d), F32), jax.ShapeDtypeStruct((m, LANES), F32)),
        grid=(m // tm,),
        in_specs=[pl.BlockSpec((tm, d), lambda i: (i, 0)), const((1, d)),
                  const((d, LANES)), const((d, LANES)), const((1, LANES))],
        out_specs=(pl.BlockSpec((tm, d), lambda i: (i, 0)), pl.BlockSpec((tm, LANES), lambda i: (i, 0))),
        compiler_params=_cparams(("parallel",)),
        name="moe_router",
    )(x, norm.reshape(1, d), whi, wlo, bias)


def _rank_body(e_ref, rank_ref, cnt_ref, carry_ref):
    @pl.when(pl.program_id(0) == 0)
    def _():
        carry_ref[...] = jnp.zeros_like(carry_ref)

    e = e_ref[0]
    sub = lax.broadcasted_iota(jnp.int32, (N_EXPERTS, RANK_T), 0)
    onehot = jnp.where(sub == e, 1.0, 0.0)
    r = lax.broadcasted_iota(jnp.int32, (RANK_T, RANK_T), 0)
    c = lax.broadcasted_iota(jnp.int32, (RANK_T, RANK_T), 1)
    before = jnp.where(r < c, 1.0, 0.0).astype(BF16)
    prefix = _dot(onehot.astype(BF16), before)
    carry = carry_ref[:, 0:1]
    rank = jnp.sum(onehot * (prefix + carry), axis=0, keepdims=True)
    rank_ref[0] = rank.astype(jnp.int32)
    carry_ref[...] = carry_ref[...] + jnp.sum(onehot, axis=1, keepdims=True)
    cnt_ref[...] = carry_ref[...]


def moe_rank(flat_e):
    a = flat_e.shape[0]
    nt = a // RANK_T
    rank, cnt = pl.pallas_call(
        _rank_body,
        out_shape=(jax.ShapeDtypeStruct((nt, 1, RANK_T), jnp.int32),
                   jax.ShapeDtypeStruct((N_EXPERTS, LANES), F32)),
        grid=(nt,),
        in_specs=[pl.BlockSpec((1, 1, RANK_T), lambda i: (i, 0, 0))],
        out_specs=(pl.BlockSpec((1, 1, RANK_T), lambda i: (i, 0, 0)),
                   pl.BlockSpec((N_EXPERTS, LANES), lambda i: (0, 0))),
        scratch_shapes=[pltpu.VMEM((N_EXPERTS, LANES), F32)],
        compiler_params=_cparams(("arbitrary",)),
        name="moe_rank",
    )(flat_e.reshape(nt, 1, RANK_T))
    return rank.reshape(a), cnt[:, 0].astype(jnp.int32)


def _row_copy(src, s, dst, d, sem):
    return pltpu.make_async_copy(src.at[pl.ds(s, 1), :], dst.at[pl.ds(d, 1), :], sem)


def _scatter_body(dest_ref, h_ref, xs_in_ref, xs_ref, sem):
    del xs_in_ref
    base = pl.program_id(0) * (MOE_TT * TOP_K)

    def start(r, carry):
        for k in range(TOP_K):
            _row_copy(h_ref, r, xs_ref, dest_ref[base + r * TOP_K + k], sem).start()
        return carry

    def wait(r, carry):
        for k in range(TOP_K):
            _row_copy(h_ref, 0, xs_ref, 0, sem).wait()
        return carry

    lax.fori_loop(0, MOE_TT, start, 0)
    lax.fori_loop(0, MOE_TT, wait, 0)


def moe_scatter(h, dest, xs_init):
    m, d = h.shape
    n_rows = xs_init.shape[0]
    return pl.pallas_call(
        _scatter_body,
        out_shape=jax.ShapeDtypeStruct((n_rows, d), h.dtype),
        grid_spec=pltpu.PrefetchScalarGridSpec(
            num_scalar_prefetch=1,
            grid=(m // MOE_TT,),
            in_specs=[pl.BlockSpec((MOE_TT, d), lambda i, dest: (i, 0)),
                      pl.BlockSpec(memory_space=pl.ANY)],
            out_specs=pl.BlockSpec(memory_space=pl.ANY),
            scratch_shapes=[pltpu.SemaphoreType.DMA(())]),
        input_output_aliases={2: 0},
        compiler_params=_cparams(("arbitrary",)),
        name="moe_scatter",
    )(dest, h, xs_init)


def _moe_mm_body(blk_e_ref, nblk_ref, x_ref, w1_ref, w3_ref, w2_ref, y_ref):
    del blk_e_ref
    used = pl.program_id(0) < nblk_ref[0]

    @pl.when(used)
    def _():
        xb = x_ref[...].astype(BF16)
        hb = (jax.nn.silu(_dot(xb, w1_ref[0])) * _dot(xb, w3_ref[0])).astype(BF16)
        y_ref[...] = _dot(hb, w2_ref[0])

    @pl.when(jnp.logical_not(used))
    def _():
        y_ref[...] = jnp.zeros_like(y_ref)


def moe_experts(xs, blk_e, nblk, w1, w3, w2):
    n_rows, d = xs.shape
    ff = w1.shape[2]
    wspec = lambda r, c: pl.BlockSpec((1, r, c), lambda j, be, nb: (be[j], 0, 0))
    return pl.pallas_call(
        _moe_mm_body,
        out_shape=jax.ShapeDtypeStruct((n_rows, d), F32),
        grid_spec=pltpu.PrefetchScalarGridSpec(
            num_scalar_prefetch=2,
            grid=(n_rows // MOE_ROWS,),
            in_specs=[pl.BlockSpec((MOE_ROWS, d), lambda j, be, nb: (jnp.minimum(j, nb[0] - 1), 0)),
                      wspec(d, ff), wspec(d, ff), wspec(ff, d)],
            out_specs=pl.BlockSpec((MOE_ROWS, d), lambda j, be, nb: (j, 0))),
        compiler_params=_cparams(("arbitrary",)),
        name="moe_experts",
    )(blk_e, nblk, xs, w1, w3, w2)


def _combine_body(dest_ref, x_ref, r_ref, g_ref, y_ref, xo_ref, h_ref, buf, sem):
    i = pl.program_id(0)
    n = pl.num_programs(0)

    def start_step(step, slot):
        base = step * (MOE_TT * TOP_K)

        def body(r, carry):
            for k in range(TOP_K):
                _row_copy(y_ref, dest_ref[base + r * TOP_K + k], buf.at[slot], k * MOE_TT + r,
                          sem.at[slot]).start()
            return carry

        lax.fori_loop(0, MOE_TT, body, 0)

    @pl.when(i == 0)
    def _():
        start_step(0, 0)

    @pl.when(i + 1 < n)
    def _():
        start_step(i + 1, (i + 1) % 2)

    slot = i % 2

    def wait(r, carry):
        for k in range(TOP_K):
            _row_copy(y_ref, 0, buf.at[slot], 0, sem.at[slot]).wait()
        return carry

    lax.fori_loop(0, MOE_TT, wait, 0)
    r = r_ref[...]
    moe = r[:, 2:3] * buf[slot, 0:MOE_TT, :] + r[:, 3:4] * buf[slot, MOE_TT:2 * MOE_TT, :]
    x = x_ref[...] + moe
    xo_ref[...] = x
    hn = x * lax.rsqrt(jnp.mean(x * x, axis=-1, keepdims=True) + NORM_EPS) * g_ref[...]
    h_ref[...] = hn.astype(h_ref.dtype)


def moe_combine(x, route, y, dest, norm):
    m, d = x.shape
    tok = lambda c: pl.BlockSpec((MOE_TT, c), lambda i, dest: (i, 0))
    return pl.pallas_call(
        _combine_body,
        out_shape=(jax.ShapeDtypeStruct((m, d), F32), jax.ShapeDtypeStruct((m, d), BF16)),
        grid_spec=pltpu.PrefetchScalarGridSpec(
            num_scalar_prefetch=1,
            grid=(m // MOE_TT,),
            in_specs=[tok(d), tok(LANES), pl.BlockSpec((1, d), lambda i, dest: (0, 0)),
                      pl.BlockSpec(memory_space=pl.ANY)],
            out_specs=(tok(d), tok(d)),
            scratch_shapes=[pltpu.VMEM((2, TOP_K * MOE_TT, d), F32),
                            pltpu.SemaphoreType.DMA((2,))]),
        compiler_params=_cparams(("arbitrary",)),
        name="moe_combine",
    )(dest, x, route, norm.reshape(1, d), y)


def moe_ffn(x, norm2, rg_w, rg_b, re_w, re_b, w1, w3, w2, norm3):
    m, d = x.shape
    h, route = moe_router(x, norm2, rg_w, rg_b, re_w, re_b)
    flat_e = route[:, :TOP_K].astype(jnp.int32).reshape(m * TOP_K)
    rank, counts = moe_rank(flat_e)
    padded = (counts + MOE_ROWS - 1) // MOE_ROWS * MOE_ROWS
    pend = jnp.cumsum(padded)
    pstart = pend - padded
    dest = (pstart[flat_e] + rank).astype(jnp.int32)
    n_blocks = (m * TOP_K + N_EXPERTS * (MOE_ROWS - 1) + MOE_ROWS - 1) // MOE_ROWS
    nblk = (pend[-1:] // MOE_ROWS).astype(jnp.int32)
    starts = jnp.arange(n_blocks) * MOE_ROWS
    blk_e = jnp.minimum(jnp.sum(pend[None, :] <= starts[:, None], axis=1), N_EXPERTS - 1).astype(jnp.int32)
    xs = moe_scatter(h, dest, jnp.zeros((n_blocks * MOE_ROWS, d), h.dtype))
    y = moe_experts(xs, blk_e, nblk, cast_bf16(w1), cast_bf16(w3), cast_bf16(w2))
    return moe_combine(x, route, y, dest, norm3)


def kernel(x, p, norm1, w_in, q_norm, k_norm, cmp_pos, cmp_w1, cmp_w2, gm_norm, gm_ws, gm_bs,
           s5_lambda_re, s5_lambda_im, s5_log_dt, s5_b_re, s5_b_im, s5_c_re, s5_c_im, s5_d,
           s5_w_glu, s5_b_glu, w_branch, w_out, norm2, router_group_w, router_group_b,
           router_expert_w, router_expert_b, expert_w1, expert_w3, expert_w2, norm3, w_ple,
           w_ple_gate):
    b, l, d = x.shape
    m = b * l
    xf = x.reshape(m, d)
    q_cols = HEADS * HEAD_DIM
    kv_cols = N_BRANCH * 2 * KV_HEADS * HEAD_DIM
    ng_cols = HEADS * N_BRANCH
    splits = [0, q_cols, q_cols + kv_cols, q_cols + kv_cols + ng_cols]
    splits.append(splits[-1] + 2 * GM_WIDTH)
    splits.append(splits[-1] + S5_WIDTH)
    splits.append(splits[-1] + 3 * d)
    per_head = GROUP * N_BRANCH
    for i in range(p.shape[0]):
        h = rms_cast(xf, norm1[i])
        w = w_in[i]
        w_all = cast_drop_cols(w, splits[2], ng_cols)
        widths = [q_cols, kv_cols, 2 * GM_WIDTH, S5_WIDTH, 3 * d]
        offs = [sum(widths[:k]) for k in range(len(widths))]
        proj = lambda k, name: mm_plain(h, w_all, col_off=offs[k], n=widths[k], name=name)
        q_raw = proj(0, "proj_q")
        kv_raw = proj(1, "proj_kv")
        gm_raw = proj(2, "proj_gm")
        s5_raw = proj(3, "proj_s5")
        mg_raw = proj(4, "proj_merge")
        w_ng = jnp.pad(w[:, splits[2]:splits[3]].reshape(d, KV_HEADS, per_head),
                       ((0, 0), (0, 0), (0, LANES - per_head))).reshape(d, KV_HEADS * LANES)
        ng_raw = mm_plain(h, w_ng.astype(BF16), name="proj_gate")

        o_gm = gmlp_mixer(gm_raw, gm_norm[i], gm_ws[i], gm_bs[i])
        o_s5 = s5_mixer(s5_raw, b, s5_lambda_re[i], s5_lambda_im[i], s5_log_dt[i], s5_b_re[i],
                        s5_b_im[i], s5_c_re[i], s5_c_im[i], s5_d[i], s5_w_glu[i], s5_b_glu[i])
        o_att = nsa_mixer(q_raw, kv_raw, ng_raw, b, q_norm[i], k_norm[i], cmp_pos[i], cmp_w1[i],
                          cmp_w2[i])
        wb = cast_bf16(w_branch[i])
        merged = mm_merge([o_gm, o_s5, o_att], [wb[0], wb[1], wb[2]], mg_raw)
        xf = mm_residual(merged, cast_bf16(w_out[i]), xf)
        xf, h3 = moe_ffn(xf, norm2[i], router_group_w[i], router_group_b[i], router_expert_w[i],
                         router_expert_b[i], expert_w1[i], expert_w3[i], expert_w2[i], norm3[i])
        xf = mm_ple(h3, cast_bf16(w_ple_gate[i]), p[i].reshape(m, -1).astype(BF16),
                    w_ple[i].astype(BF16), xf)
    return xf.reshape(b, l, d)
```

```python
import functools

import jax
import jax.numpy as jnp
from jax import lax
from jax.experimental import pallas as pl
from jax.experimental.pallas import tpu as pltpu

F32 = jnp.float32
BF16 = jnp.bfloat16

NORM_EPS = 1e-6
LANES = 128
VMEM_LIMIT = 52 * 1024 * 1024

HEADS = 16
KV_HEADS = 4
GROUP = HEADS // KV_HEADS
HEAD_DIM = 128
ROPE_THETA = 500000.0
ROPE_DIMS = HEAD_DIM // 4
ROPE_HALF = ROPE_DIMS // 2
CMP_LEN = 32
CMP_STRIDE = 16
SEL_LEN = 64
SEL_TOPN = 16
WINDOW = 512
N_BRANCH = 3
NEG = -1e30
ATT_SCALE = HEAD_DIM ** -0.5 * 1.4426950408889634
ATT_TQ = 128
ATT_TK = 1024
ONES_ROWS = 16

GM_WIDTH = 2048
GM_GROUPS = 16
GM_CHUNK = 128

S5_WIDTH = 2048
S5_GROUP_DIM = 16
S5_GROUPS = S5_WIDTH // S5_GROUP_DIM
S5_STATE = 64
S5_TILE_GROUPS = LANES // S5_GROUP_DIM
S5_TILE_STATES = S5_TILE_GROUPS * S5_STATE
S5_BLOCK = 8
S5_ROWS = 512

N_GROUPS = 4
PER_GROUP = 8
N_EXPERTS = N_GROUPS * PER_GROUP
TOP_K = 2
MOE_ROWS = 256
RANK_T = 512
MOE_TT = 128


def _cparams(sem):
    return pltpu.CompilerParams(dimension_semantics=sem, vmem_limit_bytes=VMEM_LIMIT)


def _dot(a, b):
    return jnp.dot(a, b, preferred_element_type=F32)


def _dot_nt(a, b):
    return lax.dot_general(a, b, (((1,), (1,)), ((), ())), preferred_element_type=F32)


def _rms_body(x_ref, g_ref, o_ref):
    x = x_ref[...]
    y = x * lax.rsqrt(jnp.mean(x * x, axis=-1, keepdims=True) + NORM_EPS)
    o_ref[...] = (y * g_ref[...]).astype(o_ref.dtype)


def rms_cast(x, g, tm=256):
    m, d = x.shape
    return pl.pallas_call(
        _rms_body,
        out_shape=jax.ShapeDtypeStruct((m, d), BF16),
        grid=(m // tm,),
        in_specs=[pl.BlockSpec((tm, d), lambda i: (i, 0)),
                  pl.BlockSpec((1, d), lambda i: (0, 0))],
        out_specs=pl.BlockSpec((tm, d), lambda i: (i, 0)),
        compiler_params=_cparams(("parallel",)),
        name="rms_cast",
    )(x, g.reshape(1, d))


def _mm_plain_body(a_ref, w_ref, o_ref):
    o_ref[...] = _dot(a_ref[...], w_ref[...]).astype(o_ref.dtype)


def _mm_glu_body(a_ref, w_ref, y_ref, b_ref, o_ref):
    acc = _dot(a_ref[...].astype(BF16), w_ref[...]) + b_ref[...]
    o_ref[...] = (y_ref[...] * jax.nn.sigmoid(acc)).astype(o_ref.dtype)


def _mm_res_body(a_ref, w_ref, x_ref, o_ref):
    o_ref[...] = x_ref[...] + _dot(a_ref[...], w_ref[...])


def _mm_merge_body(a0, a1, a2, w0, w1, w2, g0, g1, g2, o_ref):
    acc = jax.nn.sigmoid(g0[...]) * _dot(a0[...], w0[...])
    acc = acc + jax.nn.sigmoid(g1[...]) * _dot(a1[...], w1[...])
    acc = acc + jax.nn.sigmoid(g2[...]) * _dot(a2[...], w2[...])
    o_ref[...] = acc.astype(o_ref.dtype)


def _mm_ple_body(a_ref, w_ref, p_ref, wp_ref, x_ref, o_ref):
    gate = jax.nn.sigmoid(_dot(a_ref[...], w_ref[...]))
    o_ref[...] = x_ref[...] + _dot(p_ref[...], wp_ref[...]) * gate


def _lhs_spec(tm, k):
    return pl.BlockSpec((tm, k), lambda j, i: (i, 0))


def _rhs_spec(k, tn):
    return pl.BlockSpec((k, tn), lambda j, i: (0, j))


def _tile_spec(tm, tn, col_blocks=0):
    return pl.BlockSpec((tm, tn), lambda j, i: (i, j + col_blocks))


def _mm_call(body, args, in_specs, m, n, tm, tn, out_dtype, name):
    return pl.pallas_call(
        body,
        out_shape=jax.ShapeDtypeStruct((m, n), out_dtype),
        grid=(n // tn, m // tm),
        in_specs=in_specs,
        out_specs=pl.BlockSpec((tm, tn), lambda j, i: (i, j)),
        compiler_params=_cparams(("parallel", "parallel")),
        name=name,
    )(*args)


def _pick(n, prefs):
    for t in prefs:
        if n % t == 0:
            return t
    return n


def mm_plain(a, w, col_off=0, n=None, out_dtype=F32, name="mm"):
    m, k = a.shape
    n = w.shape[1] if n is None else n
    tm, tn = _pick(m, (512, 256, 128)), _pick(n, (1024, 512, 256, 128))
    assert col_off % tn == 0
    rhs = pl.BlockSpec((k, tn), lambda j, i: (0, j + col_off // tn))
    return _mm_call(_mm_plain_body, (a, w), [_lhs_spec(tm, k), rhs], m, n, tm, tn, out_dtype, name)


def _drop_cols_body(x_ref, o_ref, *, cut, skip):
    x = x_ref[...]
    o_ref[:, :cut] = x[:, :cut].astype(o_ref.dtype)
    o_ref[:, cut:] = x[:, cut + skip:].astype(o_ref.dtype)


def cast_drop_cols(w, layer, cut, skip, tr=64):
    _, rows, cols = w.shape
    return pl.pallas_call(
        functools.partial(_drop_cols_body, cut=cut, skip=skip),
        out_shape=jax.ShapeDtypeStruct((rows, cols - skip), BF16),
        grid=(rows // tr,),
        in_specs=[pl.BlockSpec((None, tr, cols), lambda i: (layer, i, 0))],
        out_specs=pl.BlockSpec((tr, cols - skip), lambda i: (i, 0)),
        compiler_params=_cparams(("parallel",)),
        name="cast_drop_cols",
    )(w)


def _cast_body(x_ref, o_ref):
    o_ref[...] = x_ref[...].astype(o_ref.dtype)


def cast_bf16(w, layer, block_bytes=8 * 1024 * 1024):
    cols = w.shape[-1]
    x = w.reshape(w.shape[0], -1, cols)
    rows = x.shape[1]
    tr = next(t for t in (4096, 2048, 1024, 512, 256, 128, 64, 32, 16)
              if rows % t == 0 and t * cols * 4 <= block_bytes)
    out = pl.pallas_call(
        _cast_body,
        out_shape=jax.ShapeDtypeStruct((rows, cols), BF16),
        grid=(rows // tr,),
        in_specs=[pl.BlockSpec((None, tr, cols), lambda i: (layer, i, 0))],
        out_specs=pl.BlockSpec((tr, cols), lambda i: (i, 0)),
        compiler_params=_cparams(("parallel",)),
        name="cast_bf16",
    )(x)
    return out.reshape(w.shape[1:])


def mm_glu(y, w, b):
    m, k = y.shape
    n = w.shape[1]
    tm, tn = _pick(m, (512, 256, 128)), _pick(n, (1024, 512, 256, 128))
    specs = [_lhs_spec(tm, k), _rhs_spec(k, tn), _tile_spec(tm, tn),
             pl.BlockSpec((1, tn), lambda j, i: (0, j))]
    return _mm_call(_mm_glu_body, (y, w, y, b.reshape(1, n)), specs, m, n, tm, tn, BF16, "mm_glu")


def mm_residual(a, w, x):
    m, k = a.shape
    n = w.shape[1]
    tm, tn = _pick(m, (512, 256, 128)), _pick(n, (1024, 512, 256, 128))
    specs = [_lhs_spec(tm, k), _rhs_spec(k, tn), _tile_spec(tm, tn)]
    return _mm_call(_mm_res_body, (a, w, x), specs, m, n, tm, tn, F32, "mm_residual")


def mm_merge(o_list, w_list, mg_raw):
    m, k = o_list[0].shape
    n = w_list[0].shape[1]
    tm, tn = _pick(m, (512, 256, 128)), _pick(n, (512, 256, 128))
    specs = ([_lhs_spec(tm, k)] * 3 + [_rhs_spec(k, tn)] * 3
             + [_tile_spec(tm, tn, c * (n // tn)) for c in range(3)])
    return _mm_call(_mm_merge_body, (*o_list, *w_list, mg_raw, mg_raw, mg_raw), specs,
                    m, n, tm, tn, BF16, "mm_merge")


def mm_ple(h, wg, p, wp, x):
    m, k = h.shape
    n = wg.shape[1]
    kp = p.shape[1]
    tm, tn = _pick(m, (512, 256, 128)), _pick(n, (1024, 512, 256, 128))
    specs = [_lhs_spec(tm, k), _rhs_spec(k, tn), _lhs_spec(tm, kp), _rhs_spec(kp, tn),
             _tile_spec(tm, tn)]
    return _mm_call(_mm_ple_body, (h, wg, p, wp, x), specs, m, n, tm, tn, F32, "mm_ple")


def _gmlp_body(z_ref, gn_ref, w_ref, b_ref, o_ref, *, rows):
    z = jax.nn.gelu(z_ref[...])
    u = z[:, :GM_WIDTH]
    v = z[:, GM_WIDTH:]
    v = v * lax.rsqrt(jnp.mean(v * v, axis=-1, keepdims=True) + NORM_EPS) * gn_ref[...]
    vb = v.astype(BF16)
    for c in range(rows // GM_CHUNK):
        r0 = c * GM_CHUNK
        for g in range(GM_GROUPS):
            c0 = g * LANES
            s = _dot(w_ref[g], vb[r0:r0 + GM_CHUNK, c0:c0 + LANES]) + b_ref[:, c0:c0 + LANES]
            o_ref[r0:r0 + GM_CHUNK, c0:c0 + LANES] = (
                u[r0:r0 + GM_CHUNK, c0:c0 + LANES] * s).astype(o_ref.dtype)


def gmlp_mixer(z, gm_norm, ws, bs, rows=256):
    m = z.shape[0]
    tri = jnp.tril(jnp.ones((GM_CHUNK, GM_CHUNK), dtype=bool))
    w = jnp.where(tri[None], ws, 0.0).astype(BF16)
    bias = jnp.repeat(bs.T, LANES, axis=1)
    return pl.pallas_call(
        functools.partial(_gmlp_body, rows=rows),
        out_shape=jax.ShapeDtypeStruct((m, GM_WIDTH), BF16),
        grid=(m // rows,),
        in_specs=[pl.BlockSpec((rows, 2 * GM_WIDTH), lambda i: (i, 0)),
                  pl.BlockSpec((1, GM_WIDTH), lambda i: (0, 0)),
                  pl.BlockSpec((GM_GROUPS, GM_CHUNK, GM_CHUNK), lambda i: (0, 0, 0)),
                  pl.BlockSpec((GM_CHUNK, GM_WIDTH), lambda i: (0, 0))],
        out_specs=pl.BlockSpec((rows, GM_WIDTH), lambda i: (i, 0)),
        compiler_params=_cparams(("parallel",)),
        name="gmlp",
    )(z, gm_norm.reshape(1, GM_WIDTH), w, bias)


def _shift_rows(x, s):
    t = x.shape[0]
    if s % 8 == 0:
        return jnp.concatenate([jnp.zeros((s, x.shape[1]), x.dtype), x[:t - s]], axis=0)
    rolled = pltpu.roll(x, s, 0)
    row = lax.broadcasted_iota(jnp.int32, x.shape, 0)
    return jnp.where(row >= s, rolled, 0.0)


def _group_diag(table, rows_per_group, lanes_per_group):
    full = jnp.concatenate([table] * S5_TILE_GROUPS, axis=0)
    rg = lax.shift_right_logical(lax.broadcasted_iota(jnp.int32, full.shape, 0),
                                 rows_per_group.bit_length() - 1)
    lg = lax.shift_right_logical(lax.broadcasted_iota(jnp.int32, full.shape, 1),
                                 lanes_per_group.bit_length() - 1)
    return jnp.where(rg == lg, full, 0.0).astype(BF16)


def _s5_body(u_ref, kt_ref, qt_ref, rt_ref, are_ref, aim_ref, d_ref, o_ref,
             w_ref, q_ref, r_ref, hre_ref, him_ref, *, rows, levels):
    tb, hg, ns = S5_BLOCK, S5_GROUP_DIM, S5_STATE

    @pl.when(pl.program_id(2) == 0)
    def _():
        hre_ref[...] = jnp.zeros_like(hre_ref)
        him_ref[...] = jnp.zeros_like(him_ref)
        lag_blocks = [_group_diag(kt_ref[0, d], hg, hg) for d in range(tb)]
        zero = jnp.zeros((LANES, LANES), BF16)
        for ip in range(tb):
            for i in range(tb):
                w_ref[ip * LANES:(ip + 1) * LANES, i * LANES:(i + 1) * LANES] = (
                    lag_blocks[i - ip] if i >= ip else zero)
        for ip in range(tb):
            for ri in range(2):
                q_ref[ip * LANES:(ip + 1) * LANES, ri * S5_TILE_STATES:(ri + 1) * S5_TILE_STATES] = (
                    _group_diag(qt_ref[0, ip, ri], hg, ns))
        for ri in range(2):
            for i in range(tb):
                r_ref[ri * S5_TILE_STATES:(ri + 1) * S5_TILE_STATES, i * LANES:(i + 1) * LANES] = (
                    _group_diag(rt_ref[0, ri, i], ns, hg))

    us = [u_ref[pl.ds(i, rows, stride=S5_BLOCK), :] for i in range(S5_BLOCK)]
    ucat = jnp.concatenate([x.astype(BF16) for x in us], axis=1)
    s = _dot(ucat, q_ref[...])
    xr = s[:, :S5_TILE_STATES]
    xi = s[:, S5_TILE_STATES:]
    are = are_ref[0]
    aim = aim_ref[0]
    hr = hre_ref[0:1, :]
    hi = him_ref[0:1, :]
    a_r = are[0:1, :]
    a_i = aim[0:1, :]
    first = lax.broadcasted_iota(jnp.int32, xr.shape, 0) == 0
    xr = xr + jnp.where(first, a_r * hr - a_i * hi, 0.0)
    xi = xi + jnp.where(first, a_r * hi + a_i * hr, 0.0)
    for k in range(levels):
        sh = 1 << k
        cr = are[k:k + 1, :]
        ci = aim[k:k + 1, :]
        sr = _shift_rows(xr, sh)
        si = _shift_rows(xi, sh)
        xr, xi = xr + cr * sr - ci * si, xi + cr * si + ci * sr
    hre_ref[0:1, :] = xr[rows - 1:rows, :]
    him_ref[0:1, :] = xi[rows - 1:rows, :]
    xpr = _shift_rows(xr, 1) + jnp.where(first, hr, 0.0)
    xpi = _shift_rows(xi, 1) + jnp.where(first, hi, 0.0)
    xprev = jnp.concatenate([xpr, xpi], axis=1).astype(BF16)
    y = _dot(ucat, w_ref[...]) + _dot(xprev, r_ref[...])
    d = d_ref[...]
    for i in range(S5_BLOCK):
        yi = y[:, i * LANES:(i + 1) * LANES] + d * us[i]
        o_ref[pl.ds(i, rows, stride=S5_BLOCK), :] = jax.nn.gelu(yi)


def _s5_discretize(lam_re, lam_im, log_dt, b_re, b_im):
    dt = jnp.exp(log_dt.astype(F32))[:, None]
    lre = jnp.minimum(lam_re.astype(F32), -1e-4)
    lim = lam_im.astype(F32)
    mag = jnp.exp(lre * dt)
    a_re = mag * jnp.cos(lim * dt)
    a_im = mag * jnp.sin(lim * dt)
    den = lre * lre + lim * lim
    nr = a_re - 1.0
    f_re = (nr * lre + a_im * lim) / den
    f_im = (a_im * lre - nr * lim) / den
    br = b_re.astype(F32)
    bi = b_im.astype(F32)
    bb_re = f_re[..., None] * br - f_im[..., None] * bi
    bb_im = f_re[..., None] * bi + f_im[..., None] * br
    return a_re, a_im, bb_re, bb_im


def _s5_block_weights(a_re, a_im, bb_re, bb_im, c_re, c_im, levels):
    hp = lax.Precision.HIGHEST
    ng, ns = a_re.shape
    tb, gl, hg = S5_BLOCK, S5_TILE_GROUPS, S5_GROUP_DIM
    nt = ng // gl
    pr, pi = [jnp.ones_like(a_re)], [jnp.zeros_like(a_re)]
    for _ in range(tb):
        r, i = pr[-1], pi[-1]
        pr.append(r * a_re - i * a_im)
        pi.append(r * a_im + i * a_re)
    pw_re, pw_im = jnp.stack(pr), jnp.stack(pi)
    ab_re = pw_re[:tb, :, :, None] * bb_re[None] - pw_im[:tb, :, :, None] * bb_im[None]
    ab_im = pw_re[:tb, :, :, None] * bb_im[None] + pw_im[:tb, :, :, None] * bb_re[None]
    cr, ci = c_re.astype(F32), c_im.astype(F32)
    kern = (jnp.einsum('gcp,dgpk->dgck', cr, ab_re, precision=hp)
            - jnp.einsum('gcp,dgpk->dgck', ci, ab_im, precision=hp))
    kt = kern.reshape(tb, nt, gl, hg, hg).transpose(1, 0, 4, 2, 3).reshape(nt, tb, hg, gl * hg)
    q = jnp.stack([ab_re[::-1], ab_im[::-1]], axis=1).reshape(tb, 2, nt, gl, ns, hg)
    qt = q.transpose(2, 0, 1, 5, 3, 4).reshape(nt, tb, 2, hg, gl * ns)
    nr, ni = pw_re[1:, :, None, :], pw_im[1:, :, None, :]
    r = jnp.stack([cr[None] * nr - ci[None] * ni, -(cr[None] * ni + ci[None] * nr)])
    r = r.reshape(2, tb, nt, gl, hg, ns)
    rt = r.transpose(2, 0, 1, 5, 3, 4).reshape(nt, 2, tb, ns, gl * hg)
    sr, si = [pw_re[tb].reshape(-1)], [pw_im[tb].reshape(-1)]
    for _ in range(15):
        x, y = sr[-1], si[-1]
        sr.append(x * x - y * y)
        si.append(2.0 * x * y)
    assert levels <= 16
    scan_re = jnp.stack(sr).reshape(16, nt, gl * ns).transpose(1, 0, 2)
    scan_im = jnp.stack(si).reshape(16, nt, gl * ns).transpose(1, 0, 2)
    return kt, qt, rt, scan_re, scan_im


def s5_mixer(u, batch, lam_re, lam_im, log_dt, b_re, b_im, c_re, c_im, d_skip, w_glu, b_glu):
    m = u.shape[0]
    seq = m // batch
    nt = S5_GROUPS // S5_TILE_GROUPS
    nblk = seq // S5_BLOCK
    rows = min(S5_ROWS, nblk)
    levels = rows.bit_length() - 1
    assert rows == 1 << levels and nblk % rows == 0
    a_re, a_im, bb_re, bb_im = _s5_discretize(lam_re, lam_im, log_dt, b_re, b_im)
    kt, qt, rt, scan_re, scan_im = _s5_block_weights(a_re, a_im, bb_re, bb_im, c_re, c_im, levels)

    nchunk = nblk // rows
    tok = rows * S5_BLOCK
    tspec = lambda shape: pl.BlockSpec((1,) + shape, lambda b, j, t: (j,) + (0,) * len(shape))
    kin = S5_BLOCK * LANES
    y = pl.pallas_call(
        functools.partial(_s5_body, rows=rows, levels=levels),
        out_shape=jax.ShapeDtypeStruct((m, S5_WIDTH), F32),
        grid=(batch, nt, nchunk),
        in_specs=[pl.BlockSpec((tok, LANES), lambda b, j, t: (b * nchunk + t, j)),
                  tspec(kt.shape[1:]), tspec(qt.shape[1:]), tspec(rt.shape[1:]),
                  tspec((16, S5_TILE_STATES)), tspec((16, S5_TILE_STATES)),
                  pl.BlockSpec((1, LANES), lambda b, j, t: (0, j))],
        out_specs=pl.BlockSpec((tok, LANES), lambda b, j, t: (b * nchunk + t, j)),
        scratch_shapes=[pltpu.VMEM((kin, kin), BF16), pltpu.VMEM((kin, 2 * S5_TILE_STATES), BF16),
                        pltpu.VMEM((2 * S5_TILE_STATES, kin), BF16),
                        pltpu.VMEM((8, S5_TILE_STATES), F32), pltpu.VMEM((8, S5_TILE_STATES), F32)],
        compiler_params=_cparams(("parallel", "parallel", "arbitrary")),
        name="s5_scan",
    )(u, kt, qt, rt, scan_re, scan_im, d_skip.reshape(1, S5_WIDTH))
    return mm_glu(y, w_glu.astype(BF16), b_glu)


def _rope_tables(pos):
    inv_freq = ROPE_THETA ** (-jnp.arange(ROPE_HALF, dtype=F32) / ROPE_HALF)
    ang = pos.astype(F32)[:, None] * inv_freq[None, :]
    cos, sin = jnp.cos(ang), jnp.sin(ang)
    n = pos.shape[0]
    rest = HEAD_DIM - ROPE_DIMS
    c = jnp.concatenate([cos, cos, jnp.ones((n, rest), F32)], axis=1)
    s1 = jnp.concatenate([-sin, jnp.zeros((n, HEAD_DIM - ROPE_HALF), F32)], axis=1)
    s2 = jnp.concatenate([jnp.zeros((n, ROPE_HALF), F32), sin, jnp.zeros((n, rest), F32)], axis=1)
    return c, s1, s2


def _norm_rope(x, gain, c, s1, s2):
    x = x * lax.rsqrt(jnp.mean(x * x, axis=-1, keepdims=True) + NORM_EPS) * gain
    return (x * c + pltpu.roll(x, HEAD_DIM - ROPE_HALF, 1) * s1 + pltpu.roll(x, ROPE_HALF, 1) * s2)


def _kv_prep_body(ks_ref, vs_ref, kw_ref, vw_ref, kn_ref, c_ref, s1_ref, s2_ref,
                  oks_ref, ovs_ref, okw_ref, ovw_ref):
    c, s1, s2 = c_ref[...], s1_ref[...], s2_ref[...]
    oks_ref[...] = _norm_rope(ks_ref[...], kn_ref[1:2, :], c, s1, s2).astype(BF16)
    okw_ref[...] = _norm_rope(kw_ref[...], kn_ref[2:3, :], c, s1, s2).astype(BF16)
    vt = vs_ref[...].T
    ones = jnp.ones((ONES_ROWS, vt.shape[1]), F32)
    ovs_ref[...] = jnp.concatenate([vt, ones], axis=0).astype(BF16)
    ovw_ref[...] = jnp.concatenate([vw_ref[...].T, ones], axis=0).astype(BF16)


def _cmp_prep_body(k_ref, v_ref, pos_ref, w1_ref, w2_ref, kn_ref, c_ref, s1_ref, s2_ref,
                   ok_ref, ov_ref, *, ncp):
    half = CMP_LEN // 2
    outs = []
    for which, t_ref in enumerate((k_ref, v_ref)):
        lo = jnp.zeros((ncp, HEAD_DIM), F32)
        hi = jnp.zeros((ncp, HEAD_DIM), F32)
        for j in range(half):
            tj = t_ref[pl.ds(j, ncp, stride=CMP_STRIDE), :]
            a = (tj + pos_ref[which, j:j + 1, :]).astype(BF16)
            b = (tj + pos_ref[which, half + j:half + j + 1, :]).astype(BF16)
            lo = lo + _dot(a, w1_ref[which, j])
            hi = hi + _dot(b, w1_ref[which, half + j])
        pre = lo + pltpu.roll(hi, ncp - 1, 0)
        outs.append(_dot(jax.nn.gelu(pre).astype(BF16), w2_ref[which]))
    valid = lax.broadcasted_iota(jnp.int32, (ncp, HEAD_DIM), 0) < ncp - 1
    kc = _norm_rope(outs[0], kn_ref[0:1, :], c_ref[...], s1_ref[...], s2_ref[...])
    ok_ref[...] = jnp.where(valid, kc, 0.0).astype(BF16)
    ov_ref[...] = jnp.where(valid, outs[1], 0.0).T.astype(BF16)


def _attn_body(q_ref, gate_ref, c_ref, s1_ref, s2_ref, qn_ref, kc_ref, vct_ref,
               ks_ref, vst_ref, kw_ref, vwt_ref, ovt_ref, o_ref, bias_ref, s_ref, *, tq, ncp):
    s0 = pl.program_id(2) * tq
    cols = GROUP * tq
    qpos = s0 + (lax.broadcasted_iota(jnp.int32, (1, cols), 1) & (tq - 1))
    qpos1 = qpos[:, :tq]
    c, s1, s2 = c_ref[...], s1_ref[...], s2_ref[...]
    qn = qn_ref[...]
    q_t = jnp.concatenate(
        [(_norm_rope(q_ref[:, g * HEAD_DIM:(g + 1) * HEAD_DIM], qn, c, s1, s2) * ATT_SCALE).T.astype(BF16)
         for g in range(GROUP)], axis=1)

    n_id = lax.broadcasted_iota(jnp.int32, (ncp, 1), 0)
    cbias = jnp.where(n_id * CMP_STRIDE + (CMP_LEN - 1) <= qpos1, 0.0, NEG)
    sc = _dot(kc_ref[...], q_t) + jnp.concatenate([cbias] * GROUP, axis=1)
    mx = jnp.max(sc, axis=0, keepdims=True)
    e = jnp.exp2(sc - mx)
    inv = jnp.where(qpos >= CMP_LEN - 1, 1.0 / jnp.sum(e, axis=0, keepdims=True), 0.0)
    p = e * inv
    o_cmp = _dot(vct_ref[...], p.astype(BF16))

    span = WINDOW + tq
    w0 = pl.multiple_of(jnp.maximum(s0 - WINDOW, 0), tq)
    kp = w0 + lax.broadcasted_iota(jnp.int32, (span, 1), 0)
    wbias = jnp.where((kp <= qpos1) & (kp > qpos1 - WINDOW), 0.0, NEG)
    sw = _dot(kw_ref[pl.ds(w0, span), :], q_t) + jnp.concatenate([wbias] * GROUP, axis=1)
    ew = jnp.exp2(sw - jnp.max(sw, axis=0, keepdims=True))
    acc_win = _dot(vwt_ref[:, pl.ds(w0, span)], ew.astype(BF16))
    o_win = acc_win[:HEAD_DIM] / acc_win[HEAD_DIM:HEAD_DIM + 1]

    pg = p[:, 0:tq]
    for g in range(1, GROUP):
        pg = pg + p[:, g * tq:(g + 1) * tq]
    p_hi = pg.astype(BF16)
    p_lo = (pg - p_hi.astype(F32)).astype(BF16)
    imp = _dot(ovt_ref[...], p_hi) + _dot(ovt_ref[...], p_lo)
    blk = lax.broadcasted_iota(jnp.int32, (LANES, tq), 0)
    cur = lax.shift_right_logical(qpos1, 6)
    forced = (blk == 0) | (blk == cur) | (blk == cur - 1)
    val = jnp.where(forced, -jnp.inf, jnp.where(blk <= cur, imp, -1e9))
    blk_f = blk.astype(F32)
    sel = jnp.where(forced, 1.0, 0.0)
    for _ in range(SEL_TOPN - 3):
        top = jnp.max(val, axis=0, keepdims=True)
        idx = jnp.min(jnp.where(val == top, blk_f, float(LANES)), axis=0, keepdims=True)
        hit = blk_f == idx
        val = jnp.where(hit, -jnp.inf, val)
        sel = jnp.where(hit, 1.0, sel)
    bias_ref[...] = (sel - 1.0) * (-NEG)

    blocks_per_tile = ATT_TK // SEL_LEN

    def score_tile(kt, m8, diagonal):
        k0 = pl.multiple_of(kt * ATT_TK, ATT_TK)
        s = _dot(ks_ref[pl.ds(k0, ATT_TK), :], q_t)
        b8 = bias_ref[pl.ds(pl.multiple_of(kt * blocks_per_tile, blocks_per_tile), blocks_per_tile), :]
        bias = jnp.concatenate([jnp.broadcast_to(b8[j:j + 1, :], (SEL_LEN, tq))
                                for j in range(blocks_per_tile)], axis=0)
        if diagonal:
            kpos = k0 + lax.broadcasted_iota(jnp.int32, (ATT_TK, 1), 0)
            bias = jnp.where(kpos <= qpos1, bias, NEG)
        s = s + jnp.concatenate([bias] * GROUP, axis=1)
        s_ref[pl.ds(k0, ATT_TK), :] = s
        return jnp.maximum(m8, jnp.max(s.reshape(ATT_TK // 8, 8, cols), axis=0))

    n_full = lax.div(s0, ATT_TK)
    m8 = lax.fori_loop(0, n_full, lambda kt, mm: score_tile(kt, mm, False), jnp.full((8, cols), NEG, F32))
    m_sel = jnp.max(score_tile(n_full, m8, True), axis=0, keepdims=True)

    def value_tile(kt, acc):
        k0 = pl.multiple_of(kt * ATT_TK, ATT_TK)
        pe = jnp.exp2(s_ref[pl.ds(k0, ATT_TK), :] - m_sel).astype(BF16)
        return acc + _dot(vst_ref[:, pl.ds(k0, ATT_TK)], pe)

    acc_sel = lax.fori_loop(0, n_full + 1, value_tile, jnp.zeros((HEAD_DIM + ONES_ROWS, cols), F32))
    o_sel = acc_sel[:HEAD_DIM] / acc_sel[HEAD_DIM:HEAD_DIM + 1]

    gates = jax.nn.sigmoid(gate_ref[...]).T
    for g in range(GROUP):
        cs = slice(g * tq, (g + 1) * tq)
        r = g * N_BRANCH
        out = (gates[r:r + 1, :] * o_cmp[:, cs] + gates[r + 1:r + 2, :] * o_sel[:, cs]
               + gates[r + 2:r + 3, :] * o_win[:, cs])
        o_ref[:, g * HEAD_DIM:(g + 1) * HEAD_DIM] = out.T.astype(o_ref.dtype)


def nsa_mixer(q_raw, kv_raw, gate_raw, batch, q_norm, k_norm, cmp_pos, cmp_w1, cmp_w2):
    m = q_raw.shape[0]
    seq = m // batch
    ncp = seq // CMP_STRIDE
    nsel = seq // SEL_LEN
    assert seq % ATT_TK == 0 and nsel <= LANES and seq >= WINDOW + ATT_TQ
    pos = jnp.arange(seq)
    c, s1, s2 = _rope_tables(pos)
    cc, cs1, cs2 = _rope_tables(jnp.arange(ncp) * CMP_STRIDE + CMP_LEN - 1)
    col = lambda branch, kv: (branch * 2 + kv) * KV_HEADS

    tl = _pick(seq, (1024, 512))
    nl = seq // tl
    kvspec = lambda base: pl.BlockSpec((tl, HEAD_DIM), lambda b, h, t: (b * nl + t, base + h))
    tab = pl.BlockSpec((tl, HEAD_DIM), lambda b, h, t: (t, 0))
    outspec = pl.BlockSpec((None, None, tl, HEAD_DIM), lambda b, h, t: (b, h, t, 0))
    outspec_t = pl.BlockSpec((None, None, HEAD_DIM, tl), lambda b, h, t: (b, h, 0, t))
    kv_shape = jax.ShapeDtypeStruct((batch, KV_HEADS, seq, HEAD_DIM), BF16)
    kv_shape_t = jax.ShapeDtypeStruct((batch, KV_HEADS, HEAD_DIM, seq), BF16)
    vrows = HEAD_DIM + ONES_ROWS
    outspec_t1 = pl.BlockSpec((None, None, vrows, tl), lambda b, h, t: (b, h, 0, t))
    kv_shape_t1 = jax.ShapeDtypeStruct((batch, KV_HEADS, vrows, seq), BF16)
    k_s, v_s, k_w, v_w = pl.pallas_call(
        _kv_prep_body,
        out_shape=(kv_shape, kv_shape_t1, kv_shape, kv_shape_t1),
        grid=(batch, KV_HEADS, nl),
        in_specs=[kvspec(col(1, 0)), kvspec(col(1, 1)), kvspec(col(2, 0)), kvspec(col(2, 1)),
                  pl.BlockSpec((N_BRANCH, HEAD_DIM), lambda b, h, t: (0, 0)), tab, tab, tab],
        out_specs=(outspec, outspec_t1, outspec, outspec_t1),
        compiler_params=_cparams(("parallel", "parallel", "parallel")),
        name="nsa_kv_prep",
    )(kv_raw, kv_raw, kv_raw, kv_raw, k_norm, c, s1, s2)

    w1 = cmp_w1.reshape(2, CMP_LEN, HEAD_DIM, HEAD_DIM).astype(BF16)
    full = lambda shape: pl.BlockSpec(shape, lambda b, h: (0,) * len(shape))
    cshape = jax.ShapeDtypeStruct((batch, KV_HEADS, ncp, HEAD_DIM), BF16)
    cspec = pl.BlockSpec((None, None, ncp, HEAD_DIM), lambda b, h: (b, h, 0, 0))
    cshape_t = jax.ShapeDtypeStruct((batch, KV_HEADS, HEAD_DIM, ncp), BF16)
    cspec_t = pl.BlockSpec((None, None, HEAD_DIM, ncp), lambda b, h: (b, h, 0, 0))
    k_c, v_c = pl.pallas_call(
        functools.partial(_cmp_prep_body, ncp=ncp),
        out_shape=(cshape, cshape_t),
        grid=(batch, KV_HEADS),
        in_specs=[pl.BlockSpec((seq, HEAD_DIM), lambda b, h: (b, col(0, 0) + h)),
                  pl.BlockSpec((seq, HEAD_DIM), lambda b, h: (b, col(0, 1) + h)),
                  full((2, CMP_LEN, HEAD_DIM)), full((2, CMP_LEN, HEAD_DIM, HEAD_DIM)),
                  full((2, HEAD_DIM, HEAD_DIM)), full((N_BRANCH, HEAD_DIM)),
                  full((ncp, HEAD_DIM)), full((ncp, HEAD_DIM)), full((ncp, HEAD_DIM))],
        out_specs=(cspec, cspec_t),
        compiler_params=_cparams(("parallel", "parallel")),
        name="nsa_cmp_prep",
    )(kv_raw, kv_raw, cmp_pos, w1, cmp_w2.astype(BF16), k_norm, cc, cs1, cs2)

    cstart = jnp.arange(ncp)[:, None] * CMP_STRIDE
    sstart = jnp.arange(LANES)[None, :] * SEL_LEN
    overlap = jnp.maximum(jnp.minimum(cstart + CMP_LEN, sstart + SEL_LEN) - jnp.maximum(cstart, sstart), 0)
    overlap = jnp.where(jnp.arange(LANES)[None, :] < nsel, overlap, 0)
    overlap_t = (overlap.astype(F32) / CMP_STRIDE).astype(BF16).T

    tq = ATT_TQ
    nq = seq // tq
    qtab = pl.BlockSpec((tq, HEAD_DIM), lambda b, h, i: (i, 0))
    kvfull = lambda n: pl.BlockSpec((None, None, n, HEAD_DIM), lambda b, h, i: (b, h, 0, 0))
    kvfull_t = lambda n: pl.BlockSpec((None, None, HEAD_DIM, n), lambda b, h, i: (b, h, 0, 0))
    vfull_t = pl.BlockSpec((None, None, vrows, seq), lambda b, h, i: (b, h, 0, 0))
    return pl.pallas_call(
        functools.partial(_attn_body, tq=tq, ncp=ncp),
        out_shape=jax.ShapeDtypeStruct((m, HEADS * HEAD_DIM), BF16),
        grid=(batch, KV_HEADS, nq),
        in_specs=[pl.BlockSpec((tq, GROUP * HEAD_DIM), lambda b, h, i: (b * nq + i, h)),
                  pl.BlockSpec((tq, LANES), lambda b, h, i: (b * nq + i, h)),
                  qtab, qtab, qtab,
                  pl.BlockSpec((1, HEAD_DIM), lambda b, h, i: (0, 0)),
                  kvfull(ncp), kvfull_t(ncp), kvfull(seq), vfull_t, kvfull(seq), vfull_t,
                  pl.BlockSpec((LANES, ncp), lambda b, h, i: (0, 0))],
        out_specs=pl.BlockSpec((tq, GROUP * HEAD_DIM), lambda b, h, i: (b * nq + i, h)),
        scratch_shapes=[pltpu.VMEM((LANES, tq), F32), pltpu.VMEM((seq, GROUP * tq), F32)],
        compiler_params=_cparams(("parallel", "parallel", "arbitrary")),
        name="nsa_attn",
    )(q_raw, gate_raw, c, s1, s2, q_norm.reshape(1, HEAD_DIM), k_c, v_c, k_s, v_s, k_w, v_w, overlap_t)


def _router_body(x_ref, g_ref, whi_ref, wlo_ref, b_ref, h_ref, r_ref):
    x = x_ref[...]
    h = x * lax.rsqrt(jnp.mean(x * x, axis=-1, keepdims=True) + NORM_EPS) * g_ref[...]
    h_ref[...] = h
    h_hi = h.astype(BF16)
    h_lo = (h - h_hi.astype(F32)).astype(BF16)
    logits = (_dot(h_hi, whi_ref[...]) + _dot(h_lo, whi_ref[...]) + _dot(h_hi, wlo_ref[...])
              + b_ref[...])
    lane = lax.broadcasted_iota(jnp.int32, logits.shape, 1)
    lane_f = lane.astype(F32)
    none = float(LANES)
    is_g = lane < N_GROUPS
    glog = jnp.where(is_g, logits, -jnp.inf)
    gmax = jnp.max(glog, axis=-1, keepdims=True)
    gidx = jnp.min(jnp.where(glog == gmax, lane_f, none), axis=-1, keepdims=True)
    g_p = 1.0 / jnp.sum(jnp.where(is_g, jnp.exp(logits - gmax), 0.0), axis=-1, keepdims=True)
    e_lo = N_GROUPS + gidx * PER_GROUP
    elog = jnp.where((lane_f >= e_lo) & (lane_f < e_lo + PER_GROUP), logits, -jnp.inf)
    m1 = jnp.max(elog, axis=-1, keepdims=True)
    i1 = jnp.min(jnp.where(elog == m1, lane_f, none), axis=-1, keepdims=True)
    elog = jnp.where(lane_f == i1, -jnp.inf, elog)
    m2 = jnp.max(elog, axis=-1, keepdims=True)
    i2 = jnp.min(jnp.where(elog == m2, lane_f, none), axis=-1, keepdims=True)
    t = jnp.exp(m2 - m1)
    w1 = g_p / (1.0 + t)
    w2 = g_p * t / (1.0 + t)
    r_ref[...] = jnp.where(lane == 0, i1 - N_GROUPS,
                           jnp.where(lane == 1, i2 - N_GROUPS,
                                     jnp.where(lane == 2, w1, jnp.where(lane == 3, w2, 0.0))))


def moe_router(x, norm, rg_w, rg_b, re_w, re_b, tm=256):
    m, d = x.shape
    pad = LANES - N_GROUPS - N_EXPERTS
    wr = jnp.pad(jnp.concatenate([rg_w, re_w], axis=1).astype(F32), ((0, 0), (0, pad)))
    whi = wr.astype(BF16)
    wlo = (wr - whi.astype(F32)).astype(BF16)
    bias = jnp.pad(jnp.concatenate([rg_b, re_b]).astype(F32), (0, pad)).reshape(1, LANES)
    const = lambda shape: pl.BlockSpec(shape, lambda i: (0, 0))
    return pl.pallas_call(
        _router_body,
        out_shape=(jax.ShapeDtypeStruct((m, d), F32), jax.ShapeDtypeStruct((m, LANES), F32)),
        grid=(m // tm,),
        in_specs=[pl.BlockSpec((tm, d), lambda i: (i, 0)), const((1, d)),
                  const((d, LANES)), const((d, LANES)), const((1, LANES))],
        out_specs=(pl.BlockSpec((tm, d), lambda i: (i, 0)), pl.BlockSpec((tm, LANES), lambda i: (i, 0))),
        compiler_params=_cparams(("parallel",)),
        name="moe_router",
    )(x, norm.reshape(1, d), whi, wlo, bias)


def _rank_body(e_ref, rank_ref, cnt_ref, carry_ref):
    @pl.when(pl.program_id(0) == 0)
    def _():
        carry_ref[...] = jnp.zeros_like(carry_ref)

    e = e_ref[0]
    sub = lax.broadcasted_iota(jnp.int32, (N_EXPERTS, RANK_T), 0)
    onehot = jnp.where(sub == e, 1.0, 0.0)
    r = lax.broadcasted_iota(jnp.int32, (RANK_T, RANK_T), 0)
    c = lax.broadcasted_iota(jnp.int32, (RANK_T, RANK_T), 1)
    before = jnp.where(r < c, 1.0, 0.0).astype(BF16)
    prefix = _dot(onehot.astype(BF16), before)
    carry = carry_ref[:, 0:1]
    rank = jnp.sum(onehot * (prefix + carry), axis=0, keepdims=True)
    rank_ref[0] = rank.astype(jnp.int32)
    carry_ref[...] = carry_ref[...] + jnp.sum(onehot, axis=1, keepdims=True)
    cnt_ref[...] = carry_ref[...]


def moe_rank(flat_e):
    a = flat_e.shape[0]
    nt = a // RANK_T
    rank, cnt = pl.pallas_call(
        _rank_body,
        out_shape=(jax.ShapeDtypeStruct((nt, 1, RANK_T), jnp.int32),
                   jax.ShapeDtypeStruct((N_EXPERTS, LANES), F32)),
        grid=(nt,),
        in_specs=[pl.BlockSpec((1, 1, RANK_T), lambda i: (i, 0, 0))],
        out_specs=(pl.BlockSpec((1, 1, RANK_T), lambda i: (i, 0, 0)),
                   pl.BlockSpec((N_EXPERTS, LANES), lambda i: (0, 0))),
        scratch_shapes=[pltpu.VMEM((N_EXPERTS, LANES), F32)],
        compiler_params=_cparams(("arbitrary",)),
        name="moe_rank",
    )(flat_e.reshape(nt, 1, RANK_T))
    return rank.reshape(a), cnt[:, 0].astype(jnp.int32)


def _row_copy(src, s, dst, d, sem):
    return pltpu.make_async_copy(src.at[pl.ds(s, 1), :], dst.at[pl.ds(d, 1), :], sem)


def _scatter_body(dest_ref, h_ref, xs_in_ref, xs_ref, sem):
    del xs_in_ref
    base = pl.program_id(0) * (MOE_TT * TOP_K)

    def start(r, carry):
        for k in range(TOP_K):
            _row_copy(h_ref, r, xs_ref, dest_ref[base + r * TOP_K + k], sem).start(priority=k)
        return carry

    def wait(r, carry):
        for k in range(TOP_K):
            _row_copy(h_ref, 0, xs_ref, 0, sem).wait()
        return carry

    lax.fori_loop(0, MOE_TT, start, 0)
    lax.fori_loop(0, MOE_TT, wait, 0)


def moe_scatter(h, dest, xs_init):
    m, d = h.shape
    n_rows = xs_init.shape[0]
    return pl.pallas_call(
        _scatter_body,
        out_shape=jax.ShapeDtypeStruct((n_rows, d), h.dtype),
        grid_spec=pltpu.PrefetchScalarGridSpec(
            num_scalar_prefetch=1,
            grid=(m // MOE_TT,),
            in_specs=[pl.BlockSpec((MOE_TT, d), lambda i, dest: (i, 0)),
                      pl.BlockSpec(memory_space=pl.ANY)],
            out_specs=pl.BlockSpec(memory_space=pl.ANY),
            scratch_shapes=[pltpu.SemaphoreType.DMA(())]),
        input_output_aliases={2: 0},
        compiler_params=_cparams(("arbitrary",)),
        name="moe_scatter",
    )(dest, h, xs_init)


def _moe_mm_body(blk_e_ref, nblk_ref, x_ref, w1_ref, w3_ref, w2_ref, y_ref):
    del blk_e_ref
    used = pl.program_id(0) < nblk_ref[0]

    @pl.when(used)
    def _():
        xb = x_ref[...].astype(BF16)
        hb = (jax.nn.silu(_dot(xb, w1_ref[0])) * _dot(xb, w3_ref[0])).astype(BF16)
        y_ref[...] = _dot(hb, w2_ref[0])

    @pl.when(jnp.logical_not(used))
    def _():
        y_ref[...] = jnp.zeros_like(y_ref)


def moe_experts(xs, blk_e, nblk, w1, w3, w2):
    n_rows, d = xs.shape
    ff = w1.shape[2]
    wspec = lambda r, c: pl.BlockSpec((1, r, c), lambda j, be, nb: (be[j], 0, 0))
    return pl.pallas_call(
        _moe_mm_body,
        out_shape=jax.ShapeDtypeStruct((n_rows, d), F32),
        grid_spec=pltpu.PrefetchScalarGridSpec(
            num_scalar_prefetch=2,
            grid=(n_rows // MOE_ROWS,),
            in_specs=[pl.BlockSpec((MOE_ROWS, d), lambda j, be, nb: (jnp.minimum(j, nb[0] - 1), 0)),
                      wspec(d, ff), wspec(d, ff), wspec(ff, d)],
            out_specs=pl.BlockSpec((MOE_ROWS, d), lambda j, be, nb: (j, 0))),
        compiler_params=_cparams(("arbitrary",)),
        name="moe_experts",
    )(blk_e, nblk, xs, w1, w3, w2)


def _combine_body(dest_ref, x_ref, r_ref, g_ref, y_ref, xo_ref, h_ref, buf, sem):
    i = pl.program_id(0)
    n = pl.num_programs(0)

    def start_step(step, slot):
        base = step * (MOE_TT * TOP_K)

        def body(r, carry):
            for k in range(TOP_K):
                _row_copy(y_ref, dest_ref[base + r * TOP_K + k], buf.at[slot], k * MOE_TT + r,
                          sem.at[slot]).start(priority=k)
            return carry

        lax.fori_loop(0, MOE_TT, body, 0)

    @pl.when(i == 0)
    def _():
        start_step(0, 0)

    @pl.when(i + 1 < n)
    def _():
        start_step(i + 1, (i + 1) % 2)

    slot = i % 2

    def wait(r, carry):
        for k in range(TOP_K):
            _row_copy(y_ref, 0, buf.at[slot], 0, sem.at[slot]).wait()
        return carry

    lax.fori_loop(0, MOE_TT, wait, 0)
    r = r_ref[...]
    moe = r[:, 2:3] * buf[slot, 0:MOE_TT, :] + r[:, 3:4] * buf[slot, MOE_TT:2 * MOE_TT, :]
    x = x_ref[...] + moe
    xo_ref[...] = x
    hn = x * lax.rsqrt(jnp.mean(x * x, axis=-1, keepdims=True) + NORM_EPS) * g_ref[...]
    h_ref[...] = hn.astype(h_ref.dtype)


def moe_combine(x, route, y, dest, norm):
    m, d = x.shape
    tok = lambda c: pl.BlockSpec((MOE_TT, c), lambda i, dest: (i, 0))
    return pl.pallas_call(
        _combine_body,
        out_shape=(jax.ShapeDtypeStruct((m, d), F32), jax.ShapeDtypeStruct((m, d), BF16)),
        grid_spec=pltpu.PrefetchScalarGridSpec(
            num_scalar_prefetch=1,
            grid=(m // MOE_TT,),
            in_specs=[tok(d), tok(LANES), pl.BlockSpec((1, d), lambda i, dest: (0, 0)),
                      pl.BlockSpec(memory_space=pl.ANY)],
            out_specs=(tok(d), tok(d)),
            scratch_shapes=[pltpu.VMEM((2, TOP_K * MOE_TT, d), F32),
                            pltpu.SemaphoreType.DMA((2,))]),
        compiler_params=_cparams(("arbitrary",)),
        name="moe_combine",
    )(dest, x, route, norm.reshape(1, d), y)


def moe_ffn(x, layer, norm2, rg_w, rg_b, re_w, re_b, w1, w3, w2, norm3):
    m, d = x.shape
    h, route = moe_router(x, norm2, rg_w, rg_b, re_w, re_b)
    flat_e = route[:, :TOP_K].astype(jnp.int32).reshape(m * TOP_K)
    rank, counts = moe_rank(flat_e)
    padded = (counts + MOE_ROWS - 1) // MOE_ROWS * MOE_ROWS
    pend = jnp.cumsum(padded)
    pstart = pend - padded
    dest = (pstart[flat_e] + rank).astype(jnp.int32)
    n_blocks = (m * TOP_K + N_EXPERTS * (MOE_ROWS - 1) + MOE_ROWS - 1) // MOE_ROWS
    nblk = (pend[-1:] // MOE_ROWS).astype(jnp.int32)
    starts = jnp.arange(n_blocks) * MOE_ROWS
    blk_e = jnp.minimum(jnp.sum(pend[None, :] <= starts[:, None], axis=1), N_EXPERTS - 1).astype(jnp.int32)
    xs = moe_scatter(h, dest, jnp.zeros((n_blocks * MOE_ROWS, d), h.dtype))
    y = moe_experts(xs, blk_e, nblk, cast_bf16(w1, layer), cast_bf16(w3, layer), cast_bf16(w2, layer))
    return moe_combine(x, route, y, dest, norm3)


def kernel(x, p, norm1, w_in, q_norm, k_norm, cmp_pos, cmp_w1, cmp_w2, gm_norm, gm_ws, gm_bs,
           s5_lambda_re, s5_lambda_im, s5_log_dt, s5_b_re, s5_b_im, s5_c_re, s5_c_im, s5_d,
           s5_w_glu, s5_b_glu, w_branch, w_out, norm2, router_group_w, router_group_b,
           router_expert_w, router_expert_b, expert_w1, expert_w3, expert_w2, norm3, w_ple,
           w_ple_gate):
    b, l, d = x.shape
    m = b * l
    xf = x.reshape(m, d)
    q_cols = HEADS * HEAD_DIM
    kv_cols = N_BRANCH * 2 * KV_HEADS * HEAD_DIM
    ng_cols = HEADS * N_BRANCH
    splits = [0, q_cols, q_cols + kv_cols, q_cols + kv_cols + ng_cols]
    splits.append(splits[-1] + 2 * GM_WIDTH)
    splits.append(splits[-1] + S5_WIDTH)
    splits.append(splits[-1] + 3 * d)
    per_head = GROUP * N_BRANCH
    for i in range(p.shape[0]):
        h = rms_cast(xf, norm1[i])
        w_all = cast_drop_cols(w_in, i, splits[2], ng_cols)
        widths = [q_cols, kv_cols, 2 * GM_WIDTH, S5_WIDTH, 3 * d]
        offs = [sum(widths[:k]) for k in range(len(widths))]
        proj = lambda k, name: mm_plain(h, w_all, col_off=offs[k], n=widths[k], name=name)
        q_raw = proj(0, "proj_q")
        kv_raw = proj(1, "proj_kv")
        gm_raw = proj(2, "proj_gm")
        s5_raw = proj(3, "proj_s5")
        mg_raw = proj(4, "proj_merge")
        w_ng = jnp.pad(w_in[i, :, splits[2]:splits[3]].reshape(d, KV_HEADS, per_head),
                       ((0, 0), (0, 0), (0, LANES - per_head))).reshape(d, KV_HEADS * LANES)
        ng_raw = mm_plain(h, w_ng.astype(BF16), name="proj_gate")

        o_gm = gmlp_mixer(gm_raw, gm_norm[i], gm_ws[i], gm_bs[i])
        o_s5 = s5_mixer(s5_raw, b, s5_lambda_re[i], s5_lambda_im[i], s5_log_dt[i], s5_b_re[i],
                        s5_b_im[i], s5_c_re[i], s5_c_im[i], s5_d[i], s5_w_glu[i], s5_b_glu[i])
        o_att = nsa_mixer(q_raw, kv_raw, ng_raw, b, q_norm[i], k_norm[i], cmp_pos[i], cmp_w1[i],
                          cmp_w2[i])
        wb = cast_bf16(w_branch, i)
        merged = mm_merge([o_gm, o_s5, o_att], [wb[0], wb[1], wb[2]], mg_raw)
        xf = mm_residual(merged, cast_bf16(w_out, i), xf)
        xf, h3 = moe_ffn(xf, i, norm2[i], router_group_w[i], router_group_b[i], router_expert_w[i],
                         router_expert_b[i], expert_w1, expert_w3, expert_w2, norm3[i])
        xf = mm_ple(h3, cast_bf16(w_ple_gate, i), p[i].reshape(m, -1).astype(BF16),
                    w_ple[i].astype(BF16), xf)
    return xf.reshape(b, l, d)
```

```python
import functools

import jax
import jax.numpy as jnp
from jax import lax
from jax.experimental import pallas as pl
from jax.experimental.pallas import tpu as pltpu

F32 = jnp.float32
BF16 = jnp.bfloat16

NORM_EPS = 1e-6
LANES = 128
VMEM_LIMIT = 52 * 1024 * 1024

HEADS = 16
KV_HEADS = 4
GROUP = HEADS // KV_HEADS
HEAD_DIM = 128
ROPE_THETA = 500000.0
ROPE_DIMS = HEAD_DIM // 4
ROPE_HALF = ROPE_DIMS // 2
CMP_LEN = 32
CMP_STRIDE = 16
SEL_LEN = 64
SEL_TOPN = 16
WINDOW = 512
N_BRANCH = 3
NEG = -1e30
ATT_SCALE = HEAD_DIM ** -0.5 * 1.4426950408889634
ATT_TQ = 128
ATT_TK = 1024
ONES_ROWS = 16

GM_WIDTH = 2048
GM_GROUPS = 16
GM_CHUNK = 128

S5_WIDTH = 2048
S5_GROUP_DIM = 16
S5_GROUPS = S5_WIDTH // S5_GROUP_DIM
S5_STATE = 64
S5_TILE_GROUPS = LANES // S5_GROUP_DIM
S5_TILE_STATES = S5_TILE_GROUPS * S5_STATE
S5_BLOCK = 8
S5_ROWS = 512

N_GROUPS = 4
PER_GROUP = 8
N_EXPERTS = N_GROUPS * PER_GROUP
TOP_K = 2
MOE_ROWS = 256
RANK_T = 512
MOE_TT = 128
DMA_UNROLL = 8


def _cparams(sem):
    return pltpu.CompilerParams(dimension_semantics=sem, vmem_limit_bytes=VMEM_LIMIT)


def _dot(a, b):
    return jnp.dot(a, b, preferred_element_type=F32)


def _dot_nt(a, b):
    return lax.dot_general(a, b, (((1,), (1,)), ((), ())), preferred_element_type=F32)


def _rms_body(x_ref, g_ref, o_ref):
    x = x_ref[...]
    y = x * lax.rsqrt(jnp.mean(x * x, axis=-1, keepdims=True) + NORM_EPS)
    o_ref[...] = (y * g_ref[...]).astype(o_ref.dtype)


def rms_cast(x, g, tm=256):
    m, d = x.shape
    return pl.pallas_call(
        _rms_body,
        out_shape=jax.ShapeDtypeStruct((m, d), BF16),
        grid=(m // tm,),
        in_specs=[pl.BlockSpec((tm, d), lambda i: (i, 0)),
                  pl.BlockSpec((1, d), lambda i: (0, 0))],
        out_specs=pl.BlockSpec((tm, d), lambda i: (i, 0)),
        compiler_params=_cparams(("parallel",)),
        name="rms_cast",
    )(x, g.reshape(1, d))


def _mm_plain_body(a_ref, w_ref, o_ref):
    o_ref[...] = _dot(a_ref[...], w_ref[...]).astype(o_ref.dtype)


def _mm_glu_body(a_ref, w_ref, y_ref, b_ref, o_ref):
    acc = _dot(a_ref[...].astype(BF16), w_ref[...]) + b_ref[...]
    o_ref[...] = (y_ref[...] * jax.nn.sigmoid(acc)).astype(o_ref.dtype)


def _mm_res_body(a_ref, w_ref, x_ref, o_ref):
    o_ref[...] = x_ref[...] + _dot(a_ref[...], w_ref[...])


def _mm_merge_body(a0, a1, a2, w0, w1, w2, g0, g1, g2, o_ref):
    acc = jax.nn.sigmoid(g0[...]) * _dot(a0[...], w0[...])
    acc = acc + jax.nn.sigmoid(g1[...]) * _dot(a1[...], w1[...])
    acc = acc + jax.nn.sigmoid(g2[...]) * _dot(a2[...], w2[...])
    o_ref[...] = acc.astype(o_ref.dtype)


def _mm_ple_body(a_ref, w_ref, p_ref, wp_ref, x_ref, o_ref):
    gate = jax.nn.sigmoid(_dot(a_ref[...], w_ref[...]))
    o_ref[...] = x_ref[...] + _dot(p_ref[...], wp_ref[...]) * gate


def _lhs_spec(tm, k):
    return pl.BlockSpec((tm, k), lambda j, i: (i, 0))


def _rhs_spec(k, tn):
    return pl.BlockSpec((k, tn), lambda j, i: (0, j))


def _tile_spec(tm, tn, col_blocks=0):
    return pl.BlockSpec((tm, tn), lambda j, i: (i, j + col_blocks))


def _mm_call(body, args, in_specs, m, n, tm, tn, out_dtype, name):
    return pl.pallas_call(
        body,
        out_shape=jax.ShapeDtypeStruct((m, n), out_dtype),
        grid=(n // tn, m // tm),
        in_specs=in_specs,
        out_specs=pl.BlockSpec((tm, tn), lambda j, i: (i, j)),
        compiler_params=_cparams(("parallel", "parallel")),
        name=name,
    )(*args)


def _pick(n, prefs):
    for t in prefs:
        if n % t == 0:
            return t
    return n


def mm_plain(a, w, col_off=0, n=None, out_dtype=F32, name="mm"):
    m, k = a.shape
    n = w.shape[1] if n is None else n
    tm, tn = _pick(m, (512, 256, 128)), _pick(n, (1024, 512, 256, 128))
    assert col_off % tn == 0
    rhs = pl.BlockSpec((k, tn), lambda j, i: (0, j + col_off // tn))
    return _mm_call(_mm_plain_body, (a, w), [_lhs_spec(tm, k), rhs], m, n, tm, tn, out_dtype, name)


def _drop_cols_body(x_ref, o_ref, cut_ref, *, cut, skip):
    x = x_ref[...]
    o_ref[:, :cut] = x[:, :cut].astype(o_ref.dtype)
    o_ref[:, cut:] = x[:, cut + skip:].astype(o_ref.dtype)
    cut_ref[...] = x[:, cut:cut + LANES]


def cast_drop_cols(w, layer, cut, skip, tr=64):
    _, rows, cols = w.shape
    return pl.pallas_call(
        functools.partial(_drop_cols_body, cut=cut, skip=skip),
        out_shape=(jax.ShapeDtypeStruct((rows, cols - skip), BF16),
                   jax.ShapeDtypeStruct((rows, LANES), F32)),
        grid=(rows // tr,),
        in_specs=[pl.BlockSpec((None, tr, cols), lambda i: (layer, i, 0))],
        out_specs=(pl.BlockSpec((tr, cols - skip), lambda i: (i, 0)),
                   pl.BlockSpec((tr, LANES), lambda i: (i, 0))),
        compiler_params=_cparams(("parallel",)),
        name="cast_drop_cols",
    )(w)


def _cast_body(x_ref, o_ref):
    o_ref[...] = x_ref[...].astype(o_ref.dtype)


def cast_bf16(w, layer, block_bytes=8 * 1024 * 1024):
    cols = w.shape[-1]
    x = w.reshape(w.shape[0], -1, cols)
    rows = x.shape[1]
    tr = next(t for t in (4096, 2048, 1024, 512, 256, 128, 64, 32, 16)
              if rows % t == 0 and t * cols * 4 <= block_bytes)
    out = pl.pallas_call(
        _cast_body,
        out_shape=jax.ShapeDtypeStruct((rows, cols), BF16),
        grid=(rows // tr,),
        in_specs=[pl.BlockSpec((None, tr, cols), lambda i: (layer, i, 0))],
        out_specs=pl.BlockSpec((tr, cols), lambda i: (i, 0)),
        compiler_params=_cparams(("parallel",)),
        name="cast_bf16",
    )(x)
    return out.reshape(w.shape[1:])


def mm_glu(y, w, b):
    m, k = y.shape
    n = w.shape[1]
    tm, tn = _pick(m, (512, 256, 128)), _pick(n, (1024, 512, 256, 128))
    specs = [_lhs_spec(tm, k), _rhs_spec(k, tn), _tile_spec(tm, tn),
             pl.BlockSpec((1, tn), lambda j, i: (0, j))]
    return _mm_call(_mm_glu_body, (y, w, y, b.reshape(1, n)), specs, m, n, tm, tn, BF16, "mm_glu")


def mm_residual(a, w, x):
    m, k = a.shape
    n = w.shape[1]
    tm, tn = _pick(m, (512, 256, 128)), _pick(n, (1024, 512, 256, 128))
    specs = [_lhs_spec(tm, k), _rhs_spec(k, tn), _tile_spec(tm, tn)]
    return _mm_call(_mm_res_body, (a, w, x), specs, m, n, tm, tn, F32, "mm_residual")


def mm_merge(o_list, w_list, mg_raw):
    m, k = o_list[0].shape
    n = w_list[0].shape[1]
    tm, tn = _pick(m, (512, 256, 128)), _pick(n, (512, 256, 128))
    specs = ([_lhs_spec(tm, k)] * 3 + [_rhs_spec(k, tn)] * 3
             + [_tile_spec(tm, tn, c * (n // tn)) for c in range(3)])
    return _mm_call(_mm_merge_body, (*o_list, *w_list, mg_raw, mg_raw, mg_raw), specs,
                    m, n, tm, tn, BF16, "mm_merge")


def mm_ple(h, wg, p, wp, x):
    m, k = h.shape
    n = wg.shape[1]
    kp = p.shape[1]
    tm, tn = _pick(m, (512, 256, 128)), _pick(n, (1024, 512, 256, 128))
    specs = [_lhs_spec(tm, k), _rhs_spec(k, tn), _lhs_spec(tm, kp), _rhs_spec(kp, tn),
             _tile_spec(tm, tn)]
    return _mm_call(_mm_ple_body, (h, wg, p, wp, x), specs, m, n, tm, tn, F32, "mm_ple")


def _gmlp_body(z_ref, gn_ref, w_ref, b_ref, o_ref, *, rows):
    z = jax.nn.gelu(z_ref[...])
    u = z[:, :GM_WIDTH]
    v = z[:, GM_WIDTH:]
    v = v * lax.rsqrt(jnp.mean(v * v, axis=-1, keepdims=True) + NORM_EPS) * gn_ref[...]
    vb = v.astype(BF16)
    for c in range(rows // GM_CHUNK):
        r0 = c * GM_CHUNK
        for g in range(GM_GROUPS):
            c0 = g * LANES
            s = _dot(w_ref[g], vb[r0:r0 + GM_CHUNK, c0:c0 + LANES]) + b_ref[:, c0:c0 + LANES]
            o_ref[r0:r0 + GM_CHUNK, c0:c0 + LANES] = (
                u[r0:r0 + GM_CHUNK, c0:c0 + LANES] * s).astype(o_ref.dtype)


def gmlp_mixer(z, gm_norm, ws, bs, rows=256):
    m = z.shape[0]
    tri = jnp.tril(jnp.ones((GM_CHUNK, GM_CHUNK), dtype=bool))
    w = jnp.where(tri[None], ws, 0.0).astype(BF16)
    bias = jnp.repeat(bs.T, LANES, axis=1)
    return pl.pallas_call(
        functools.partial(_gmlp_body, rows=rows),
        out_shape=jax.ShapeDtypeStruct((m, GM_WIDTH), BF16),
        grid=(m // rows,),
        in_specs=[pl.BlockSpec((rows, 2 * GM_WIDTH), lambda i: (i, 0)),
                  pl.BlockSpec((1, GM_WIDTH), lambda i: (0, 0)),
                  pl.BlockSpec((GM_GROUPS, GM_CHUNK, GM_CHUNK), lambda i: (0, 0, 0)),
                  pl.BlockSpec((GM_CHUNK, GM_WIDTH), lambda i: (0, 0))],
        out_specs=pl.BlockSpec((rows, GM_WIDTH), lambda i: (i, 0)),
        compiler_params=_cparams(("parallel",)),
        name="gmlp",
    )(z, gm_norm.reshape(1, GM_WIDTH), w, bias)


def _shift_rows(x, s):
    t = x.shape[0]
    if s % 8 == 0:
        return jnp.concatenate([jnp.zeros((s, x.shape[1]), x.dtype), x[:t - s]], axis=0)
    rolled = pltpu.roll(x, s, 0)
    row = lax.broadcasted_iota(jnp.int32, x.shape, 0)
    return jnp.where(row >= s, rolled, 0.0)


def _group_diag(table, rows_per_group, lanes_per_group):
    full = jnp.concatenate([table] * S5_TILE_GROUPS, axis=0)
    rg = lax.shift_right_logical(lax.broadcasted_iota(jnp.int32, full.shape, 0),
                                 rows_per_group.bit_length() - 1)
    lg = lax.shift_right_logical(lax.broadcasted_iota(jnp.int32, full.shape, 1),
                                 lanes_per_group.bit_length() - 1)
    return jnp.where(rg == lg, full, 0.0).astype(BF16)


def _s5_body(u_ref, kt_ref, qt_ref, rt_ref, are_ref, aim_ref, d_ref, o_ref,
             w_ref, q_ref, r_ref, hre_ref, him_ref, *, rows, levels):
    tb, hg, ns = S5_BLOCK, S5_GROUP_DIM, S5_STATE

    @pl.when(pl.program_id(2) == 0)
    def _():
        hre_ref[...] = jnp.zeros_like(hre_ref)
        him_ref[...] = jnp.zeros_like(him_ref)
        lag_blocks = [_group_diag(kt_ref[0, d], hg, hg) for d in range(tb)]
        zero = jnp.zeros((LANES, LANES), BF16)
        for ip in range(tb):
            for i in range(tb):
                w_ref[ip * LANES:(ip + 1) * LANES, i * LANES:(i + 1) * LANES] = (
                    lag_blocks[i - ip] if i >= ip else zero)
        for ip in range(tb):
            for ri in range(2):
                q_ref[ip * LANES:(ip + 1) * LANES, ri * S5_TILE_STATES:(ri + 1) * S5_TILE_STATES] = (
                    _group_diag(qt_ref[0, ip, ri], hg, ns))
        for ri in range(2):
            for i in range(tb):
                r_ref[ri * S5_TILE_STATES:(ri + 1) * S5_TILE_STATES, i * LANES:(i + 1) * LANES] = (
                    _group_diag(rt_ref[0, ri, i], ns, hg))

    us = [u_ref[pl.ds(i, rows, stride=S5_BLOCK), :] for i in range(S5_BLOCK)]
    ucat = jnp.concatenate([x.astype(BF16) for x in us], axis=1)
    s = _dot(ucat, q_ref[...])
    xr = s[:, :S5_TILE_STATES]
    xi = s[:, S5_TILE_STATES:]
    are = are_ref[0]
    aim = aim_ref[0]
    hr = hre_ref[0:1, :]
    hi = him_ref[0:1, :]
    a_r = are[0:1, :]
    a_i = aim[0:1, :]
    first = lax.broadcasted_iota(jnp.int32, xr.shape, 0) == 0
    xr = xr + jnp.where(first, a_r * hr - a_i * hi, 0.0)
    xi = xi + jnp.where(first, a_r * hi + a_i * hr, 0.0)
    for k in range(levels):
        sh = 1 << k
        cr = are[k:k + 1, :]
        ci = aim[k:k + 1, :]
        sr = _shift_rows(xr, sh)
        si = _shift_rows(xi, sh)
        xr, xi = xr + cr * sr - ci * si, xi + cr * si + ci * sr
    hre_ref[0:1, :] = xr[rows - 1:rows, :]
    him_ref[0:1, :] = xi[rows - 1:rows, :]
    xpr = _shift_rows(xr, 1) + jnp.where(first, hr, 0.0)
    xpi = _shift_rows(xi, 1) + jnp.where(first, hi, 0.0)
    xprev = jnp.concatenate([xpr, xpi], axis=1).astype(BF16)
    y = _dot(ucat, w_ref[...]) + _dot(xprev, r_ref[...])
    d = d_ref[...]
    for i in range(S5_BLOCK):
        yi = y[:, i * LANES:(i + 1) * LANES] + d * us[i]
        o_ref[pl.ds(i, rows, stride=S5_BLOCK), :] = jax.nn.gelu(yi)


def _s5_discretize(lam_re, lam_im, log_dt, b_re, b_im):
    dt = jnp.exp(log_dt.astype(F32))[:, None]
    lre = jnp.minimum(lam_re.astype(F32), -1e-4)
    lim = lam_im.astype(F32)
    mag = jnp.exp(lre * dt)
    a_re = mag * jnp.cos(lim * dt)
    a_im = mag * jnp.sin(lim * dt)
    den = lre * lre + lim * lim
    nr = a_re - 1.0
    f_re = (nr * lre + a_im * lim) / den
    f_im = (a_im * lre - nr * lim) / den
    br = b_re.astype(F32)
    bi = b_im.astype(F32)
    bb_re = f_re[..., None] * br - f_im[..., None] * bi
    bb_im = f_re[..., None] * bi + f_im[..., None] * br
    return a_re, a_im, bb_re, bb_im


def _s5_block_weights(a_re, a_im, bb_re, bb_im, c_re, c_im, levels):
    hp = lax.Precision.HIGHEST
    ng, ns = a_re.shape
    tb, gl, hg = S5_BLOCK, S5_TILE_GROUPS, S5_GROUP_DIM
    nt = ng // gl
    pr, pi = [jnp.ones_like(a_re)], [jnp.zeros_like(a_re)]
    for _ in range(tb):
        r, i = pr[-1], pi[-1]
        pr.append(r * a_re - i * a_im)
        pi.append(r * a_im + i * a_re)
    pw_re, pw_im = jnp.stack(pr), jnp.stack(pi)
    ab_re = pw_re[:tb, :, :, None] * bb_re[None] - pw_im[:tb, :, :, None] * bb_im[None]
    ab_im = pw_re[:tb, :, :, None] * bb_im[None] + pw_im[:tb, :, :, None] * bb_re[None]
    cr, ci = c_re.astype(F32), c_im.astype(F32)
    kern = (jnp.einsum('gcp,dgpk->dgck', cr, ab_re, precision=hp)
            - jnp.einsum('gcp,dgpk->dgck', ci, ab_im, precision=hp))
    kt = kern.reshape(tb, nt, gl, hg, hg).transpose(1, 0, 4, 2, 3).reshape(nt, tb, hg, gl * hg)
    q = jnp.stack([ab_re[::-1], ab_im[::-1]], axis=1).reshape(tb, 2, nt, gl, ns, hg)
    qt = q.transpose(2, 0, 1, 5, 3, 4).reshape(nt, tb, 2, hg, gl * ns)
    nr, ni = pw_re[1:, :, None, :], pw_im[1:, :, None, :]
    r = jnp.stack([cr[None] * nr - ci[None] * ni, -(cr[None] * ni + ci[None] * nr)])
    r = r.reshape(2, tb, nt, gl, hg, ns)
    rt = r.transpose(2, 0, 1, 5, 3, 4).reshape(nt, 2, tb, ns, gl * hg)
    sr, si = [pw_re[tb].reshape(-1)], [pw_im[tb].reshape(-1)]
    for _ in range(15):
        x, y = sr[-1], si[-1]
        sr.append(x * x - y * y)
        si.append(2.0 * x * y)
    assert levels <= 16
    scan_re = jnp.stack(sr).reshape(16, nt, gl * ns).transpose(1, 0, 2)
    scan_im = jnp.stack(si).reshape(16, nt, gl * ns).transpose(1, 0, 2)
    return kt, qt, rt, scan_re, scan_im


def s5_mixer(u, batch, lam_re, lam_im, log_dt, b_re, b_im, c_re, c_im, d_skip, w_glu, b_glu):
    m = u.shape[0]
    seq = m // batch
    nt = S5_GROUPS // S5_TILE_GROUPS
    nblk = seq // S5_BLOCK
    rows = min(S5_ROWS, nblk)
    levels = rows.bit_length() - 1
    assert rows == 1 << levels and nblk % rows == 0
    a_re, a_im, bb_re, bb_im = _s5_discretize(lam_re, lam_im, log_dt, b_re, b_im)
    kt, qt, rt, scan_re, scan_im = _s5_block_weights(a_re, a_im, bb_re, bb_im, c_re, c_im, levels)

    nchunk = nblk // rows
    tok = rows * S5_BLOCK
    tspec = lambda shape: pl.BlockSpec((1,) + shape, lambda b, j, t: (j,) + (0,) * len(shape))
    kin = S5_BLOCK * LANES
    y = pl.pallas_call(
        functools.partial(_s5_body, rows=rows, levels=levels),
        out_shape=jax.ShapeDtypeStruct((m, S5_WIDTH), F32),
        grid=(batch, nt, nchunk),
        in_specs=[pl.BlockSpec((tok, LANES), lambda b, j, t: (b * nchunk + t, j)),
                  tspec(kt.shape[1:]), tspec(qt.shape[1:]), tspec(rt.shape[1:]),
                  tspec((16, S5_TILE_STATES)), tspec((16, S5_TILE_STATES)),
                  pl.BlockSpec((1, LANES), lambda b, j, t: (0, j))],
        out_specs=pl.BlockSpec((tok, LANES), lambda b, j, t: (b * nchunk + t, j)),
        scratch_shapes=[pltpu.VMEM((kin, kin), BF16), pltpu.VMEM((kin, 2 * S5_TILE_STATES), BF16),
                        pltpu.VMEM((2 * S5_TILE_STATES, kin), BF16),
                        pltpu.VMEM((8, S5_TILE_STATES), F32), pltpu.VMEM((8, S5_TILE_STATES), F32)],
        compiler_params=_cparams(("parallel", "parallel", "arbitrary")),
        name="s5_scan",
    )(u, kt, qt, rt, scan_re, scan_im, d_skip.reshape(1, S5_WIDTH))
    return mm_glu(y, w_glu.astype(BF16), b_glu)


def _rope_tables(pos):
    inv_freq = ROPE_THETA ** (-jnp.arange(ROPE_HALF, dtype=F32) / ROPE_HALF)
    ang = pos.astype(F32)[:, None] * inv_freq[None, :]
    cos, sin = jnp.cos(ang), jnp.sin(ang)
    n = pos.shape[0]
    rest = HEAD_DIM - ROPE_DIMS
    c = jnp.concatenate([cos, cos, jnp.ones((n, rest), F32)], axis=1)
    s1 = jnp.concatenate([-sin, jnp.zeros((n, HEAD_DIM - ROPE_HALF), F32)], axis=1)
    s2 = jnp.concatenate([jnp.zeros((n, ROPE_HALF), F32), sin, jnp.zeros((n, rest), F32)], axis=1)
    return c, s1, s2


def _norm_rope(x, gain, c, s1, s2):
    x = x * lax.rsqrt(jnp.mean(x * x, axis=-1, keepdims=True) + NORM_EPS) * gain
    return (x * c + pltpu.roll(x, HEAD_DIM - ROPE_HALF, 1) * s1 + pltpu.roll(x, ROPE_HALF, 1) * s2)


def _kv_prep_body(ks_ref, vs_ref, kw_ref, vw_ref, kn_ref, c_ref, s1_ref, s2_ref,
                  oks_ref, ovs_ref, okw_ref, ovw_ref):
    c, s1, s2 = c_ref[...], s1_ref[...], s2_ref[...]
    oks_ref[...] = _norm_rope(ks_ref[...], kn_ref[1:2, :], c, s1, s2).astype(BF16)
    okw_ref[...] = _norm_rope(kw_ref[...], kn_ref[2:3, :], c, s1, s2).astype(BF16)
    vt = vs_ref[...].T
    ones = jnp.ones((ONES_ROWS, vt.shape[1]), F32)
    ovs_ref[...] = jnp.concatenate([vt, ones], axis=0).astype(BF16)
    ovw_ref[...] = jnp.concatenate([vw_ref[...].T, ones], axis=0).astype(BF16)


def _cmp_prep_body(k_ref, v_ref, pos_ref, w1_ref, w2_ref, kn_ref, c_ref, s1_ref, s2_ref,
                   ok_ref, ov_ref, *, ncp):
    half = CMP_LEN // 2
    outs = []
    for which, t_ref in enumerate((k_ref, v_ref)):
        lo = jnp.zeros((ncp, HEAD_DIM), F32)
        hi = jnp.zeros((ncp, HEAD_DIM), F32)
        for j in range(half):
            tj = t_ref[pl.ds(j, ncp, stride=CMP_STRIDE), :]
            a = (tj + pos_ref[which, j:j + 1, :]).astype(BF16)
            b = (tj + pos_ref[which, half + j:half + j + 1, :]).astype(BF16)
            lo = lo + _dot(a, w1_ref[which, j])
            hi = hi + _dot(b, w1_ref[which, half + j])
        pre = lo + pltpu.roll(hi, ncp - 1, 0)
        outs.append(_dot(jax.nn.gelu(pre).astype(BF16), w2_ref[which]))
    valid = lax.broadcasted_iota(jnp.int32, (ncp, HEAD_DIM), 0) < ncp - 1
    kc = _norm_rope(outs[0], kn_ref[0:1, :], c_ref[...], s1_ref[...], s2_ref[...])
    ok_ref[...] = jnp.where(valid, kc, 0.0).astype(BF16)
    ov_ref[...] = jnp.where(valid, outs[1], 0.0).T.astype(BF16)


def _attn_body(q_ref, gate_ref, c_ref, s1_ref, s2_ref, qn_ref, kc_ref, vct_ref,
               ks_ref, vst_ref, kw_ref, vwt_ref, ovt_ref, o_ref, bias_ref, s_ref, *, tq, ncp):
    s0 = pl.program_id(2) * tq
    cols = GROUP * tq
    qpos = s0 + (lax.broadcasted_iota(jnp.int32, (1, cols), 1) & (tq - 1))
    qpos1 = qpos[:, :tq]
    c, s1, s2 = c_ref[...], s1_ref[...], s2_ref[...]
    qn = qn_ref[...]
    q_t = jnp.concatenate(
        [(_norm_rope(q_ref[:, g * HEAD_DIM:(g + 1) * HEAD_DIM], qn, c, s1, s2) * ATT_SCALE).T.astype(BF16)
         for g in range(GROUP)], axis=1)

    n_id = lax.broadcasted_iota(jnp.int32, (ncp, 1), 0)
    cbias = jnp.where(n_id * CMP_STRIDE + (CMP_LEN - 1) <= qpos1, 0.0, NEG)
    sc = _dot(kc_ref[...], q_t) + jnp.concatenate([cbias] * GROUP, axis=1)
    mx = jnp.max(sc, axis=0, keepdims=True)
    e = jnp.exp2(sc - mx)
    inv = jnp.where(qpos >= CMP_LEN - 1, 1.0 / jnp.sum(e, axis=0, keepdims=True), 0.0)
    p = e * inv
    o_cmp = _dot(vct_ref[...], p.astype(BF16))

    span = WINDOW + tq
    w0 = pl.multiple_of(jnp.maximum(s0 - WINDOW, 0), tq)
    kp = w0 + lax.broadcasted_iota(jnp.int32, (span, 1), 0)
    wbias = jnp.where((kp <= qpos1) & (kp > qpos1 - WINDOW), 0.0, NEG)
    sw = _dot(kw_ref[pl.ds(w0, span), :], q_t) + jnp.concatenate([wbias] * GROUP, axis=1)
    ew = jnp.exp2(sw - jnp.max(sw, axis=0, keepdims=True))
    acc_win = _dot(vwt_ref[:, pl.ds(w0, span)], ew.astype(BF16))
    o_win = acc_win[:HEAD_DIM] / acc_win[HEAD_DIM:HEAD_DIM + 1]

    pg = p[:, 0:tq]
    for g in range(1, GROUP):
        pg = pg + p[:, g * tq:(g + 1) * tq]
    p_hi = pg.astype(BF16)
    p_lo = (pg - p_hi.astype(F32)).astype(BF16)
    imp = _dot(ovt_ref[...], p_hi) + _dot(ovt_ref[...], p_lo)
    blk = lax.broadcasted_iota(jnp.int32, (LANES, tq), 0)
    cur = lax.shift_right_logical(qpos1, 6)
    forced = (blk == 0) | (blk == cur) | (blk == cur - 1)
    val = jnp.where(forced, -jnp.inf, jnp.where(blk <= cur, imp, -1e9))
    blk_f = blk.astype(F32)
    sel = jnp.where(forced, 1.0, 0.0)
    for _ in range(SEL_TOPN - 3):
        top = jnp.max(val, axis=0, keepdims=True)
        idx = jnp.min(jnp.where(val == top, blk_f, float(LANES)), axis=0, keepdims=True)
        hit = blk_f == idx
        val = jnp.where(hit, -jnp.inf, val)
        sel = jnp.where(hit, 1.0, sel)
    bias_ref[...] = (sel - 1.0) * (-NEG)

    blocks_per_tile = ATT_TK // SEL_LEN

    def score_tile(kt, m8, diagonal):
        k0 = pl.multiple_of(kt * ATT_TK, ATT_TK)
        s = _dot(ks_ref[pl.ds(k0, ATT_TK), :], q_t)
        b8 = bias_ref[pl.ds(pl.multiple_of(kt * blocks_per_tile, blocks_per_tile), blocks_per_tile), :]
        bias = jnp.concatenate([jnp.broadcast_to(b8[j:j + 1, :], (SEL_LEN, tq))
                                for j in range(blocks_per_tile)], axis=0)
        if diagonal:
            kpos = k0 + lax.broadcasted_iota(jnp.int32, (ATT_TK, 1), 0)
            bias = jnp.where(kpos <= qpos1, bias, NEG)
        s = s + jnp.concatenate([bias] * GROUP, axis=1)
        s_ref[pl.ds(k0, ATT_TK), :] = s
        return jnp.maximum(m8, jnp.max(s.reshape(ATT_TK // 8, 8, cols), axis=0))

    n_full = lax.div(s0, ATT_TK)
    m8 = lax.fori_loop(0, n_full, lambda kt, mm: score_tile(kt, mm, False), jnp.full((8, cols), NEG, F32))
    m_sel = jnp.max(score_tile(n_full, m8, True), axis=0, keepdims=True)

    def value_tile(kt, acc):
        k0 = pl.multiple_of(kt * ATT_TK, ATT_TK)
        pe = jnp.exp2(s_ref[pl.ds(k0, ATT_TK), :] - m_sel).astype(BF16)
        return acc + _dot(vst_ref[:, pl.ds(k0, ATT_TK)], pe)

    acc_sel = lax.fori_loop(0, n_full + 1, value_tile, jnp.zeros((HEAD_DIM + ONES_ROWS, cols), F32))
    o_sel = acc_sel[:HEAD_DIM] / acc_sel[HEAD_DIM:HEAD_DIM + 1]

    gates = jax.nn.sigmoid(gate_ref[...]).T
    for g in range(GROUP):
        cs = slice(g * tq, (g + 1) * tq)
        r = g * N_BRANCH
        out = (gates[r:r + 1, :] * o_cmp[:, cs] + gates[r + 1:r + 2, :] * o_sel[:, cs]
               + gates[r + 2:r + 3, :] * o_win[:, cs])
        o_ref[:, g * HEAD_DIM:(g + 1) * HEAD_DIM] = out.T.astype(o_ref.dtype)


def nsa_mixer(q_raw, kv_raw, gate_raw, batch, q_norm, k_norm, cmp_pos, cmp_w1, cmp_w2):
    m = q_raw.shape[0]
    seq = m // batch
    ncp = seq // CMP_STRIDE
    nsel = seq // SEL_LEN
    assert seq % ATT_TK == 0 and nsel <= LANES and seq >= WINDOW + ATT_TQ
    pos = jnp.arange(seq)
    c, s1, s2 = _rope_tables(pos)
    cc, cs1, cs2 = _rope_tables(jnp.arange(ncp) * CMP_STRIDE + CMP_LEN - 1)
    col = lambda branch, kv: (branch * 2 + kv) * KV_HEADS

    tl = _pick(seq, (1024, 512))
    nl = seq // tl
    kvspec = lambda base: pl.BlockSpec((tl, HEAD_DIM), lambda b, h, t: (b * nl + t, base + h))
    tab = pl.BlockSpec((tl, HEAD_DIM), lambda b, h, t: (t, 0))
    outspec = pl.BlockSpec((None, None, tl, HEAD_DIM), lambda b, h, t: (b, h, t, 0))
    outspec_t = pl.BlockSpec((None, None, HEAD_DIM, tl), lambda b, h, t: (b, h, 0, t))
    kv_shape = jax.ShapeDtypeStruct((batch, KV_HEADS, seq, HEAD_DIM), BF16)
    kv_shape_t = jax.ShapeDtypeStruct((batch, KV_HEADS, HEAD_DIM, seq), BF16)
    vrows = HEAD_DIM + ONES_ROWS
    outspec_t1 = pl.BlockSpec((None, None, vrows, tl), lambda b, h, t: (b, h, 0, t))
    kv_shape_t1 = jax.ShapeDtypeStruct((batch, KV_HEADS, vrows, seq), BF16)
    k_s, v_s, k_w, v_w = pl.pallas_call(
        _kv_prep_body,
        out_shape=(kv_shape, kv_shape_t1, kv_shape, kv_shape_t1),
        grid=(batch, KV_HEADS, nl),
        in_specs=[kvspec(col(1, 0)), kvspec(col(1, 1)), kvspec(col(2, 0)), kvspec(col(2, 1)),
                  pl.BlockSpec((N_BRANCH, HEAD_DIM), lambda b, h, t: (0, 0)), tab, tab, tab],
        out_specs=(outspec, outspec_t1, outspec, outspec_t1),
        compiler_params=_cparams(("parallel", "parallel", "parallel")),
        name="nsa_kv_prep",
    )(kv_raw, kv_raw, kv_raw, kv_raw, k_norm, c, s1, s2)

    w1 = cmp_w1.reshape(2, CMP_LEN, HEAD_DIM, HEAD_DIM).astype(BF16)
    full = lambda shape: pl.BlockSpec(shape, lambda b, h: (0,) * len(shape))
    cshape = jax.ShapeDtypeStruct((batch, KV_HEADS, ncp, HEAD_DIM), BF16)
    cspec = pl.BlockSpec((None, None, ncp, HEAD_DIM), lambda b, h: (b, h, 0, 0))
    cshape_t = jax.ShapeDtypeStruct((batch, KV_HEADS, HEAD_DIM, ncp), BF16)
    cspec_t = pl.BlockSpec((None, None, HEAD_DIM, ncp), lambda b, h: (b, h, 0, 0))
    k_c, v_c = pl.pallas_call(
        functools.partial(_cmp_prep_body, ncp=ncp),
        out_shape=(cshape, cshape_t),
        grid=(batch, KV_HEADS),
        in_specs=[pl.BlockSpec((seq, HEAD_DIM), lambda b, h: (b, col(0, 0) + h)),
                  pl.BlockSpec((seq, HEAD_DIM), lambda b, h: (b, col(0, 1) + h)),
                  full((2, CMP_LEN, HEAD_DIM)), full((2, CMP_LEN, HEAD_DIM, HEAD_DIM)),
                  full((2, HEAD_DIM, HEAD_DIM)), full((N_BRANCH, HEAD_DIM)),
                  full((ncp, HEAD_DIM)), full((ncp, HEAD_DIM)), full((ncp, HEAD_DIM))],
        out_specs=(cspec, cspec_t),
        compiler_params=_cparams(("parallel", "parallel")),
        name="nsa_cmp_prep",
    )(kv_raw, kv_raw, cmp_pos, w1, cmp_w2.astype(BF16), k_norm, cc, cs1, cs2)

    cstart = jnp.arange(ncp)[:, None] * CMP_STRIDE
    sstart = jnp.arange(LANES)[None, :] * SEL_LEN
    overlap = jnp.maximum(jnp.minimum(cstart + CMP_LEN, sstart + SEL_LEN) - jnp.maximum(cstart, sstart), 0)
    overlap = jnp.where(jnp.arange(LANES)[None, :] < nsel, overlap, 0)
    overlap_t = (overlap.astype(F32) / CMP_STRIDE).astype(BF16).T

    tq = ATT_TQ
    nq = seq // tq
    qtab = pl.BlockSpec((tq, HEAD_DIM), lambda b, h, i: (i, 0))
    kvfull = lambda n: pl.BlockSpec((None, None, n, HEAD_DIM), lambda b, h, i: (b, h, 0, 0))
    kvfull_t = lambda n: pl.BlockSpec((None, None, HEAD_DIM, n), lambda b, h, i: (b, h, 0, 0))
    vfull_t = pl.BlockSpec((None, None, vrows, seq), lambda b, h, i: (b, h, 0, 0))
    return pl.pallas_call(
        functools.partial(_attn_body, tq=tq, ncp=ncp),
        out_shape=jax.ShapeDtypeStruct((m, HEADS * HEAD_DIM), BF16),
        grid=(batch, KV_HEADS, nq),
        in_specs=[pl.BlockSpec((tq, GROUP * HEAD_DIM), lambda b, h, i: (b * nq + i, h)),
                  pl.BlockSpec((tq, LANES), lambda b, h, i: (b * nq + i, h)),
                  qtab, qtab, qtab,
                  pl.BlockSpec((1, HEAD_DIM), lambda b, h, i: (0, 0)),
                  kvfull(ncp), kvfull_t(ncp), kvfull(seq), vfull_t, kvfull(seq), vfull_t,
                  pl.BlockSpec((LANES, ncp), lambda b, h, i: (0, 0))],
        out_specs=pl.BlockSpec((tq, GROUP * HEAD_DIM), lambda b, h, i: (b * nq + i, h)),
        scratch_shapes=[pltpu.VMEM((LANES, tq), F32), pltpu.VMEM((seq, GROUP * tq), F32)],
        compiler_params=_cparams(("parallel", "parallel", "arbitrary")),
        name="nsa_attn",
    )(q_raw, gate_raw, c, s1, s2, q_norm.reshape(1, HEAD_DIM), k_c, v_c, k_s, v_s, k_w, v_w, overlap_t)


def _router_body(x_ref, g_ref, whi_ref, wlo_ref, b_ref, h_ref, r_ref):
    x = x_ref[...]
    h = x * lax.rsqrt(jnp.mean(x * x, axis=-1, keepdims=True) + NORM_EPS) * g_ref[...]
    h_ref[...] = h
    h_hi = h.astype(BF16)
    h_lo = (h - h_hi.astype(F32)).astype(BF16)
    logits = (_dot(h_hi, whi_ref[...]) + _dot(h_lo, whi_ref[...]) + _dot(h_hi, wlo_ref[...])
              + b_ref[...])
    lane = lax.broadcasted_iota(jnp.int32, logits.shape, 1)
    lane_f = lane.astype(F32)
    none = float(LANES)
    is_g = lane < N_GROUPS
    glog = jnp.where(is_g, logits, -jnp.inf)
    gmax = jnp.max(glog, axis=-1, keepdims=True)
    gidx = jnp.min(jnp.where(glog == gmax, lane_f, none), axis=-1, keepdims=True)
    g_p = 1.0 / jnp.sum(jnp.where(is_g, jnp.exp(logits - gmax), 0.0), axis=-1, keepdims=True)
    e_lo = N_GROUPS + gidx * PER_GROUP
    elog = jnp.where((lane_f >= e_lo) & (lane_f < e_lo + PER_GROUP), logits, -jnp.inf)
    m1 = jnp.max(elog, axis=-1, keepdims=True)
    i1 = jnp.min(jnp.where(elog == m1, lane_f, none), axis=-1, keepdims=True)
    elog = jnp.where(lane_f == i1, -jnp.inf, elog)
    m2 = jnp.max(elog, axis=-1, keepdims=True)
    i2 = jnp.min(jnp.where(elog == m2, lane_f, none), axis=-1, keepdims=True)
    t = jnp.exp(m2 - m1)
    w1 = g_p / (1.0 + t)
    w2 = g_p * t / (1.0 + t)
    r_ref[...] = jnp.where(lane == 0, i1 - N_GROUPS,
                           jnp.where(lane == 1, i2 - N_GROUPS,
                                     jnp.where(lane == 2, w1, jnp.where(lane == 3, w2, 0.0))))


def moe_router(x, norm, rg_w, rg_b, re_w, re_b, tm=256):
    m, d = x.shape
    pad = LANES - N_GROUPS - N_EXPERTS
    wr = jnp.pad(jnp.concatenate([rg_w, re_w], axis=1).astype(F32), ((0, 0), (0, pad)))
    whi = wr.astype(BF16)
    wlo = (wr - whi.astype(F32)).astype(BF16)
    bias = jnp.pad(jnp.concatenate([rg_b, re_b]).astype(F32), (0, pad)).reshape(1, LANES)
    const = lambda shape: pl.BlockSpec(shape, lambda i: (0, 0))
    return pl.pallas_call(
        _router_body,
        out_shape=(jax.ShapeDtypeStruct((m, d), F32), jax.ShapeDtypeStruct((m, LANES), F32)),
        grid=(m // tm,),
        in_specs=[pl.BlockSpec((tm, d), lambda i: (i, 0)), const((1, d)),
                  const((d, LANES)), const((d, LANES)), const((1, LANES))],
        out_specs=(pl.BlockSpec((tm, d), lambda i: (i, 0)), pl.BlockSpec((tm, LANES), lambda i: (i, 0))),
        compiler_params=_cparams(("parallel",)),
        name="moe_router",
    )(x, norm.reshape(1, d), whi, wlo, bias)


def _rank_body(e_ref, rank_ref, cnt_ref, carry_ref):
    @pl.when(pl.program_id(0) == 0)
    def _():
        carry_ref[...] = jnp.zeros_like(carry_ref)

    e = e_ref[0]
    sub = lax.broadcasted_iota(jnp.int32, (N_EXPERTS, RANK_T), 0)
    onehot = jnp.where(sub == e, 1.0, 0.0)
    r = lax.broadcasted_iota(jnp.int32, (RANK_T, RANK_T), 0)
    c = lax.broadcasted_iota(jnp.int32, (RANK_T, RANK_T), 1)
    before = jnp.where(r < c, 1.0, 0.0).astype(BF16)
    prefix = _dot(onehot.astype(BF16), before)
    carry = carry_ref[:, 0:1]
    rank = jnp.sum(onehot * (prefix + carry), axis=0, keepdims=True)
    rank_ref[0] = rank.astype(jnp.int32)
    carry_ref[...] = carry_ref[...] + jnp.sum(onehot, axis=1, keepdims=True)
    cnt_ref[...] = carry_ref[...]


def moe_rank(flat_e):
    a = flat_e.shape[0]
    nt = a // RANK_T
    rank, cnt = pl.pallas_call(
        _rank_body,
        out_shape=(jax.ShapeDtypeStruct((nt, 1, RANK_T), jnp.int32),
                   jax.ShapeDtypeStruct((N_EXPERTS, LANES), F32)),
        grid=(nt,),
        in_specs=[pl.BlockSpec((1, 1, RANK_T), lambda i: (i, 0, 0))],
        out_specs=(pl.BlockSpec((1, 1, RANK_T), lambda i: (i, 0, 0)),
                   pl.BlockSpec((N_EXPERTS, LANES), lambda i: (0, 0))),
        scratch_shapes=[pltpu.VMEM((N_EXPERTS, LANES), F32)],
        compiler_params=_cparams(("arbitrary",)),
        name="moe_rank",
    )(flat_e.reshape(nt, 1, RANK_T))
    return rank.reshape(a), cnt[:, 0].astype(jnp.int32)


def _row_copy(src, s, dst, d, sem):
    return pltpu.make_async_copy(src.at[pl.ds(s, 1), :], dst.at[pl.ds(d, 1), :], sem)


def _scatter_body(dest_ref, h_ref, xs_in_ref, xs_ref, sem):
    del xs_in_ref
    base = pl.program_id(0) * (MOE_TT * TOP_K)

    def start(r, carry):
        for k in range(TOP_K):
            _row_copy(h_ref, r, xs_ref, dest_ref[base + r * TOP_K + k], sem).start(priority=k)
        return carry

    def wait(r, carry):
        for k in range(TOP_K):
            _row_copy(h_ref, 0, xs_ref, 0, sem).wait()
        return carry

    lax.fori_loop(0, MOE_TT, start, 0, unroll=DMA_UNROLL)
    lax.fori_loop(0, MOE_TT, wait, 0, unroll=DMA_UNROLL)


def moe_scatter(h, dest, xs_init):
    m, d = h.shape
    n_rows = xs_init.shape[0]
    return pl.pallas_call(
        _scatter_body,
        out_shape=jax.ShapeDtypeStruct((n_rows, d), h.dtype),
        grid_spec=pltpu.PrefetchScalarGridSpec(
            num_scalar_prefetch=1,
            grid=(m // MOE_TT,),
            in_specs=[pl.BlockSpec((MOE_TT, d), lambda i, dest: (i, 0)),
                      pl.BlockSpec(memory_space=pl.ANY)],
            out_specs=pl.BlockSpec(memory_space=pl.ANY),
            scratch_shapes=[pltpu.SemaphoreType.DMA(())]),
        input_output_aliases={2: 0},
        compiler_params=_cparams(("arbitrary",)),
        name="moe_scatter",
    )(dest, h, xs_init)


def _moe_mm_body(blk_e_ref, nblk_ref, x_ref, w1_ref, w3_ref, w2_ref, y_ref):
    del blk_e_ref
    used = pl.program_id(0) < nblk_ref[0]

    @pl.when(used)
    def _():
        xb = x_ref[...].astype(BF16)
        hb = (jax.nn.silu(_dot(xb, w1_ref[0])) * _dot(xb, w3_ref[0])).astype(BF16)
        y_ref[...] = _dot(hb, w2_ref[0])

    @pl.when(jnp.logical_not(used))
    def _():
        y_ref[...] = jnp.zeros_like(y_ref)


def moe_experts(xs, blk_e, nblk, w1, w3, w2):
    n_rows, d = xs.shape
    ff = w1.shape[2]
    wspec = lambda r, c: pl.BlockSpec((1, r, c), lambda j, be, nb: (be[j], 0, 0))
    return pl.pallas_call(
        _moe_mm_body,
        out_shape=jax.ShapeDtypeStruct((n_rows, d), F32),
        grid_spec=pltpu.PrefetchScalarGridSpec(
            num_scalar_prefetch=2,
            grid=(n_rows // MOE_ROWS,),
            in_specs=[pl.BlockSpec((MOE_ROWS, d), lambda j, be, nb: (jnp.minimum(j, nb[0] - 1), 0)),
                      wspec(d, ff), wspec(d, ff), wspec(ff, d)],
            out_specs=pl.BlockSpec((MOE_ROWS, d), lambda j, be, nb: (j, 0))),
        compiler_params=_cparams(("arbitrary",)),
        name="moe_experts",
    )(blk_e, nblk, xs, w1, w3, w2)


def _combine_body(dest_ref, x_ref, r_ref, g_ref, y_ref, xo_ref, h_ref, buf, sem):
    i = pl.program_id(0)
    n = pl.num_programs(0)

    def start_step(step, slot):
        base = step * (MOE_TT * TOP_K)

        def body(r, carry):
            for k in range(TOP_K):
                _row_copy(y_ref, dest_ref[base + r * TOP_K + k], buf.at[slot], k * MOE_TT + r,
                          sem.at[slot]).start(priority=k)
            return carry

        lax.fori_loop(0, MOE_TT, body, 0, unroll=DMA_UNROLL)

    @pl.when(i == 0)
    def _():
        start_step(0, 0)

    @pl.when(i + 1 < n)
    def _():
        start_step(i + 1, (i + 1) % 2)

    slot = i % 2

    def wait(r, carry):
        for k in range(TOP_K):
            _row_copy(y_ref, 0, buf.at[slot], 0, sem.at[slot]).wait()
        return carry

    lax.fori_loop(0, MOE_TT, wait, 0, unroll=DMA_UNROLL)
    r = r_ref[...]
    moe = r[:, 2:3] * buf[slot, 0:MOE_TT, :] + r[:, 3:4] * buf[slot, MOE_TT:2 * MOE_TT, :]
    x = x_ref[...] + moe
    xo_ref[...] = x
    hn = x * lax.rsqrt(jnp.mean(x * x, axis=-1, keepdims=True) + NORM_EPS) * g_ref[...]
    h_ref[...] = hn.astype(h_ref.dtype)


def moe_combine(x, route, y, dest, norm):
    m, d = x.shape
    tok = lambda c: pl.BlockSpec((MOE_TT, c), lambda i, dest: (i, 0))
    return pl.pallas_call(
        _combine_body,
        out_shape=(jax.ShapeDtypeStruct((m, d), F32), jax.ShapeDtypeStruct((m, d), BF16)),
        grid_spec=pltpu.PrefetchScalarGridSpec(
            num_scalar_prefetch=1,
            grid=(m // MOE_TT,),
            in_specs=[tok(d), tok(LANES), pl.BlockSpec((1, d), lambda i, dest: (0, 0)),
                      pl.BlockSpec(memory_space=pl.ANY)],
            out_specs=(tok(d), tok(d)),
            scratch_shapes=[pltpu.VMEM((2, TOP_K * MOE_TT, d), F32),
                            pltpu.SemaphoreType.DMA((2,))]),
        compiler_params=_cparams(("arbitrary",)),
        name="moe_combine",
    )(dest, x, route, norm.reshape(1, d), y)


def moe_ffn(x, layer, norm2, rg_w, rg_b, re_w, re_b, w1, w3, w2, norm3):
    m, d = x.shape
    h, route = moe_router(x, norm2, rg_w, rg_b, re_w, re_b)
    flat_e = route[:, :TOP_K].astype(jnp.int32).reshape(m * TOP_K)
    rank, counts = moe_rank(flat_e)
    padded = (counts + MOE_ROWS - 1) // MOE_ROWS * MOE_ROWS
    pend = jnp.cumsum(padded)
    pstart = pend - padded
    dest = (pstart[flat_e] + rank).astype(jnp.int32)
    n_blocks = (m * TOP_K + N_EXPERTS * (MOE_ROWS - 1) + MOE_ROWS - 1) // MOE_ROWS
    nblk = (pend[-1:] // MOE_ROWS).astype(jnp.int32)
    starts = jnp.arange(n_blocks) * MOE_ROWS
    blk_e = jnp.minimum(jnp.sum(pend[None, :] <= starts[:, None], axis=1), N_EXPERTS - 1).astype(jnp.int32)
    xs = moe_scatter(h, dest, jnp.zeros((n_blocks * MOE_ROWS, d), h.dtype))
    y = moe_experts(xs, blk_e, nblk, cast_bf16(w1, layer), cast_bf16(w3, layer), cast_bf16(w2, layer))
    return moe_combine(x, route, y, dest, norm3)


def kernel(x, p, norm1, w_in, q_norm, k_norm, cmp_pos, cmp_w1, cmp_w2, gm_norm, gm_ws, gm_bs,
           s5_lambda_re, s5_lambda_im, s5_log_dt, s5_b_re, s5_b_im, s5_c_re, s5_c_im, s5_d,
           s5_w_glu, s5_b_glu, w_branch, w_out, norm2, router_group_w, router_group_b,
           router_expert_w, router_expert_b, expert_w1, expert_w3, expert_w2, norm3, w_ple,
           w_ple_gate):
    b, l, d = x.shape
    m = b * l
    xf = x.reshape(m, d)
    q_cols = HEADS * HEAD_DIM
    kv_cols = N_BRANCH * 2 * KV_HEADS * HEAD_DIM
    ng_cols = HEADS * N_BRANCH
    splits = [0, q_cols, q_cols + kv_cols, q_cols + kv_cols + ng_cols]
    splits.append(splits[-1] + 2 * GM_WIDTH)
    splits.append(splits[-1] + S5_WIDTH)
    splits.append(splits[-1] + 3 * d)
    per_head = GROUP * N_BRANCH
    for i in range(p.shape[0]):
        h = rms_cast(xf, norm1[i])
        w_all, w_cut = cast_drop_cols(w_in, i, splits[2], ng_cols)
        widths = [q_cols, kv_cols, 2 * GM_WIDTH, S5_WIDTH, 3 * d]
        offs = [sum(widths[:k]) for k in range(len(widths))]
        proj = lambda k, name: mm_plain(h, w_all, col_off=offs[k], n=widths[k], name=name)
        q_raw = proj(0, "proj_q")
        kv_raw = proj(1, "proj_kv")
        gm_raw = proj(2, "proj_gm")
        s5_raw = proj(3, "proj_s5")
        mg_raw = proj(4, "proj_merge")
        w_ng = jnp.pad(w_cut[:, :ng_cols].reshape(d, KV_HEADS, per_head),
                       ((0, 0), (0, 0), (0, LANES - per_head))).reshape(d, KV_HEADS * LANES)
        ng_raw = mm_plain(h, w_ng.astype(BF16), name="proj_gate")

        o_gm = gmlp_mixer(gm_raw, gm_norm[i], gm_ws[i], gm_bs[i])
        o_s5 = s5_mixer(s5_raw, b, s5_lambda_re[i], s5_lambda_im[i], s5_log_dt[i], s5_b_re[i],
                        s5_b_im[i], s5_c_re[i], s5_c_im[i], s5_d[i], s5_w_glu[i], s5_b_glu[i])
        o_att = nsa_mixer(q_raw, kv_raw, ng_raw, b, q_norm[i], k_norm[i], cmp_pos[i], cmp_w1[i],
                          cmp_w2[i])
        wb = cast_bf16(w_branch, i)
        merged = mm_merge([o_gm, o_s5, o_att], [wb[0], wb[1], wb[2]], mg_raw)
        xf = mm_residual(merged, cast_bf16(w_out, i), xf)
        xf, h3 = moe_ffn(xf, i, norm2[i], router_group_w[i], router_group_b[i], router_expert_w[i],
                         router_expert_b[i], expert_w1, expert_w3, expert_w2, norm3[i])
        xf = mm_ple(h3, cast_bf16(w_ple_gate, i), p[i].reshape(m, -1).astype(BF16),
                    w_ple[i].astype(BF16), xf)
    return xf.reshape(b, l, d)
```

```python
import functools

import jax
import jax.numpy as jnp
from jax import lax
from jax.experimental import pallas as pl
from jax.experimental.pallas import tpu as pltpu

F32 = jnp.float32
BF16 = jnp.bfloat16

NORM_EPS = 1e-6
LANES = 128
VMEM_LIMIT = 52 * 1024 * 1024

HEADS = 16
KV_HEADS = 4
GROUP = HEADS // KV_HEADS
HEAD_DIM = 128
ROPE_THETA = 500000.0
ROPE_DIMS = HEAD_DIM // 4
ROPE_HALF = ROPE_DIMS // 2
CMP_LEN = 32
CMP_STRIDE = 16
SEL_LEN = 64
SEL_TOPN = 16
WINDOW = 512
N_BRANCH = 3
NEG = -1e30
ATT_SCALE = HEAD_DIM ** -0.5 * 1.4426950408889634
ATT_TQ = 128
ATT_TK = 1024
ONES_ROWS = 16

GM_WIDTH = 2048
GM_GROUPS = 16
GM_CHUNK = 128

S5_WIDTH = 2048
S5_GROUP_DIM = 16
S5_GROUPS = S5_WIDTH // S5_GROUP_DIM
S5_STATE = 64
S5_TILE_GROUPS = LANES // S5_GROUP_DIM
S5_TILE_STATES = S5_TILE_GROUPS * S5_STATE
S5_BLOCK = 8
S5_ROWS = 512

N_GROUPS = 4
PER_GROUP = 8
N_EXPERTS = N_GROUPS * PER_GROUP
TOP_K = 2
MOE_ROWS = 256
RANK_T = 512
MOE_TT = 128
DMA_UNROLL = 8


def _cparams(sem):
    return pltpu.CompilerParams(dimension_semantics=sem, vmem_limit_bytes=VMEM_LIMIT)


def _dot(a, b):
    return jnp.dot(a, b, preferred_element_type=F32)


def _dot_nt(a, b):
    return lax.dot_general(a, b, (((1,), (1,)), ((), ())), preferred_element_type=F32)


def _rms_body(x_ref, g_ref, o_ref):
    x = x_ref[...]
    y = x * lax.rsqrt(jnp.mean(x * x, axis=-1, keepdims=True) + NORM_EPS)
    o_ref[...] = (y * g_ref[...]).astype(o_ref.dtype)


def rms_cast(x, g, tm=256):
    m, d = x.shape
    return pl.pallas_call(
        _rms_body,
        out_shape=jax.ShapeDtypeStruct((m, d), BF16),
        grid=(m // tm,),
        in_specs=[pl.BlockSpec((tm, d), lambda i: (i, 0)),
                  pl.BlockSpec((1, d), lambda i: (0, 0))],
        out_specs=pl.BlockSpec((tm, d), lambda i: (i, 0)),
        compiler_params=_cparams(("parallel",)),
        name="rms_cast",
    )(x, g.reshape(1, d))


def _mm_plain_body(a_ref, w_ref, o_ref):
    o_ref[...] = _dot(a_ref[...], w_ref[...]).astype(o_ref.dtype)


def _mm_glu_body(a_ref, w_ref, y_ref, b_ref, o_ref):
    acc = _dot(a_ref[...].astype(BF16), w_ref[...]) + b_ref[...]
    o_ref[...] = (y_ref[...] * jax.nn.sigmoid(acc)).astype(o_ref.dtype)


def _mm_res_body(a_ref, w_ref, x_ref, o_ref):
    o_ref[...] = x_ref[...] + _dot(a_ref[...], w_ref[...])


def _mm_merge_body(a0, a1, a2, w0, w1, w2, g0, g1, g2, o_ref):
    acc = jax.nn.sigmoid(g0[...]) * _dot(a0[...], w0[...])
    acc = acc + jax.nn.sigmoid(g1[...]) * _dot(a1[...], w1[...])
    acc = acc + jax.nn.sigmoid(g2[...]) * _dot(a2[...], w2[...])
    o_ref[...] = acc.astype(o_ref.dtype)


def _mm_ple_body(a_ref, w_ref, p_ref, wp_ref, x_ref, o_ref):
    gate = jax.nn.sigmoid(_dot(a_ref[...], w_ref[...]))
    o_ref[...] = x_ref[...] + _dot(p_ref[...], wp_ref[...]) * gate


def _lhs_spec(tm, k):
    return pl.BlockSpec((tm, k), lambda j, i: (i, 0))


def _rhs_spec(k, tn):
    return pl.BlockSpec((k, tn), lambda j, i: (0, j))


def _tile_spec(tm, tn, col_blocks=0):
    return pl.BlockSpec((tm, tn), lambda j, i: (i, j + col_blocks))


def _mm_call(body, args, in_specs, m, n, tm, tn, out_dtype, name):
    return pl.pallas_call(
        body,
        out_shape=jax.ShapeDtypeStruct((m, n), out_dtype),
        grid=(n // tn, m // tm),
        in_specs=in_specs,
        out_specs=pl.BlockSpec((tm, tn), lambda j, i: (i, j)),
        compiler_params=_cparams(("parallel", "parallel")),
        name=name,
    )(*args)


def _pick(n, prefs):
    for t in prefs:
        if n % t == 0:
            return t
    return n


def mm_plain(a, w, col_off=0, n=None, out_dtype=F32, name="mm"):
    m, k = a.shape
    n = w.shape[1] if n is None else n
    tm, tn = _pick(m, (512, 256, 128)), _pick(n, (1024, 512, 256, 128))
    assert col_off % tn == 0
    rhs = pl.BlockSpec((k, tn), lambda j, i: (0, j + col_off // tn))
    return _mm_call(_mm_plain_body, (a, w), [_lhs_spec(tm, k), rhs], m, n, tm, tn, out_dtype, name)


def _drop_cols_body(x_ref, o_ref, cut_ref, *, cut, skip):
    x = x_ref[...]
    o_ref[:, :cut] = x[:, :cut].astype(o_ref.dtype)
    o_ref[:, cut:] = x[:, cut + skip:].astype(o_ref.dtype)
    cut_ref[...] = x[:, cut:cut + LANES]


def cast_drop_cols(w, layer, cut, skip, tr=64):
    _, rows, cols = w.shape
    return pl.pallas_call(
        functools.partial(_drop_cols_body, cut=cut, skip=skip),
        out_shape=(jax.ShapeDtypeStruct((rows, cols - skip), BF16),
                   jax.ShapeDtypeStruct((rows, LANES), F32)),
        grid=(rows // tr,),
        in_specs=[pl.BlockSpec((None, tr, cols), lambda i: (layer, i, 0))],
        out_specs=(pl.BlockSpec((tr, cols - skip), lambda i: (i, 0)),
                   pl.BlockSpec((tr, LANES), lambda i: (i, 0))),
        compiler_params=_cparams(("parallel",)),
        name="cast_drop_cols",
    )(w)


def _cast_t_body(x_ref, o_ref):
    o_ref[...] = x_ref[...].T.astype(o_ref.dtype)


def cast_transposed_drop_rows(wt, layer, cut, skip, tn=512):
    _, n, k = wt.shape
    assert cut % tn == 0 and (n - skip) % tn == 0
    assert n % 8 == 0 and skip % 8 == 0
    src_row = lambda j: pl.multiple_of(layer * n + j * tn + jnp.where(j * tn >= cut, skip, 0), 8)
    return pl.pallas_call(
        _cast_t_body,
        out_shape=jax.ShapeDtypeStruct((k, n - skip), BF16),
        grid=((n - skip) // tn,),
        in_specs=[pl.BlockSpec((pl.Element(tn), pl.Element(k)), lambda j: (src_row(j), 0))],
        out_specs=pl.BlockSpec((k, tn), lambda j: (0, j)),
        compiler_params=_cparams(("parallel",)),
        name="cast_transposed",
    )(wt.reshape(-1, k))


def _cast_body(x_ref, o_ref):
    o_ref[...] = x_ref[...].astype(o_ref.dtype)


def cast_bf16(w, layer, block_bytes=8 * 1024 * 1024):
    cols = w.shape[-1]
    x = w.reshape(w.shape[0], -1, cols)
    rows = x.shape[1]
    tr = next(t for t in (4096, 2048, 1024, 512, 256, 128, 64, 32, 16)
              if rows % t == 0 and t * cols * 4 <= block_bytes)
    out = pl.pallas_call(
        _cast_body,
        out_shape=jax.ShapeDtypeStruct((rows, cols), BF16),
        grid=(rows // tr,),
        in_specs=[pl.BlockSpec((None, tr, cols), lambda i: (layer, i, 0))],
        out_specs=pl.BlockSpec((tr, cols), lambda i: (i, 0)),
        compiler_params=_cparams(("parallel",)),
        name="cast_bf16",
    )(x)
    return out.reshape(w.shape[1:])


def mm_glu(y, w, b):
    m, k = y.shape
    n = w.shape[1]
    tm, tn = _pick(m, (512, 256, 128)), _pick(n, (1024, 512, 256, 128))
    specs = [_lhs_spec(tm, k), _rhs_spec(k, tn), _tile_spec(tm, tn),
             pl.BlockSpec((1, tn), lambda j, i: (0, j))]
    return _mm_call(_mm_glu_body, (y, w, y, b.reshape(1, n)), specs, m, n, tm, tn, BF16, "mm_glu")


def mm_residual(a, w, x):
    m, k = a.shape
    n = w.shape[1]
    tm, tn = _pick(m, (512, 256, 128)), _pick(n, (1024, 512, 256, 128))
    specs = [_lhs_spec(tm, k), _rhs_spec(k, tn), _tile_spec(tm, tn)]
    return _mm_call(_mm_res_body, (a, w, x), specs, m, n, tm, tn, F32, "mm_residual")


def mm_merge(o_list, w_list, mg_raw):
    m, k = o_list[0].shape
    n = w_list[0].shape[1]
    tm, tn = _pick(m, (512, 256, 128)), _pick(n, (512, 256, 128))
    specs = ([_lhs_spec(tm, k)] * 3 + [_rhs_spec(k, tn)] * 3
             + [_tile_spec(tm, tn, c * (n // tn)) for c in range(3)])
    return _mm_call(_mm_merge_body, (*o_list, *w_list, mg_raw, mg_raw, mg_raw), specs,
                    m, n, tm, tn, BF16, "mm_merge")


def mm_ple(h, wg, p, wp, x):
    m, k = h.shape
    n = wg.shape[1]
    kp = p.shape[1]
    tm, tn = _pick(m, (512, 256, 128)), _pick(n, (1024, 512, 256, 128))
    specs = [_lhs_spec(tm, k), _rhs_spec(k, tn), _lhs_spec(tm, kp), _rhs_spec(kp, tn),
             _tile_spec(tm, tn)]
    return _mm_call(_mm_ple_body, (h, wg, p, wp, x), specs, m, n, tm, tn, F32, "mm_ple")


def _gmlp_body(z_ref, gn_ref, w_ref, b_ref, o_ref, *, rows):
    z = jax.nn.gelu(z_ref[...])
    u = z[:, :GM_WIDTH]
    v = z[:, GM_WIDTH:]
    v = v * lax.rsqrt(jnp.mean(v * v, axis=-1, keepdims=True) + NORM_EPS) * gn_ref[...]
    vb = v.astype(BF16)
    for c in range(rows // GM_CHUNK):
        r0 = c * GM_CHUNK
        for g in range(GM_GROUPS):
            c0 = g * LANES
            s = _dot(w_ref[g], vb[r0:r0 + GM_CHUNK, c0:c0 + LANES]) + b_ref[:, c0:c0 + LANES]
            o_ref[r0:r0 + GM_CHUNK, c0:c0 + LANES] = (
                u[r0:r0 + GM_CHUNK, c0:c0 + LANES] * s).astype(o_ref.dtype)


def gmlp_mixer(z, gm_norm, ws, bs, rows=256):
    m = z.shape[0]
    tri = jnp.tril(jnp.ones((GM_CHUNK, GM_CHUNK), dtype=bool))
    w = jnp.where(tri[None], ws, 0.0).astype(BF16)
    bias = jnp.repeat(bs.T, LANES, axis=1)
    return pl.pallas_call(
        functools.partial(_gmlp_body, rows=rows),
        out_shape=jax.ShapeDtypeStruct((m, GM_WIDTH), BF16),
        grid=(m // rows,),
        in_specs=[pl.BlockSpec((rows, 2 * GM_WIDTH), lambda i: (i, 0)),
                  pl.BlockSpec((1, GM_WIDTH), lambda i: (0, 0)),
                  pl.BlockSpec((GM_GROUPS, GM_CHUNK, GM_CHUNK), lambda i: (0, 0, 0)),
                  pl.BlockSpec((GM_CHUNK, GM_WIDTH), lambda i: (0, 0))],
        out_specs=pl.BlockSpec((rows, GM_WIDTH), lambda i: (i, 0)),
        compiler_params=_cparams(("parallel",)),
        name="gmlp",
    )(z, gm_norm.reshape(1, GM_WIDTH), w, bias)


def _shift_rows(x, s):
    t = x.shape[0]
    if s % 8 == 0:
        return jnp.concatenate([jnp.zeros((s, x.shape[1]), x.dtype), x[:t - s]], axis=0)
    rolled = pltpu.roll(x, s, 0)
    row = lax.broadcasted_iota(jnp.int32, x.shape, 0)
    return jnp.where(row >= s, rolled, 0.0)


def _group_diag(table, rows_per_group, lanes_per_group):
    full = jnp.concatenate([table] * S5_TILE_GROUPS, axis=0)
    rg = lax.shift_right_logical(lax.broadcasted_iota(jnp.int32, full.shape, 0),
                                 rows_per_group.bit_length() - 1)
    lg = lax.shift_right_logical(lax.broadcasted_iota(jnp.int32, full.shape, 1),
                                 lanes_per_group.bit_length() - 1)
    return jnp.where(rg == lg, full, 0.0).astype(BF16)


def _s5_body(u_ref, kt_ref, qt_ref, rt_ref, are_ref, aim_ref, d_ref, o_ref,
             w_ref, q_ref, r_ref, hre_ref, him_ref, *, rows, levels):
    tb, hg, ns = S5_BLOCK, S5_GROUP_DIM, S5_STATE

    @pl.when(pl.program_id(2) == 0)
    def _():
        hre_ref[...] = jnp.zeros_like(hre_ref)
        him_ref[...] = jnp.zeros_like(him_ref)
        lag_blocks = [_group_diag(kt_ref[0, d], hg, hg) for d in range(tb)]
        zero = jnp.zeros((LANES, LANES), BF16)
        for ip in range(tb):
            for i in range(tb):
                w_ref[ip * LANES:(ip + 1) * LANES, i * LANES:(i + 1) * LANES] = (
                    lag_blocks[i - ip] if i >= ip else zero)
        for ip in range(tb):
            for ri in range(2):
                q_ref[ip * LANES:(ip + 1) * LANES, ri * S5_TILE_STATES:(ri + 1) * S5_TILE_STATES] = (
                    _group_diag(qt_ref[0, ip, ri], hg, ns))
        for ri in range(2):
            for i in range(tb):
                r_ref[ri * S5_TILE_STATES:(ri + 1) * S5_TILE_STATES, i * LANES:(i + 1) * LANES] = (
                    _group_diag(rt_ref[0, ri, i], ns, hg))

    us = [u_ref[pl.ds(i, rows, stride=S5_BLOCK), :] for i in range(S5_BLOCK)]
    ucat = jnp.concatenate([x.astype(BF16) for x in us], axis=1)
    s = _dot(ucat, q_ref[...])
    xr = s[:, :S5_TILE_STATES]
    xi = s[:, S5_TILE_STATES:]
    are = are_ref[0]
    aim = aim_ref[0]
    hr = hre_ref[0:1, :]
    hi = him_ref[0:1, :]
    a_r = are[0:1, :]
    a_i = aim[0:1, :]
    first = lax.broadcasted_iota(jnp.int32, xr.shape, 0) == 0
    xr = xr + jnp.where(first, a_r * hr - a_i * hi, 0.0)
    xi = xi + jnp.where(first, a_r * hi + a_i * hr, 0.0)
    for k in range(levels):
        sh = 1 << k
        cr = are[k:k + 1, :]
        ci = aim[k:k + 1, :]
        sr = _shift_rows(xr, sh)
        si = _shift_rows(xi, sh)
        xr, xi = xr + cr * sr - ci * si, xi + cr * si + ci * sr
    hre_ref[0:1, :] = xr[rows - 1:rows, :]
    him_ref[0:1, :] = xi[rows - 1:rows, :]
    xpr = _shift_rows(xr, 1) + jnp.where(first, hr, 0.0)
    xpi = _shift_rows(xi, 1) + jnp.where(first, hi, 0.0)
    xprev = jnp.concatenate([xpr, xpi], axis=1).astype(BF16)
    y = _dot(ucat, w_ref[...]) + _dot(xprev, r_ref[...])
    d = d_ref[...]
    for i in range(S5_BLOCK):
        yi = y[:, i * LANES:(i + 1) * LANES] + d * us[i]
        o_ref[pl.ds(i, rows, stride=S5_BLOCK), :] = jax.nn.gelu(yi)


def _s5_discretize(lam_re, lam_im, log_dt, b_re, b_im):
    dt = jnp.exp(log_dt.astype(F32))[:, None]
    lre = jnp.minimum(lam_re.astype(F32), -1e-4)
    lim = lam_im.astype(F32)
    mag = jnp.exp(lre * dt)
    a_re = mag * jnp.cos(lim * dt)
    a_im = mag * jnp.sin(lim * dt)
    den = lre * lre + lim * lim
    nr = a_re - 1.0
    f_re = (nr * lre + a_im * lim) / den
    f_im = (a_im * lre - nr * lim) / den
    br = b_re.astype(F32)
    bi = b_im.astype(F32)
    bb_re = f_re[..., None] * br - f_im[..., None] * bi
    bb_im = f_re[..., None] * bi + f_im[..., None] * br
    return a_re, a_im, bb_re, bb_im


def _s5_block_weights(a_re, a_im, bb_re, bb_im, c_re, c_im, levels):
    hp = lax.Precision.HIGHEST
    ng, ns = a_re.shape
    tb, gl, hg = S5_BLOCK, S5_TILE_GROUPS, S5_GROUP_DIM
    nt = ng // gl
    pr, pi = [jnp.ones_like(a_re)], [jnp.zeros_like(a_re)]
    for _ in range(tb):
        r, i = pr[-1], pi[-1]
        pr.append(r * a_re - i * a_im)
        pi.append(r * a_im + i * a_re)
    pw_re, pw_im = jnp.stack(pr), jnp.stack(pi)
    ab_re = pw_re[:tb, :, :, None] * bb_re[None] - pw_im[:tb, :, :, None] * bb_im[None]
    ab_im = pw_re[:tb, :, :, None] * bb_im[None] + pw_im[:tb, :, :, None] * bb_re[None]
    cr, ci = c_re.astype(F32), c_im.astype(F32)
    kern = (jnp.einsum('gcp,dgpk->dgck', cr, ab_re, precision=hp)
            - jnp.einsum('gcp,dgpk->dgck', ci, ab_im, precision=hp))
    kt = kern.reshape(tb, nt, gl, hg, hg).transpose(1, 0, 4, 2, 3).reshape(nt, tb, hg, gl * hg)
    q = jnp.stack([ab_re[::-1], ab_im[::-1]], axis=1).reshape(tb, 2, nt, gl, ns, hg)
    qt = q.transpose(2, 0, 1, 5, 3, 4).reshape(nt, tb, 2, hg, gl * ns)
    nr, ni = pw_re[1:, :, None, :], pw_im[1:, :, None, :]
    r = jnp.stack([cr[None] * nr - ci[None] * ni, -(cr[None] * ni + ci[None] * nr)])
    r = r.reshape(2, tb, nt, gl, hg, ns)
    rt = r.transpose(2, 0, 1, 5, 3, 4).reshape(nt, 2, tb, ns, gl * hg)
    sr, si = [pw_re[tb].reshape(-1)], [pw_im[tb].reshape(-1)]
    for _ in range(15):
        x, y = sr[-1], si[-1]
        sr.append(x * x - y * y)
        si.append(2.0 * x * y)
    assert levels <= 16
    scan_re = jnp.stack(sr).reshape(16, nt, gl * ns).transpose(1, 0, 2)
    scan_im = jnp.stack(si).reshape(16, nt, gl * ns).transpose(1, 0, 2)
    return kt, qt, rt, scan_re, scan_im


def s5_mixer(u, batch, lam_re, lam_im, log_dt, b_re, b_im, c_re, c_im, d_skip, w_glu, b_glu):
    m = u.shape[0]
    seq = m // batch
    nt = S5_GROUPS // S5_TILE_GROUPS
    nblk = seq // S5_BLOCK
    rows = min(S5_ROWS, nblk)
    levels = rows.bit_length() - 1
    assert rows == 1 << levels and nblk % rows == 0
    a_re, a_im, bb_re, bb_im = _s5_discretize(lam_re, lam_im, log_dt, b_re, b_im)
    kt, qt, rt, scan_re, scan_im = _s5_block_weights(a_re, a_im, bb_re, bb_im, c_re, c_im, levels)

    nchunk = nblk // rows
    tok = rows * S5_BLOCK
    tspec = lambda shape: pl.BlockSpec((1,) + shape, lambda b, j, t: (j,) + (0,) * len(shape))
    kin = S5_BLOCK * LANES
    y = pl.pallas_call(
        functools.partial(_s5_body, rows=rows, levels=levels),
        out_shape=jax.ShapeDtypeStruct((m, S5_WIDTH), F32),
        grid=(batch, nt, nchunk),
        in_specs=[pl.BlockSpec((tok, LANES), lambda b, j, t: (b * nchunk + t, j)),
                  tspec(kt.shape[1:]), tspec(qt.shape[1:]), tspec(rt.shape[1:]),
                  tspec((16, S5_TILE_STATES)), tspec((16, S5_TILE_STATES)),
                  pl.BlockSpec((1, LANES), lambda b, j, t: (0, j))],
        out_specs=pl.BlockSpec((tok, LANES), lambda b, j, t: (b * nchunk + t, j)),
        scratch_shapes=[pltpu.VMEM((kin, kin), BF16), pltpu.VMEM((kin, 2 * S5_TILE_STATES), BF16),
                        pltpu.VMEM((2 * S5_TILE_STATES, kin), BF16),
                        pltpu.VMEM((8, S5_TILE_STATES), F32), pltpu.VMEM((8, S5_TILE_STATES), F32)],
        compiler_params=_cparams(("parallel", "parallel", "arbitrary")),
        name="s5_scan",
    )(u, kt, qt, rt, scan_re, scan_im, d_skip.reshape(1, S5_WIDTH))
    return mm_glu(y, w_glu.astype(BF16), b_glu)


def _rope_tables(pos):
    inv_freq = ROPE_THETA ** (-jnp.arange(ROPE_HALF, dtype=F32) / ROPE_HALF)
    ang = pos.astype(F32)[:, None] * inv_freq[None, :]
    cos, sin = jnp.cos(ang), jnp.sin(ang)
    n = pos.shape[0]
    rest = HEAD_DIM - ROPE_DIMS
    c = jnp.concatenate([cos, cos, jnp.ones((n, rest), F32)], axis=1)
    s1 = jnp.concatenate([-sin, jnp.zeros((n, HEAD_DIM - ROPE_HALF), F32)], axis=1)
    s2 = jnp.concatenate([jnp.zeros((n, ROPE_HALF), F32), sin, jnp.zeros((n, rest), F32)], axis=1)
    return c, s1, s2


def _norm_rope(x, gain, c, s1, s2):
    x = x * lax.rsqrt(jnp.mean(x * x, axis=-1, keepdims=True) + NORM_EPS) * gain
    return (x * c + pltpu.roll(x, HEAD_DIM - ROPE_HALF, 1) * s1 + pltpu.roll(x, ROPE_HALF, 1) * s2)


def _kv_prep_body(ks_ref, vs_ref, kw_ref, vw_ref, kn_ref, c_ref, s1_ref, s2_ref,
                  oks_ref, ovs_ref, okw_ref, ovw_ref):
    c, s1, s2 = c_ref[...], s1_ref[...], s2_ref[...]
    oks_ref[...] = _norm_rope(ks_ref[...], kn_ref[1:2, :], c, s1, s2).astype(BF16)
    okw_ref[...] = _norm_rope(kw_ref[...], kn_ref[2:3, :], c, s1, s2).astype(BF16)
    vt = vs_ref[...].T
    ones = jnp.ones((ONES_ROWS, vt.shape[1]), F32)
    ovs_ref[...] = jnp.concatenate([vt, ones], axis=0).astype(BF16)
    ovw_ref[...] = jnp.concatenate([vw_ref[...].T, ones], axis=0).astype(BF16)


def _cmp_prep_body(k_ref, v_ref, pos_ref, w1_ref, w2_ref, kn_ref, c_ref, s1_ref, s2_ref,
                   ok_ref, ov_ref, *, ncp):
    half = CMP_LEN // 2
    outs = []
    for which, t_ref in enumerate((k_ref, v_ref)):
        lo = jnp.zeros((ncp, HEAD_DIM), F32)
        hi = jnp.zeros((ncp, HEAD_DIM), F32)
        for j in range(half):
            tj = t_ref[pl.ds(j, ncp, stride=CMP_STRIDE), :]
            a = (tj + pos_ref[which, j:j + 1, :]).astype(BF16)
            b = (tj + pos_ref[which, half + j:half + j + 1, :]).astype(BF16)
            lo = lo + _dot(a, w1_ref[which, j])
            hi = hi + _dot(b, w1_ref[which, half + j])
        pre = lo + pltpu.roll(hi, ncp - 1, 0)
        outs.append(_dot(jax.nn.gelu(pre).astype(BF16), w2_ref[which]))
    valid = lax.broadcasted_iota(jnp.int32, (ncp, HEAD_DIM), 0) < ncp - 1
    kc = _norm_rope(outs[0], kn_ref[0:1, :], c_ref[...], s1_ref[...], s2_ref[...])
    ok_ref[...] = jnp.where(valid, kc, 0.0).astype(BF16)
    ov_ref[...] = jnp.where(valid, outs[1], 0.0).T.astype(BF16)


def _attn_body(q_ref, gate_ref, c_ref, s1_ref, s2_ref, qn_ref, kc_ref, vct_ref,
               ks_ref, vst_ref, kw_ref, vwt_ref, ovt_ref, o_ref, bias_ref, s_ref, *, tq, ncp):
    s0 = pl.program_id(2) * tq
    cols = GROUP * tq
    qpos = s0 + (lax.broadcasted_iota(jnp.int32, (1, cols), 1) & (tq - 1))
    qpos1 = qpos[:, :tq]
    c, s1, s2 = c_ref[...], s1_ref[...], s2_ref[...]
    qn = qn_ref[...]
    q_t = jnp.concatenate(
        [(_norm_rope(q_ref[:, g * HEAD_DIM:(g + 1) * HEAD_DIM], qn, c, s1, s2) * ATT_SCALE).T.astype(BF16)
         for g in range(GROUP)], axis=1)

    n_id = lax.broadcasted_iota(jnp.int32, (ncp, 1), 0)
    cbias = jnp.where(n_id * CMP_STRIDE + (CMP_LEN - 1) <= qpos1, 0.0, NEG)
    sc = _dot(kc_ref[...], q_t) + jnp.concatenate([cbias] * GROUP, axis=1)
    mx = jnp.max(sc, axis=0, keepdims=True)
    e = jnp.exp2(sc - mx)
    inv = jnp.where(qpos >= CMP_LEN - 1, 1.0 / jnp.sum(e, axis=0, keepdims=True), 0.0)
    p = e * inv
    o_cmp = _dot(vct_ref[...], p.astype(BF16))

    span = WINDOW + tq
    w0 = pl.multiple_of(jnp.maximum(s0 - WINDOW, 0), tq)
    kp = w0 + lax.broadcasted_iota(jnp.int32, (span, 1), 0)
    wbias = jnp.where((kp <= qpos1) & (kp > qpos1 - WINDOW), 0.0, NEG)
    sw = _dot(kw_ref[pl.ds(w0, span), :], q_t) + jnp.concatenate([wbias] * GROUP, axis=1)
    ew = jnp.exp2(sw - jnp.max(sw, axis=0, keepdims=True))
    acc_win = _dot(vwt_ref[:, pl.ds(w0, span)], ew.astype(BF16))
    o_win = acc_win[:HEAD_DIM] / acc_win[HEAD_DIM:HEAD_DIM + 1]

    pg = p[:, 0:tq]
    for g in range(1, GROUP):
        pg = pg + p[:, g * tq:(g + 1) * tq]
    p_hi = pg.astype(BF16)
    p_lo = (pg - p_hi.astype(F32)).astype(BF16)
    imp = _dot(ovt_ref[...], p_hi) + _dot(ovt_ref[...], p_lo)
    blk = lax.broadcasted_iota(jnp.int32, (LANES, tq), 0)
    cur = lax.shift_right_logical(qpos1, 6)
    forced = (blk == 0) | (blk == cur) | (blk == cur - 1)
    val = jnp.where(forced, -jnp.inf, jnp.where(blk <= cur, imp, -1e9))
    blk_f = blk.astype(F32)
    sel = jnp.where(forced, 1.0, 0.0)
    for _ in range(SEL_TOPN - 3):
        top = jnp.max(val, axis=0, keepdims=True)
        idx = jnp.min(jnp.where(val == top, blk_f, float(LANES)), axis=0, keepdims=True)
        hit = blk_f == idx
        val = jnp.where(hit, -jnp.inf, val)
        sel = jnp.where(hit, 1.0, sel)
    bias_ref[...] = (sel - 1.0) * (-NEG)

    blocks_per_tile = ATT_TK // SEL_LEN

    def score_tile(kt, m8, diagonal):
        k0 = pl.multiple_of(kt * ATT_TK, ATT_TK)
        s = _dot(ks_ref[pl.ds(k0, ATT_TK), :], q_t)
        b8 = bias_ref[pl.ds(pl.multiple_of(kt * blocks_per_tile, blocks_per_tile), blocks_per_tile), :]
        bias = jnp.concatenate([jnp.broadcast_to(b8[j:j + 1, :], (SEL_LEN, tq))
                                for j in range(blocks_per_tile)], axis=0)
        if diagonal:
            kpos = k0 + lax.broadcasted_iota(jnp.int32, (ATT_TK, 1), 0)
            bias = jnp.where(kpos <= qpos1, bias, NEG)
        s = s + jnp.concatenate([bias] * GROUP, axis=1)
        s_ref[pl.ds(k0, ATT_TK), :] = s
        return jnp.maximum(m8, jnp.max(s.reshape(ATT_TK // 8, 8, cols), axis=0))

    n_full = lax.div(s0, ATT_TK)
    m8 = lax.fori_loop(0, n_full, lambda kt, mm: score_tile(kt, mm, False), jnp.full((8, cols), NEG, F32))
    m_sel = jnp.max(score_tile(n_full, m8, True), axis=0, keepdims=True)

    def value_tile(kt, acc):
        k0 = pl.multiple_of(kt * ATT_TK, ATT_TK)
        pe = jnp.exp2(s_ref[pl.ds(k0, ATT_TK), :] - m_sel).astype(BF16)
        return acc + _dot(vst_ref[:, pl.ds(k0, ATT_TK)], pe)

    acc_sel = lax.fori_loop(0, n_full + 1, value_tile, jnp.zeros((HEAD_DIM + ONES_ROWS, cols), F32))
    o_sel = acc_sel[:HEAD_DIM] / acc_sel[HEAD_DIM:HEAD_DIM + 1]

    gates = jax.nn.sigmoid(gate_ref[...]).T
    for g in range(GROUP):
        cs = slice(g * tq, (g + 1) * tq)
        r = g * N_BRANCH
        out = (gates[r:r + 1, :] * o_cmp[:, cs] + gates[r + 1:r + 2, :] * o_sel[:, cs]
               + gates[r + 2:r + 3, :] * o_win[:, cs])
        o_ref[:, g * HEAD_DIM:(g + 1) * HEAD_DIM] = out.T.astype(o_ref.dtype)


def nsa_mixer(q_raw, kv_raw, gate_raw, batch, q_norm, k_norm, cmp_pos, cmp_w1, cmp_w2):
    m = q_raw.shape[0]
    seq = m // batch
    ncp = seq // CMP_STRIDE
    nsel = seq // SEL_LEN
    assert seq % ATT_TK == 0 and nsel <= LANES and seq >= WINDOW + ATT_TQ
    pos = jnp.arange(seq)
    c, s1, s2 = _rope_tables(pos)
    cc, cs1, cs2 = _rope_tables(jnp.arange(ncp) * CMP_STRIDE + CMP_LEN - 1)
    col = lambda branch, kv: (branch * 2 + kv) * KV_HEADS

    tl = _pick(seq, (1024, 512))
    nl = seq // tl
    kvspec = lambda base: pl.BlockSpec((tl, HEAD_DIM), lambda b, h, t: (b * nl + t, base + h))
    tab = pl.BlockSpec((tl, HEAD_DIM), lambda b, h, t: (t, 0))
    outspec = pl.BlockSpec((None, None, tl, HEAD_DIM), lambda b, h, t: (b, h, t, 0))
    outspec_t = pl.BlockSpec((None, None, HEAD_DIM, tl), lambda b, h, t: (b, h, 0, t))
    kv_shape = jax.ShapeDtypeStruct((batch, KV_HEADS, seq, HEAD_DIM), BF16)
    kv_shape_t = jax.ShapeDtypeStruct((batch, KV_HEADS, HEAD_DIM, seq), BF16)
    vrows = HEAD_DIM + ONES_ROWS
    outspec_t1 = pl.BlockSpec((None, None, vrows, tl), lambda b, h, t: (b, h, 0, t))
    kv_shape_t1 = jax.ShapeDtypeStruct((batch, KV_HEADS, vrows, seq), BF16)
    k_s, v_s, k_w, v_w = pl.pallas_call(
        _kv_prep_body,
        out_shape=(kv_shape, kv_shape_t1, kv_shape, kv_shape_t1),
        grid=(batch, KV_HEADS, nl),
        in_specs=[kvspec(col(1, 0)), kvspec(col(1, 1)), kvspec(col(2, 0)), kvspec(col(2, 1)),
                  pl.BlockSpec((N_BRANCH, HEAD_DIM), lambda b, h, t: (0, 0)), tab, tab, tab],
        out_specs=(outspec, outspec_t1, outspec, outspec_t1),
        compiler_params=_cparams(("parallel", "parallel", "parallel")),
        name="nsa_kv_prep",
    )(kv_raw, kv_raw, kv_raw, kv_raw, k_norm, c, s1, s2)

    w1 = cmp_w1.reshape(2, CMP_LEN, HEAD_DIM, HEAD_DIM).astype(BF16)
    full = lambda shape: pl.BlockSpec(shape, lambda b, h: (0,) * len(shape))
    cshape = jax.ShapeDtypeStruct((batch, KV_HEADS, ncp, HEAD_DIM), BF16)
    cspec = pl.BlockSpec((None, None, ncp, HEAD_DIM), lambda b, h: (b, h, 0, 0))
    cshape_t = jax.ShapeDtypeStruct((batch, KV_HEADS, HEAD_DIM, ncp), BF16)
    cspec_t = pl.BlockSpec((None, None, HEAD_DIM, ncp), lambda b, h: (b, h, 0, 0))
    k_c, v_c = pl.pallas_call(
        functools.partial(_cmp_prep_body, ncp=ncp),
        out_shape=(cshape, cshape_t),
        grid=(batch, KV_HEADS),
        in_specs=[pl.BlockSpec((seq, HEAD_DIM), lambda b, h: (b, col(0, 0) + h)),
                  pl.BlockSpec((seq, HEAD_DIM), lambda b, h: (b, col(0, 1) + h)),
                  full((2, CMP_LEN, HEAD_DIM)), full((2, CMP_LEN, HEAD_DIM, HEAD_DIM)),
                  full((2, HEAD_DIM, HEAD_DIM)), full((N_BRANCH, HEAD_DIM)),
                  full((ncp, HEAD_DIM)), full((ncp, HEAD_DIM)), full((ncp, HEAD_DIM))],
        out_specs=(cspec, cspec_t),
        compiler_params=_cparams(("parallel", "parallel")),
        name="nsa_cmp_prep",
    )(kv_raw, kv_raw, cmp_pos, w1, cmp_w2.astype(BF16), k_norm, cc, cs1, cs2)

    cstart = jnp.arange(ncp)[:, None] * CMP_STRIDE
    sstart = jnp.arange(LANES)[None, :] * SEL_LEN
    overlap = jnp.maximum(jnp.minimum(cstart + CMP_LEN, sstart + SEL_LEN) - jnp.maximum(cstart, sstart), 0)
    overlap = jnp.where(jnp.arange(LANES)[None, :] < nsel, overlap, 0)
    overlap_t = (overlap.astype(F32) / CMP_STRIDE).astype(BF16).T

    tq = ATT_TQ
    nq = seq // tq
    qtab = pl.BlockSpec((tq, HEAD_DIM), lambda b, h, i: (i, 0))
    kvfull = lambda n: pl.BlockSpec((None, None, n, HEAD_DIM), lambda b, h, i: (b, h, 0, 0))
    kvfull_t = lambda n: pl.BlockSpec((None, None, HEAD_DIM, n), lambda b, h, i: (b, h, 0, 0))
    vfull_t = pl.BlockSpec((None, None, vrows, seq), lambda b, h, i: (b, h, 0, 0))
    return pl.pallas_call(
        functools.partial(_attn_body, tq=tq, ncp=ncp),
        out_shape=jax.ShapeDtypeStruct((m, HEADS * HEAD_DIM), BF16),
        grid=(batch, KV_HEADS, nq),
        in_specs=[pl.BlockSpec((tq, GROUP * HEAD_DIM), lambda b, h, i: (b * nq + i, h)),
                  pl.BlockSpec((tq, LANES), lambda b, h, i: (b * nq + i, h)),
                  qtab, qtab, qtab,
                  pl.BlockSpec((1, HEAD_DIM), lambda b, h, i: (0, 0)),
                  kvfull(ncp), kvfull_t(ncp), kvfull(seq), vfull_t, kvfull(seq), vfull_t,
                  pl.BlockSpec((LANES, ncp), lambda b, h, i: (0, 0))],
        out_specs=pl.BlockSpec((tq, GROUP * HEAD_DIM), lambda b, h, i: (b * nq + i, h)),
        scratch_shapes=[pltpu.VMEM((LANES, tq), F32), pltpu.VMEM((seq, GROUP * tq), F32)],
        compiler_params=_cparams(("parallel", "parallel", "arbitrary")),
        name="nsa_attn",
    )(q_raw, gate_raw, c, s1, s2, q_norm.reshape(1, HEAD_DIM), k_c, v_c, k_s, v_s, k_w, v_w, overlap_t)


def _router_body(x_ref, g_ref, whi_ref, wlo_ref, b_ref, h_ref, r_ref):
    x = x_ref[...]
    h = x * lax.rsqrt(jnp.mean(x * x, axis=-1, keepdims=True) + NORM_EPS) * g_ref[...]
    h_ref[...] = h
    h_hi = h.astype(BF16)
    h_lo = (h - h_hi.astype(F32)).astype(BF16)
    logits = (_dot(h_hi, whi_ref[...]) + _dot(h_lo, whi_ref[...]) + _dot(h_hi, wlo_ref[...])
              + b_ref[...])
    lane = lax.broadcasted_iota(jnp.int32, logits.shape, 1)
    lane_f = lane.astype(F32)
    none = float(LANES)
    is_g = lane < N_GROUPS
    glog = jnp.where(is_g, logits, -jnp.inf)
    gmax = jnp.max(glog, axis=-1, keepdims=True)
    gidx = jnp.min(jnp.where(glog == gmax, lane_f, none), axis=-1, keepdims=True)
    g_p = 1.0 / jnp.sum(jnp.where(is_g, jnp.exp(logits - gmax), 0.0), axis=-1, keepdims=True)
    e_lo = N_GROUPS + gidx * PER_GROUP
    elog = jnp.where((lane_f >= e_lo) & (lane_f < e_lo + PER_GROUP), logits, -jnp.inf)
    m1 = jnp.max(elog, axis=-1, keepdims=True)
    i1 = jnp.min(jnp.where(elog == m1, lane_f, none), axis=-1, keepdims=True)
    elog = jnp.where(lane_f == i1, -jnp.inf, elog)
    m2 = jnp.max(elog, axis=-1, keepdims=True)
    i2 = jnp.min(jnp.where(elog == m2, lane_f, none), axis=-1, keepdims=True)
    t = jnp.exp(m2 - m1)
    w1 = g_p / (1.0 + t)
    w2 = g_p * t / (1.0 + t)
    r_ref[...] = jnp.where(lane == 0, i1 - N_GROUPS,
                           jnp.where(lane == 1, i2 - N_GROUPS,
                                     jnp.where(lane == 2, w1, jnp.where(lane == 3, w2, 0.0))))


def moe_router(x, norm, rg_w, rg_b, re_w, re_b, tm=256):
    m, d = x.shape
    pad = LANES - N_GROUPS - N_EXPERTS
    wr = jnp.pad(jnp.concatenate([rg_w, re_w], axis=1).astype(F32), ((0, 0), (0, pad)))
    whi = wr.astype(BF16)
    wlo = (wr - whi.astype(F32)).astype(BF16)
    bias = jnp.pad(jnp.concatenate([rg_b, re_b]).astype(F32), (0, pad)).reshape(1, LANES)
    const = lambda shape: pl.BlockSpec(shape, lambda i: (0, 0))
    return pl.pallas_call(
        _router_body,
        out_shape=(jax.ShapeDtypeStruct((m, d), F32), jax.ShapeDtypeStruct((m, LANES), F32)),
        grid=(m // tm,),
        in_specs=[pl.BlockSpec((tm, d), lambda i: (i, 0)), const((1, d)),
                  const((d, LANES)), const((d, LANES)), const((1, LANES))],
        out_specs=(pl.BlockSpec((tm, d), lambda i: (i, 0)), pl.BlockSpec((tm, LANES), lambda i: (i, 0))),
        compiler_params=_cparams(("parallel",)),
        name="moe_router",
    )(x, norm.reshape(1, d), whi, wlo, bias)


def _rank_body(e_ref, rank_ref, cnt_ref, carry_ref):
    @pl.when(pl.program_id(0) == 0)
    def _():
        carry_ref[...] = jnp.zeros_like(carry_ref)

    e = e_ref[0]
    sub = lax.broadcasted_iota(jnp.int32, (N_EXPERTS, RANK_T), 0)
    onehot = jnp.where(sub == e, 1.0, 0.0)
    r = lax.broadcasted_iota(jnp.int32, (RANK_T, RANK_T), 0)
    c = lax.broadcasted_iota(jnp.int32, (RANK_T, RANK_T), 1)
    before = jnp.where(r < c, 1.0, 0.0).astype(BF16)
    prefix = _dot(onehot.astype(BF16), before)
    carry = carry_ref[:, 0:1]
    rank = jnp.sum(onehot * (prefix + carry), axis=0, keepdims=True)
    rank_ref[0] = rank.astype(jnp.int32)
    carry_ref[...] = carry_ref[...] + jnp.sum(onehot, axis=1, keepdims=True)
    cnt_ref[...] = carry_ref[...]


def moe_rank(flat_e):
    a = flat_e.shape[0]
    nt = a // RANK_T
    rank, cnt = pl.pallas_call(
        _rank_body,
        out_shape=(jax.ShapeDtypeStruct((nt, 1, RANK_T), jnp.int32),
                   jax.ShapeDtypeStruct((N_EXPERTS, LANES), F32)),
        grid=(nt,),
        in_specs=[pl.BlockSpec((1, 1, RANK_T), lambda i: (i, 0, 0))],
        out_specs=(pl.BlockSpec((1, 1, RANK_T), lambda i: (i, 0, 0)),
                   pl.BlockSpec((N_EXPERTS, LANES), lambda i: (0, 0))),
        scratch_shapes=[pltpu.VMEM((N_EXPERTS, LANES), F32)],
        compiler_params=_cparams(("arbitrary",)),
        name="moe_rank",
    )(flat_e.reshape(nt, 1, RANK_T))
    return rank.reshape(a), cnt[:, 0].astype(jnp.int32)


def _row_copy(src, s, dst, d, sem):
    return pltpu.make_async_copy(src.at[pl.ds(s, 1), :], dst.at[pl.ds(d, 1), :], sem)


def _scatter_body(dest_ref, h_ref, xs_in_ref, xs_ref, sem):
    del xs_in_ref
    base = pl.program_id(0) * (MOE_TT * TOP_K)

    def start(r, carry):
        for k in range(TOP_K):
            _row_copy(h_ref, r, xs_ref, dest_ref[base + r * TOP_K + k], sem).start(priority=k)
        return carry

    def wait(r, carry):
        for k in range(TOP_K):
            _row_copy(h_ref, 0, xs_ref, 0, sem).wait()
        return carry

    lax.fori_loop(0, MOE_TT, start, 0, unroll=DMA_UNROLL)
    lax.fori_loop(0, MOE_TT, wait, 0, unroll=DMA_UNROLL)


def moe_scatter(h, dest, xs_init):
    m, d = h.shape
    n_rows = xs_init.shape[0]
    return pl.pallas_call(
        _scatter_body,
        out_shape=jax.ShapeDtypeStruct((n_rows, d), h.dtype),
        grid_spec=pltpu.PrefetchScalarGridSpec(
            num_scalar_prefetch=1,
            grid=(m // MOE_TT,),
            in_specs=[pl.BlockSpec((MOE_TT, d), lambda i, dest: (i, 0)),
                      pl.BlockSpec(memory_space=pl.ANY)],
            out_specs=pl.BlockSpec(memory_space=pl.ANY),
            scratch_shapes=[pltpu.SemaphoreType.DMA(())]),
        input_output_aliases={2: 0},
        compiler_params=_cparams(("arbitrary",)),
        name="moe_scatter",
    )(dest, h, xs_init)


def _moe_mm_body(blk_e_ref, nblk_ref, x_ref, w1_ref, w3_ref, w2_ref, y_ref):
    del blk_e_ref
    used = pl.program_id(0) < nblk_ref[0]

    @pl.when(used)
    def _():
        xb = x_ref[...].astype(BF16)
        hb = (jax.nn.silu(_dot(xb, w1_ref[0])) * _dot(xb, w3_ref[0])).astype(BF16)
        y_ref[...] = _dot(hb, w2_ref[0])

    @pl.when(jnp.logical_not(used))
    def _():
        y_ref[...] = jnp.zeros_like(y_ref)


def moe_experts(xs, blk_e, nblk, w1, w3, w2):
    n_rows, d = xs.shape
    ff = w1.shape[2]
    wspec = lambda r, c: pl.BlockSpec((1, r, c), lambda j, be, nb: (be[j], 0, 0))
    return pl.pallas_call(
        _moe_mm_body,
        out_shape=jax.ShapeDtypeStruct((n_rows, d), F32),
        grid_spec=pltpu.PrefetchScalarGridSpec(
            num_scalar_prefetch=2,
            grid=(n_rows // MOE_ROWS,),
            in_specs=[pl.BlockSpec((MOE_ROWS, d), lambda j, be, nb: (jnp.minimum(j, nb[0] - 1), 0)),
                      wspec(d, ff), wspec(d, ff), wspec(ff, d)],
            out_specs=pl.BlockSpec((MOE_ROWS, d), lambda j, be, nb: (j, 0))),
        compiler_params=_cparams(("arbitrary",)),
        name="moe_experts",
    )(blk_e, nblk, xs, w1, w3, w2)


def _combine_body(dest_ref, x_ref, r_ref, g_ref, y_ref, xo_ref, h_ref, buf, sem):
    i = pl.program_id(0)
    n = pl.num_programs(0)

    def start_step(step, slot):
        base = step * (MOE_TT * TOP_K)

        def body(r, carry):
            for k in range(TOP_K):
                _row_copy(y_ref, dest_ref[base + r * TOP_K + k], buf.at[slot], k * MOE_TT + r,
                          sem.at[slot]).start(priority=k)
            return carry

        lax.fori_loop(0, MOE_TT, body, 0, unroll=DMA_UNROLL)

    @pl.when(i == 0)
    def _():
        start_step(0, 0)

    @pl.when(i + 1 < n)
    def _():
        start_step(i + 1, (i + 1) % 2)

    slot = i % 2

    def wait(r, carry):
        for k in range(TOP_K):
            _row_copy(y_ref, 0, buf.at[slot], 0, sem.at[slot]).wait()
        return carry

    lax.fori_loop(0, MOE_TT, wait, 0, unroll=DMA_UNROLL)
    r = r_ref[...]
    moe = r[:, 2:3] * buf[slot, 0:MOE_TT, :] + r[:, 3:4] * buf[slot, MOE_TT:2 * MOE_TT, :]
    x = x_ref[...] + moe
    xo_ref[...] = x
    hn = x * lax.rsqrt(jnp.mean(x * x, axis=-1, keepdims=True) + NORM_EPS) * g_ref[...]
    h_ref[...] = hn.astype(h_ref.dtype)


def moe_combine(x, route, y, dest, norm):
    m, d = x.shape
    tok = lambda c: pl.BlockSpec((MOE_TT, c), lambda i, dest: (i, 0))
    return pl.pallas_call(
        _combine_body,
        out_shape=(jax.ShapeDtypeStruct((m, d), F32), jax.ShapeDtypeStruct((m, d), BF16)),
        grid_spec=pltpu.PrefetchScalarGridSpec(
            num_scalar_prefetch=1,
            grid=(m // MOE_TT,),
            in_specs=[tok(d), tok(LANES), pl.BlockSpec((1, d), lambda i, dest: (0, 0)),
                      pl.BlockSpec(memory_space=pl.ANY)],
            out_specs=(tok(d), tok(d)),
            scratch_shapes=[pltpu.VMEM((2, TOP_K * MOE_TT, d), F32),
                            pltpu.SemaphoreType.DMA((2,))]),
        compiler_params=_cparams(("arbitrary",)),
        name="moe_combine",
    )(dest, x, route, norm.reshape(1, d), y)


def moe_ffn(x, layer, norm2, rg_w, rg_b, re_w, re_b, w1, w3, w2, norm3):
    m, d = x.shape
    h, route = moe_router(x, norm2, rg_w, rg_b, re_w, re_b)
    flat_e = route[:, :TOP_K].astype(jnp.int32).reshape(m * TOP_K)
    rank, counts = moe_rank(flat_e)
    padded = (counts + MOE_ROWS - 1) // MOE_ROWS * MOE_ROWS
    pend = jnp.cumsum(padded)
    pstart = pend - padded
    dest = (pstart[flat_e] + rank).astype(jnp.int32)
    n_blocks = (m * TOP_K + N_EXPERTS * (MOE_ROWS - 1) + MOE_ROWS - 1) // MOE_ROWS
    nblk = (pend[-1:] // MOE_ROWS).astype(jnp.int32)
    starts = jnp.arange(n_blocks) * MOE_ROWS
    blk_e = jnp.minimum(jnp.sum(pend[None, :] <= starts[:, None], axis=1), N_EXPERTS - 1).astype(jnp.int32)
    xs = moe_scatter(h, dest, jnp.zeros((n_blocks * MOE_ROWS, d), h.dtype))
    y = moe_experts(xs, blk_e, nblk, cast_bf16(w1, layer), cast_bf16(w3, layer), cast_bf16(w2, layer))
    return moe_combine(x, route, y, dest, norm3)


def kernel(x, p, norm1, w_in, q_norm, k_norm, cmp_pos, cmp_w1, cmp_w2, gm_norm, gm_ws, gm_bs,
           s5_lambda_re, s5_lambda_im, s5_log_dt, s5_b_re, s5_b_im, s5_c_re, s5_c_im, s5_d,
           s5_w_glu, s5_b_glu, w_branch, w_out, norm2, router_group_w, router_group_b,
           router_expert_w, router_expert_b, expert_w1, expert_w3, expert_w2, norm3, w_ple,
           w_ple_gate):
    b, l, d = x.shape
    m = b * l
    xf = x.reshape(m, d)
    q_cols = HEADS * HEAD_DIM
    kv_cols = N_BRANCH * 2 * KV_HEADS * HEAD_DIM
    ng_cols = HEADS * N_BRANCH
    splits = [0, q_cols, q_cols + kv_cols, q_cols + kv_cols + ng_cols]
    splits.append(splits[-1] + 2 * GM_WIDTH)
    splits.append(splits[-1] + S5_WIDTH)
    splits.append(splits[-1] + 3 * d)
    per_head = GROUP * N_BRANCH
    w_in_t = jnp.swapaxes(w_in, 1, 2)
    for i in range(p.shape[0]):
        h = rms_cast(xf, norm1[i])
        w_all = cast_transposed_drop_rows(w_in_t, i, splits[2], ng_cols)
        w_cut = w_in_t[i, splits[2]:splits[3], :].T
        widths = [q_cols, kv_cols, 2 * GM_WIDTH, S5_WIDTH, 3 * d]
        offs = [sum(widths[:k]) for k in range(len(widths))]
        proj = lambda k, name: mm_plain(h, w_all, col_off=offs[k], n=widths[k], name=name)
        q_raw = proj(0, "proj_q")
        kv_raw = proj(1, "proj_kv")
        gm_raw = proj(2, "proj_gm")
        s5_raw = proj(3, "proj_s5")
        mg_raw = proj(4, "proj_merge")
        w_ng = jnp.pad(w_cut[:, :ng_cols].reshape(d, KV_HEADS, per_head),
                       ((0, 0), (0, 0), (0, LANES - per_head))).reshape(d, KV_HEADS * LANES)
        ng_raw = mm_plain(h, w_ng.astype(BF16), name="proj_gate")

        o_gm = gmlp_mixer(gm_raw, gm_norm[i], gm_ws[i], gm_bs[i])
        o_s5 = s5_mixer(s5_raw, b, s5_lambda_re[i], s5_lambda_im[i], s5_log_dt[i], s5_b_re[i],
                        s5_b_im[i], s5_c_re[i], s5_c_im[i], s5_d[i], s5_w_glu[i], s5_b_glu[i])
        o_att = nsa_mixer(q_raw, kv_raw, ng_raw, b, q_norm[i], k_norm[i], cmp_pos[i], cmp_w1[i],
                          cmp_w2[i])
        wb = cast_bf16(w_branch, i)
        merged = mm_merge([o_gm, o_s5, o_att], [wb[0], wb[1], wb[2]], mg_raw)
        xf = mm_residual(merged, cast_bf16(w_out, i), xf)
        xf, h3 = moe_ffn(xf, i, norm2[i], router_group_w[i], router_group_b[i], router_expert_w[i],
                         router_expert_b[i], expert_w1, expert_w3, expert_w2, norm3[i])
        xf = mm_ple(h3, cast_bf16(w_ple_gate, i), p[i].reshape(m, -1).astype(BF16),
                    w_ple[i].astype(BF16), xf)
    return xf.reshape(b, l, d)
```

```python
import functools

import jax
import jax.numpy as jnp
from jax import lax
from jax.experimental import pallas as pl
from jax.experimental.pallas import tpu as pltpu

F32 = jnp.float32
BF16 = jnp.bfloat16

NORM_EPS = 1e-6
LANES = 128
VMEM_LIMIT = 52 * 1024 * 1024

HEADS = 16
KV_HEADS = 4
GROUP = HEADS // KV_HEADS
HEAD_DIM = 128
ROPE_THETA = 500000.0
ROPE_DIMS = HEAD_DIM // 4
ROPE_HALF = ROPE_DIMS // 2
CMP_LEN = 32
CMP_STRIDE = 16
SEL_LEN = 64
SEL_TOPN = 16
WINDOW = 512
N_BRANCH = 3
NEG = -1e30
ATT_SCALE = HEAD_DIM ** -0.5 * 1.4426950408889634
ATT_TQ = 128
ATT_TK = 1024
ONES_ROWS = 16

GM_WIDTH = 2048
GM_GROUPS = 16
GM_CHUNK = 128

S5_WIDTH = 2048
S5_GROUP_DIM = 16
S5_GROUPS = S5_WIDTH // S5_GROUP_DIM
S5_STATE = 64
S5_TILE_GROUPS = LANES // S5_GROUP_DIM
S5_TILE_STATES = S5_TILE_GROUPS * S5_STATE
S5_BLOCK = 8
S5_ROWS = 512

N_GROUPS = 4
PER_GROUP = 8
N_EXPERTS = N_GROUPS * PER_GROUP
TOP_K = 2
MOE_ROWS = 256
RANK_T = 512
MOE_TT = 128
DMA_UNROLL = 8


def _cparams(sem):
    return pltpu.CompilerParams(dimension_semantics=sem, vmem_limit_bytes=VMEM_LIMIT)


def _dot(a, b):
    return jnp.dot(a, b, preferred_element_type=F32)


def _dot_nt(a, b):
    return lax.dot_general(a, b, (((1,), (1,)), ((), ())), preferred_element_type=F32)


def _rms_body(x_ref, g_ref, o_ref):
    x = x_ref[...]
    y = x * lax.rsqrt(jnp.mean(x * x, axis=-1, keepdims=True) + NORM_EPS)
    o_ref[...] = (y * g_ref[...]).astype(o_ref.dtype)


def rms_cast(x, g, tm=256):
    m, d = x.shape
    return pl.pallas_call(
        _rms_body,
        out_shape=jax.ShapeDtypeStruct((m, d), BF16),
        grid=(m // tm,),
        in_specs=[pl.BlockSpec((tm, d), lambda i: (i, 0)),
                  pl.BlockSpec((1, d), lambda i: (0, 0))],
        out_specs=pl.BlockSpec((tm, d), lambda i: (i, 0)),
        compiler_params=_cparams(("parallel",)),
        name="rms_cast",
    )(x, g.reshape(1, d))


def _mm_plain_body(a_ref, w_ref, o_ref):
    o_ref[...] = _dot(a_ref[...], w_ref[...]).astype(o_ref.dtype)


def _mm_glu_body(a_ref, w_ref, y_ref, b_ref, o_ref):
    acc = _dot(a_ref[...].astype(BF16), w_ref[...]) + b_ref[...]
    o_ref[...] = (y_ref[...] * jax.nn.sigmoid(acc)).astype(o_ref.dtype)


def _mm_res_body(a_ref, w_ref, x_ref, o_ref):
    o_ref[...] = x_ref[...] + _dot(a_ref[...], w_ref[...])


def _mm_merge_body(a0, a1, a2, w0, w1, w2, g0, g1, g2, o_ref):
    acc = jax.nn.sigmoid(g0[...]) * _dot(a0[...], w0[...])
    acc = acc + jax.nn.sigmoid(g1[...]) * _dot(a1[...], w1[...])
    acc = acc + jax.nn.sigmoid(g2[...]) * _dot(a2[...], w2[...])
    o_ref[...] = acc.astype(o_ref.dtype)


def _mm_ple_body(a_ref, w_ref, p_ref, wp_ref, x_ref, o_ref):
    gate = jax.nn.sigmoid(_dot(a_ref[...], w_ref[...]))
    o_ref[...] = x_ref[...] + _dot(p_ref[...], wp_ref[...]) * gate


def _lhs_spec(tm, k):
    return pl.BlockSpec((tm, k), lambda j, i: (i, 0))


def _rhs_spec(k, tn):
    return pl.BlockSpec((k, tn), lambda j, i: (0, j))


def _tile_spec(tm, tn, col_blocks=0):
    return pl.BlockSpec((tm, tn), lambda j, i: (i, j + col_blocks))


def _mm_call(body, args, in_specs, m, n, tm, tn, out_dtype, name):
    return pl.pallas_call(
        body,
        out_shape=jax.ShapeDtypeStruct((m, n), out_dtype),
        grid=(n // tn, m // tm),
        in_specs=in_specs,
        out_specs=pl.BlockSpec((tm, tn), lambda j, i: (i, j)),
        compiler_params=_cparams(("parallel", "parallel")),
        name=name,
    )(*args)


def _pick(n, prefs):
    for t in prefs:
        if n % t == 0:
            return t
    return n


def mm_plain(a, w, col_off=0, n=None, out_dtype=F32, name="mm"):
    m, k = a.shape
    n = w.shape[1] if n is None else n
    tm, tn = _pick(m, (512, 256, 128)), _pick(n, (1024, 512, 256, 128))
    assert col_off % tn == 0
    rhs = pl.BlockSpec((k, tn), lambda j, i: (0, j + col_off // tn))
    return _mm_call(_mm_plain_body, (a, w), [_lhs_spec(tm, k), rhs], m, n, tm, tn, out_dtype, name)


def _drop_cols_body(x_ref, o_ref, cut_ref, *, cut, skip):
    x = x_ref[...]
    o_ref[:, :cut] = x[:, :cut].astype(o_ref.dtype)
    o_ref[:, cut:] = x[:, cut + skip:].astype(o_ref.dtype)
    cut_ref[...] = x[:, cut:cut + LANES]


def cast_drop_cols(w, layer, cut, skip, tr=64):
    _, rows, cols = w.shape
    return pl.pallas_call(
        functools.partial(_drop_cols_body, cut=cut, skip=skip),
        out_shape=(jax.ShapeDtypeStruct((rows, cols - skip), BF16),
                   jax.ShapeDtypeStruct((rows, LANES), F32)),
        grid=(rows // tr,),
        in_specs=[pl.BlockSpec((None, tr, cols), lambda i: (layer, i, 0))],
        out_specs=(pl.BlockSpec((tr, cols - skip), lambda i: (i, 0)),
                   pl.BlockSpec((tr, LANES), lambda i: (i, 0))),
        compiler_params=_cparams(("parallel",)),
        name="cast_drop_cols",
    )(w)


def _cast_t_body(x_ref, o_ref):
    o_ref[...] = x_ref[...].T.astype(o_ref.dtype)


def cast_transposed_drop_rows(wt, layer, cut, skip, tn=512):
    _, n, k = wt.shape
    assert cut % tn == 0 and (n - skip) % tn == 0
    assert n % 8 == 0 and skip % 8 == 0
    src_row = lambda j: pl.multiple_of(layer * n + j * tn + jnp.where(j * tn >= cut, skip, 0), 8)
    return pl.pallas_call(
        _cast_t_body,
        out_shape=jax.ShapeDtypeStruct((k, n - skip), BF16),
        grid=((n - skip) // tn,),
        in_specs=[pl.BlockSpec((pl.Element(tn), pl.Element(k)), lambda j: (src_row(j), 0))],
        out_specs=pl.BlockSpec((k, tn), lambda j: (0, j)),
        compiler_params=_cparams(("parallel",)),
        name="cast_transposed",
    )(wt.reshape(-1, k))


def take_rows(wt, layer, start, count):
    _, n, k = wt.shape
    assert n % 8 == 0 and start % 8 == 0 and count % 8 == 0
    return pl.pallas_call(
        _cast_body,
        out_shape=jax.ShapeDtypeStruct((count, k), wt.dtype),
        grid=(1,),
        in_specs=[pl.BlockSpec((pl.Element(count), pl.Element(k)),
                               lambda j: (layer * n + start, 0))],
        out_specs=pl.BlockSpec((count, k), lambda j: (0, 0)),
        compiler_params=_cparams(("arbitrary",)),
        name="take_rows",
    )(wt.reshape(-1, k))


def _cast_body(x_ref, o_ref):
    o_ref[...] = x_ref[...].astype(o_ref.dtype)


def cast_bf16(w, layer, block_bytes=8 * 1024 * 1024):
    cols = w.shape[-1]
    x = w.reshape(w.shape[0], -1, cols)
    rows = x.shape[1]
    tr = next(t for t in (4096, 2048, 1024, 512, 256, 128, 64, 32, 16)
              if rows % t == 0 and t * cols * 4 <= block_bytes)
    out = pl.pallas_call(
        _cast_body,
        out_shape=jax.ShapeDtypeStruct((rows, cols), BF16),
        grid=(rows // tr,),
        in_specs=[pl.BlockSpec((None, tr, cols), lambda i: (layer, i, 0))],
        out_specs=pl.BlockSpec((tr, cols), lambda i: (i, 0)),
        compiler_params=_cparams(("parallel",)),
        name="cast_bf16",
    )(x)
    return out.reshape(w.shape[1:])


def mm_glu(y, w, b):
    m, k = y.shape
    n = w.shape[1]
    tm, tn = _pick(m, (512, 256, 128)), _pick(n, (1024, 512, 256, 128))
    specs = [_lhs_spec(tm, k), _rhs_spec(k, tn), _tile_spec(tm, tn),
             pl.BlockSpec((1, tn), lambda j, i: (0, j))]
    return _mm_call(_mm_glu_body, (y, w, y, b.reshape(1, n)), specs, m, n, tm, tn, BF16, "mm_glu")


def mm_residual(a, w, x):
    m, k = a.shape
    n = w.shape[1]
    tm, tn = _pick(m, (512, 256, 128)), _pick(n, (1024, 512, 256, 128))
    specs = [_lhs_spec(tm, k), _rhs_spec(k, tn), _tile_spec(tm, tn)]
    return _mm_call(_mm_res_body, (a, w, x), specs, m, n, tm, tn, F32, "mm_residual")


def mm_merge(o_list, w_list, mg_raw):
    m, k = o_list[0].shape
    n = w_list[0].shape[1]
    tm, tn = _pick(m, (512, 256, 128)), _pick(n, (512, 256, 128))
    specs = ([_lhs_spec(tm, k)] * 3 + [_rhs_spec(k, tn)] * 3
             + [_tile_spec(tm, tn, c * (n // tn)) for c in range(3)])
    return _mm_call(_mm_merge_body, (*o_list, *w_list, mg_raw, mg_raw, mg_raw), specs,
                    m, n, tm, tn, BF16, "mm_merge")


def mm_ple(h, wg, p, wp, x):
    m, k = h.shape
    n = wg.shape[1]
    kp = p.shape[1]
    tm, tn = _pick(m, (512, 256, 128)), _pick(n, (1024, 512, 256, 128))
    specs = [_lhs_spec(tm, k), _rhs_spec(k, tn), _lhs_spec(tm, kp), _rhs_spec(kp, tn),
             _tile_spec(tm, tn)]
    return _mm_call(_mm_ple_body, (h, wg, p, wp, x), specs, m, n, tm, tn, F32, "mm_ple")


def _gmlp_body(z_ref, gn_ref, w_ref, b_ref, o_ref, *, rows):
    z = jax.nn.gelu(z_ref[...])
    u = z[:, :GM_WIDTH]
    v = z[:, GM_WIDTH:]
    v = v * lax.rsqrt(jnp.mean(v * v, axis=-1, keepdims=True) + NORM_EPS) * gn_ref[...]
    vb = v.astype(BF16)
    for c in range(rows // GM_CHUNK):
        r0 = c * GM_CHUNK
        for g in range(GM_GROUPS):
            c0 = g * LANES
            s = _dot(w_ref[g], vb[r0:r0 + GM_CHUNK, c0:c0 + LANES]) + b_ref[:, c0:c0 + LANES]
            o_ref[r0:r0 + GM_CHUNK, c0:c0 + LANES] = (
                u[r0:r0 + GM_CHUNK, c0:c0 + LANES] * s).astype(o_ref.dtype)


def gmlp_mixer(z, gm_norm, ws, bs, rows=256):
    m = z.shape[0]
    tri = jnp.tril(jnp.ones((GM_CHUNK, GM_CHUNK), dtype=bool))
    w = jnp.where(tri[None], ws, 0.0).astype(BF16)
    bias = jnp.repeat(bs.T, LANES, axis=1)
    return pl.pallas_call(
        functools.partial(_gmlp_body, rows=rows),
        out_shape=jax.ShapeDtypeStruct((m, GM_WIDTH), BF16),
        grid=(m // rows,),
        in_specs=[pl.BlockSpec((rows, 2 * GM_WIDTH), lambda i: (i, 0)),
                  pl.BlockSpec((1, GM_WIDTH), lambda i: (0, 0)),
                  pl.BlockSpec((GM_GROUPS, GM_CHUNK, GM_CHUNK), lambda i: (0, 0, 0)),
                  pl.BlockSpec((GM_CHUNK, GM_WIDTH), lambda i: (0, 0))],
        out_specs=pl.BlockSpec((rows, GM_WIDTH), lambda i: (i, 0)),
        compiler_params=_cparams(("parallel",)),
        name="gmlp",
    )(z, gm_norm.reshape(1, GM_WIDTH), w, bias)


def _shift_rows(x, s):
    t = x.shape[0]
    if s % 8 == 0:
        return jnp.concatenate([jnp.zeros((s, x.shape[1]), x.dtype), x[:t - s]], axis=0)
    rolled = pltpu.roll(x, s, 0)
    row = lax.broadcasted_iota(jnp.int32, x.shape, 0)
    return jnp.where(row >= s, rolled, 0.0)


def _group_diag(table, rows_per_group, lanes_per_group):
    full = jnp.concatenate([table] * S5_TILE_GROUPS, axis=0)
    rg = lax.shift_right_logical(lax.broadcasted_iota(jnp.int32, full.shape, 0),
                                 rows_per_group.bit_length() - 1)
    lg = lax.shift_right_logical(lax.broadcasted_iota(jnp.int32, full.shape, 1),
                                 lanes_per_group.bit_length() - 1)
    return jnp.where(rg == lg, full, 0.0).astype(BF16)


def _s5_body(u_ref, kt_ref, qt_ref, rt_ref, are_ref, aim_ref, d_ref, o_ref,
             w_ref, q_ref, r_ref, hre_ref, him_ref, *, rows, levels):
    tb, hg, ns = S5_BLOCK, S5_GROUP_DIM, S5_STATE

    @pl.when(pl.program_id(2) == 0)
    def _():
        hre_ref[...] = jnp.zeros_like(hre_ref)
        him_ref[...] = jnp.zeros_like(him_ref)
        lag_blocks = [_group_diag(kt_ref[0, d], hg, hg) for d in range(tb)]
        zero = jnp.zeros((LANES, LANES), BF16)
        for ip in range(tb):
            for i in range(tb):
                w_ref[ip * LANES:(ip + 1) * LANES, i * LANES:(i + 1) * LANES] = (
                    lag_blocks[i - ip] if i >= ip else zero)
        for ip in range(tb):
            for ri in range(2):
                q_ref[ip * LANES:(ip + 1) * LANES, ri * S5_TILE_STATES:(ri + 1) * S5_TILE_STATES] = (
                    _group_diag(qt_ref[0, ip, ri], hg, ns))
        for ri in range(2):
            for i in range(tb):
                r_ref[ri * S5_TILE_STATES:(ri + 1) * S5_TILE_STATES, i * LANES:(i + 1) * LANES] = (
                    _group_diag(rt_ref[0, ri, i], ns, hg))

    us = [u_ref[pl.ds(i, rows, stride=S5_BLOCK), :] for i in range(S5_BLOCK)]
    ucat = jnp.concatenate([x.astype(BF16) for x in us], axis=1)
    s = _dot(ucat, q_ref[...])
    xr = s[:, :S5_TILE_STATES]
    xi = s[:, S5_TILE_STATES:]
    are = are_ref[0]
    aim = aim_ref[0]
    hr = hre_ref[0:1, :]
    hi = him_ref[0:1, :]
    a_r = are[0:1, :]
    a_i = aim[0:1, :]
    first = lax.broadcasted_iota(jnp.int32, xr.shape, 0) == 0
    xr = xr + jnp.where(first, a_r * hr - a_i * hi, 0.0)
    xi = xi + jnp.where(first, a_r * hi + a_i * hr, 0.0)
    for k in range(levels):
        sh = 1 << k
        cr = are[k:k + 1, :]
        ci = aim[k:k + 1, :]
        sr = _shift_rows(xr, sh)
        si = _shift_rows(xi, sh)
        xr, xi = xr + cr * sr - ci * si, xi + cr * si + ci * sr
    hre_ref[0:1, :] = xr[rows - 1:rows, :]
    him_ref[0:1, :] = xi[rows - 1:rows, :]
    xpr = _shift_rows(xr, 1) + jnp.where(first, hr, 0.0)
    xpi = _shift_rows(xi, 1) + jnp.where(first, hi, 0.0)
    xprev = jnp.concatenate([xpr, xpi], axis=1).astype(BF16)
    y = _dot(ucat, w_ref[...]) + _dot(xprev, r_ref[...])
    d = d_ref[...]
    for i in range(S5_BLOCK):
        yi = y[:, i * LANES:(i + 1) * LANES] + d * us[i]
        o_ref[pl.ds(i, rows, stride=S5_BLOCK), :] = jax.nn.gelu(yi)


def _s5_discretize(lam_re, lam_im, log_dt, b_re, b_im):
    dt = jnp.exp(log_dt.astype(F32))[:, None]
    lre = jnp.minimum(lam_re.astype(F32), -1e-4)
    lim = lam_im.astype(F32)
    mag = jnp.exp(lre * dt)
    a_re = mag * jnp.cos(lim * dt)
    a_im = mag * jnp.sin(lim * dt)
    den = lre * lre + lim * lim
    nr = a_re - 1.0
    f_re = (nr * lre + a_im * lim) / den
    f_im = (a_im * lre - nr * lim) / den
    br = b_re.astype(F32)
    bi = b_im.astype(F32)
    bb_re = f_re[..., None] * br - f_im[..., None] * bi
    bb_im = f_re[..., None] * bi + f_im[..., None] * br
    return a_re, a_im, bb_re, bb_im


def _s5_block_weights(a_re, a_im, bb_re, bb_im, c_re, c_im, levels):
    hp = lax.Precision.HIGHEST
    ng, ns = a_re.shape
    tb, gl, hg = S5_BLOCK, S5_TILE_GROUPS, S5_GROUP_DIM
    nt = ng // gl
    pr, pi = [jnp.ones_like(a_re)], [jnp.zeros_like(a_re)]
    for _ in range(tb):
        r, i = pr[-1], pi[-1]
        pr.append(r * a_re - i * a_im)
        pi.append(r * a_im + i * a_re)
    pw_re, pw_im = jnp.stack(pr), jnp.stack(pi)
    ab_re = pw_re[:tb, :, :, None] * bb_re[None] - pw_im[:tb, :, :, None] * bb_im[None]
    ab_im = pw_re[:tb, :, :, None] * bb_im[None] + pw_im[:tb, :, :, None] * bb_re[None]
    cr, ci = c_re.astype(F32), c_im.astype(F32)
    kern = (jnp.einsum('gcp,dgpk->dgck', cr, ab_re, precision=hp)
            - jnp.einsum('gcp,dgpk->dgck', ci, ab_im, precision=hp))
    kt = kern.reshape(tb, nt, gl, hg, hg).transpose(1, 0, 4, 2, 3).reshape(nt, tb, hg, gl * hg)
    q = jnp.stack([ab_re[::-1], ab_im[::-1]], axis=1).reshape(tb, 2, nt, gl, ns, hg)
    qt = q.transpose(2, 0, 1, 5, 3, 4).reshape(nt, tb, 2, hg, gl * ns)
    nr, ni = pw_re[1:, :, None, :], pw_im[1:, :, None, :]
    r = jnp.stack([cr[None] * nr - ci[None] * ni, -(cr[None] * ni + ci[None] * nr)])
    r = r.reshape(2, tb, nt, gl, hg, ns)
    rt = r.transpose(2, 0, 1, 5, 3, 4).reshape(nt, 2, tb, ns, gl * hg)
    sr, si = [pw_re[tb].reshape(-1)], [pw_im[tb].reshape(-1)]
    for _ in range(15):
        x, y = sr[-1], si[-1]
        sr.append(x * x - y * y)
        si.append(2.0 * x * y)
    assert levels <= 16
    scan_re = jnp.stack(sr).reshape(16, nt, gl * ns).transpose(1, 0, 2)
    scan_im = jnp.stack(si).reshape(16, nt, gl * ns).transpose(1, 0, 2)
    return kt, qt, rt, scan_re, scan_im


def s5_mixer(u, batch, lam_re, lam_im, log_dt, b_re, b_im, c_re, c_im, d_skip, w_glu, b_glu):
    m = u.shape[0]
    seq = m // batch
    nt = S5_GROUPS // S5_TILE_GROUPS
    nblk = seq // S5_BLOCK
    rows = min(S5_ROWS, nblk)
    levels = rows.bit_length() - 1
    assert rows == 1 << levels and nblk % rows == 0
    a_re, a_im, bb_re, bb_im = _s5_discretize(lam_re, lam_im, log_dt, b_re, b_im)
    kt, qt, rt, scan_re, scan_im = _s5_block_weights(a_re, a_im, bb_re, bb_im, c_re, c_im, levels)

    nchunk = nblk // rows
    tok = rows * S5_BLOCK
    tspec = lambda shape: pl.BlockSpec((1,) + shape, lambda b, j, t: (j,) + (0,) * len(shape))
    kin = S5_BLOCK * LANES
    y = pl.pallas_call(
        functools.partial(_s5_body, rows=rows, levels=levels),
        out_shape=jax.ShapeDtypeStruct((m, S5_WIDTH), F32),
        grid=(batch, nt, nchunk),
        in_specs=[pl.BlockSpec((tok, LANES), lambda b, j, t: (b * nchunk + t, j)),
                  tspec(kt.shape[1:]), tspec(qt.shape[1:]), tspec(rt.shape[1:]),
                  tspec((16, S5_TILE_STATES)), tspec((16, S5_TILE_STATES)),
                  pl.BlockSpec((1, LANES), lambda b, j, t: (0, j))],
        out_specs=pl.BlockSpec((tok, LANES), lambda b, j, t: (b * nchunk + t, j)),
        scratch_shapes=[pltpu.VMEM((kin, kin), BF16), pltpu.VMEM((kin, 2 * S5_TILE_STATES), BF16),
                        pltpu.VMEM((2 * S5_TILE_STATES, kin), BF16),
                        pltpu.VMEM((8, S5_TILE_STATES), F32), pltpu.VMEM((8, S5_TILE_STATES), F32)],
        compiler_params=_cparams(("parallel", "parallel", "arbitrary")),
        name="s5_scan",
    )(u, kt, qt, rt, scan_re, scan_im, d_skip.reshape(1, S5_WIDTH))
    return mm_glu(y, w_glu.astype(BF16), b_glu)


def _rope_tables(pos):
    inv_freq = ROPE_THETA ** (-jnp.arange(ROPE_HALF, dtype=F32) / ROPE_HALF)
    ang = pos.astype(F32)[:, None] * inv_freq[None, :]
    cos, sin = jnp.cos(ang), jnp.sin(ang)
    n = pos.shape[0]
    rest = HEAD_DIM - ROPE_DIMS
    c = jnp.concatenate([cos, cos, jnp.ones((n, rest), F32)], axis=1)
    s1 = jnp.concatenate([-sin, jnp.zeros((n, HEAD_DIM - ROPE_HALF), F32)], axis=1)
    s2 = jnp.concatenate([jnp.zeros((n, ROPE_HALF), F32), sin, jnp.zeros((n, rest), F32)], axis=1)
    return c, s1, s2


def _norm_rope(x, gain, c, s1, s2):
    x = x * lax.rsqrt(jnp.mean(x * x, axis=-1, keepdims=True) + NORM_EPS) * gain
    return (x * c + pltpu.roll(x, HEAD_DIM - ROPE_HALF, 1) * s1 + pltpu.roll(x, ROPE_HALF, 1) * s2)


def _kv_prep_body(ks_ref, vs_ref, kw_ref, vw_ref, kn_ref, c_ref, s1_ref, s2_ref,
                  oks_ref, ovs_ref, okw_ref, ovw_ref):
    c, s1, s2 = c_ref[...], s1_ref[...], s2_ref[...]
    oks_ref[...] = _norm_rope(ks_ref[...], kn_ref[1:2, :], c, s1, s2).astype(BF16)
    okw_ref[...] = _norm_rope(kw_ref[...], kn_ref[2:3, :], c, s1, s2).astype(BF16)
    vt = vs_ref[...].T
    ones = jnp.ones((ONES_ROWS, vt.shape[1]), F32)
    ovs_ref[...] = jnp.concatenate([vt, ones], axis=0).astype(BF16)
    ovw_ref[...] = jnp.concatenate([vw_ref[...].T, ones], axis=0).astype(BF16)


def _cmp_prep_body(k_ref, v_ref, pos_ref, w1_ref, w2_ref, kn_ref, c_ref, s1_ref, s2_ref,
                   ok_ref, ov_ref, *, ncp):
    half = CMP_LEN // 2
    outs = []
    for which, t_ref in enumerate((k_ref, v_ref)):
        lo = jnp.zeros((ncp, HEAD_DIM), F32)
        hi = jnp.zeros((ncp, HEAD_DIM), F32)
        for j in range(half):
            tj = t_ref[pl.ds(j, ncp, stride=CMP_STRIDE), :]
            a = (tj + pos_ref[which, j:j + 1, :]).astype(BF16)
            b = (tj + pos_ref[which, half + j:half + j + 1, :]).astype(BF16)
            lo = lo + _dot(a, w1_ref[which, j])
            hi = hi + _dot(b, w1_ref[which, half + j])
        pre = lo + pltpu.roll(hi, ncp - 1, 0)
        outs.append(_dot(jax.nn.gelu(pre).astype(BF16), w2_ref[which]))
    valid = lax.broadcasted_iota(jnp.int32, (ncp, HEAD_DIM), 0) < ncp - 1
    kc = _norm_rope(outs[0], kn_ref[0:1, :], c_ref[...], s1_ref[...], s2_ref[...])
    ok_ref[...] = jnp.where(valid, kc, 0.0).astype(BF16)
    ov_ref[...] = jnp.where(valid, outs[1], 0.0).T.astype(BF16)


def _attn_body(q_ref, gate_ref, c_ref, s1_ref, s2_ref, qn_ref, kc_ref, vct_ref,
               ks_ref, vst_ref, kw_ref, vwt_ref, ovt_ref, o_ref, bias_ref, s_ref, *, tq, ncp):
    s0 = pl.program_id(2) * tq
    cols = GROUP * tq
    qpos = s0 + (lax.broadcasted_iota(jnp.int32, (1, cols), 1) & (tq - 1))
    qpos1 = qpos[:, :tq]
    c, s1, s2 = c_ref[...], s1_ref[...], s2_ref[...]
    qn = qn_ref[...]
    q_t = jnp.concatenate(
        [(_norm_rope(q_ref[:, g * HEAD_DIM:(g + 1) * HEAD_DIM], qn, c, s1, s2) * ATT_SCALE).T.astype(BF16)
         for g in range(GROUP)], axis=1)

    n_id = lax.broadcasted_iota(jnp.int32, (ncp, 1), 0)
    cbias = jnp.where(n_id * CMP_STRIDE + (CMP_LEN - 1) <= qpos1, 0.0, NEG)
    sc = _dot(kc_ref[...], q_t) + jnp.concatenate([cbias] * GROUP, axis=1)
    mx = jnp.max(sc, axis=0, keepdims=True)
    e = jnp.exp2(sc - mx)
    inv = jnp.where(qpos >= CMP_LEN - 1, 1.0 / jnp.sum(e, axis=0, keepdims=True), 0.0)
    p = e * inv
    o_cmp = _dot(vct_ref[...], p.astype(BF16))

    span = WINDOW + tq
    w0 = pl.multiple_of(jnp.maximum(s0 - WINDOW, 0), tq)
    kp = w0 + lax.broadcasted_iota(jnp.int32, (span, 1), 0)
    wbias = jnp.where((kp <= qpos1) & (kp > qpos1 - WINDOW), 0.0, NEG)
    sw = _dot(kw_ref[pl.ds(w0, span), :], q_t) + jnp.concatenate([wbias] * GROUP, axis=1)
    ew = jnp.exp2(sw - jnp.max(sw, axis=0, keepdims=True))
    acc_win = _dot(vwt_ref[:, pl.ds(w0, span)], ew.astype(BF16))
    o_win = acc_win[:HEAD_DIM] / acc_win[HEAD_DIM:HEAD_DIM + 1]

    pg = p[:, 0:tq]
    for g in range(1, GROUP):
        pg = pg + p[:, g * tq:(g + 1) * tq]
    p_hi = pg.astype(BF16)
    p_lo = (pg - p_hi.astype(F32)).astype(BF16)
    imp = _dot(ovt_ref[...], p_hi) + _dot(ovt_ref[...], p_lo)
    blk = lax.broadcasted_iota(jnp.int32, (LANES, tq), 0)
    cur = lax.shift_right_logical(qpos1, 6)
    forced = (blk == 0) | (blk == cur) | (blk == cur - 1)
    val = jnp.where(forced, -jnp.inf, jnp.where(blk <= cur, imp, -1e9))
    blk_f = blk.astype(F32)
    sel = jnp.where(forced, 1.0, 0.0)
    for _ in range(SEL_TOPN - 3):
        top = jnp.max(val, axis=0, keepdims=True)
        idx = jnp.min(jnp.where(val == top, blk_f, float(LANES)), axis=0, keepdims=True)
        hit = blk_f == idx
        val = jnp.where(hit, -jnp.inf, val)
        sel = jnp.where(hit, 1.0, sel)
    bias_ref[...] = (sel - 1.0) * (-NEG)

    blocks_per_tile = ATT_TK // SEL_LEN

    def score_tile(kt, m8, diagonal):
        k0 = pl.multiple_of(kt * ATT_TK, ATT_TK)
        s = _dot(ks_ref[pl.ds(k0, ATT_TK), :], q_t)
        b8 = bias_ref[pl.ds(pl.multiple_of(kt * blocks_per_tile, blocks_per_tile), blocks_per_tile), :]
        bias = jnp.concatenate([jnp.broadcast_to(b8[j:j + 1, :], (SEL_LEN, tq))
                                for j in range(blocks_per_tile)], axis=0)
        if diagonal:
            kpos = k0 + lax.broadcasted_iota(jnp.int32, (ATT_TK, 1), 0)
            bias = jnp.where(kpos <= qpos1, bias, NEG)
        s = s + jnp.concatenate([bias] * GROUP, axis=1)
        s_ref[pl.ds(k0, ATT_TK), :] = s
        return jnp.maximum(m8, jnp.max(s.reshape(ATT_TK // 8, 8, cols), axis=0))

    n_full = lax.div(s0, ATT_TK)
    m8 = lax.fori_loop(0, n_full, lambda kt, mm: score_tile(kt, mm, False), jnp.full((8, cols), NEG, F32))
    m_sel = jnp.max(score_tile(n_full, m8, True), axis=0, keepdims=True)

    def value_tile(kt, acc):
        k0 = pl.multiple_of(kt * ATT_TK, ATT_TK)
        pe = jnp.exp2(s_ref[pl.ds(k0, ATT_TK), :] - m_sel).astype(BF16)
        return acc + _dot(vst_ref[:, pl.ds(k0, ATT_TK)], pe)

    acc_sel = lax.fori_loop(0, n_full + 1, value_tile, jnp.zeros((HEAD_DIM + ONES_ROWS, cols), F32))
    o_sel = acc_sel[:HEAD_DIM] / acc_sel[HEAD_DIM:HEAD_DIM + 1]

    gates = jax.nn.sigmoid(gate_ref[...]).T
    for g in range(GROUP):
        cs = slice(g * tq, (g + 1) * tq)
        r = g * N_BRANCH
        out = (gates[r:r + 1, :] * o_cmp[:, cs] + gates[r + 1:r + 2, :] * o_sel[:, cs]
               + gates[r + 2:r + 3, :] * o_win[:, cs])
        o_ref[:, g * HEAD_DIM:(g + 1) * HEAD_DIM] = out.T.astype(o_ref.dtype)


def nsa_mixer(q_raw, kv_raw, gate_raw, batch, q_norm, k_norm, cmp_pos, cmp_w1, cmp_w2):
    m = q_raw.shape[0]
    seq = m // batch
    ncp = seq // CMP_STRIDE
    nsel = seq // SEL_LEN
    assert seq % ATT_TK == 0 and nsel <= LANES and seq >= WINDOW + ATT_TQ
    pos = jnp.arange(seq)
    c, s1, s2 = _rope_tables(pos)
    cc, cs1, cs2 = _rope_tables(jnp.arange(ncp) * CMP_STRIDE + CMP_LEN - 1)
    col = lambda branch, kv: (branch * 2 + kv) * KV_HEADS

    tl = _pick(seq, (1024, 512))
    nl = seq // tl
    kvspec = lambda base: pl.BlockSpec((tl, HEAD_DIM), lambda b, h, t: (b * nl + t, base + h))
    tab = pl.BlockSpec((tl, HEAD_DIM), lambda b, h, t: (t, 0))
    outspec = pl.BlockSpec((None, None, tl, HEAD_DIM), lambda b, h, t: (b, h, t, 0))
    outspec_t = pl.BlockSpec((None, None, HEAD_DIM, tl), lambda b, h, t: (b, h, 0, t))
    kv_shape = jax.ShapeDtypeStruct((batch, KV_HEADS, seq, HEAD_DIM), BF16)
    kv_shape_t = jax.ShapeDtypeStruct((batch, KV_HEADS, HEAD_DIM, seq), BF16)
    vrows = HEAD_DIM + ONES_ROWS
    outspec_t1 = pl.BlockSpec((None, None, vrows, tl), lambda b, h, t: (b, h, 0, t))
    kv_shape_t1 = jax.ShapeDtypeStruct((batch, KV_HEADS, vrows, seq), BF16)
    k_s, v_s, k_w, v_w = pl.pallas_call(
        _kv_prep_body,
        out_shape=(kv_shape, kv_shape_t1, kv_shape, kv_shape_t1),
        grid=(batch, KV_HEADS, nl),
        in_specs=[kvspec(col(1, 0)), kvspec(col(1, 1)), kvspec(col(2, 0)), kvspec(col(2, 1)),
                  pl.BlockSpec((N_BRANCH, HEAD_DIM), lambda b, h, t: (0, 0)), tab, tab, tab],
        out_specs=(outspec, outspec_t1, outspec, outspec_t1),
        compiler_params=_cparams(("parallel", "parallel", "parallel")),
        name="nsa_kv_prep",
    )(kv_raw, kv_raw, kv_raw, kv_raw, k_norm, c, s1, s2)

    w1 = cmp_w1.reshape(2, CMP_LEN, HEAD_DIM, HEAD_DIM).astype(BF16)
    full = lambda shape: pl.BlockSpec(shape, lambda b, h: (0,) * len(shape))
    cshape = jax.ShapeDtypeStruct((batch, KV_HEADS, ncp, HEAD_DIM), BF16)
    cspec = pl.BlockSpec((None, None, ncp, HEAD_DIM), lambda b, h: (b, h, 0, 0))
    cshape_t = jax.ShapeDtypeStruct((batch, KV_HEADS, HEAD_DIM, ncp), BF16)
    cspec_t = pl.BlockSpec((None, None, HEAD_DIM, ncp), lambda b, h: (b, h, 0, 0))
    k_c, v_c = pl.pallas_call(
        functools.partial(_cmp_prep_body, ncp=ncp),
        out_shape=(cshape, cshape_t),
        grid=(batch, KV_HEADS),
        in_specs=[pl.BlockSpec((seq, HEAD_DIM), lambda b, h: (b, col(0, 0) + h)),
                  pl.BlockSpec((seq, HEAD_DIM), lambda b, h: (b, col(0, 1) + h)),
                  full((2, CMP_LEN, HEAD_DIM)), full((2, CMP_LEN, HEAD_DIM, HEAD_DIM)),
                  full((2, HEAD_DIM, HEAD_DIM)), full((N_BRANCH, HEAD_DIM)),
                  full((ncp, HEAD_DIM)), full((ncp, HEAD_DIM)), full((ncp, HEAD_DIM))],
        out_specs=(cspec, cspec_t),
        compiler_params=_cparams(("parallel", "parallel")),
        name="nsa_cmp_prep",
    )(kv_raw, kv_raw, cmp_pos, w1, cmp_w2.astype(BF16), k_norm, cc, cs1, cs2)

    cstart = jnp.arange(ncp)[:, None] * CMP_STRIDE
    sstart = jnp.arange(LANES)[None, :] * SEL_LEN
    overlap = jnp.maximum(jnp.minimum(cstart + CMP_LEN, sstart + SEL_LEN) - jnp.maximum(cstart, sstart), 0)
    overlap = jnp.where(jnp.arange(LANES)[None, :] < nsel, overlap, 0)
    overlap_t = (overlap.astype(F32) / CMP_STRIDE).astype(BF16).T

    tq = ATT_TQ
    nq = seq // tq
    qtab = pl.BlockSpec((tq, HEAD_DIM), lambda b, h, i: (i, 0))
    kvfull = lambda n: pl.BlockSpec((None, None, n, HEAD_DIM), lambda b, h, i: (b, h, 0, 0))
    kvfull_t = lambda n: pl.BlockSpec((None, None, HEAD_DIM, n), lambda b, h, i: (b, h, 0, 0))
    vfull_t = pl.BlockSpec((None, None, vrows, seq), lambda b, h, i: (b, h, 0, 0))
    return pl.pallas_call(
        functools.partial(_attn_body, tq=tq, ncp=ncp),
        out_shape=jax.ShapeDtypeStruct((m, HEADS * HEAD_DIM), BF16),
        grid=(batch, KV_HEADS, nq),
        in_specs=[pl.BlockSpec((tq, GROUP * HEAD_DIM), lambda b, h, i: (b * nq + i, h)),
                  pl.BlockSpec((tq, LANES), lambda b, h, i: (b * nq + i, h)),
                  qtab, qtab, qtab,
                  pl.BlockSpec((1, HEAD_DIM), lambda b, h, i: (0, 0)),
                  kvfull(ncp), kvfull_t(ncp), kvfull(seq), vfull_t, kvfull(seq), vfull_t,
                  pl.BlockSpec((LANES, ncp), lambda b, h, i: (0, 0))],
        out_specs=pl.BlockSpec((tq, GROUP * HEAD_DIM), lambda b, h, i: (b * nq + i, h)),
        scratch_shapes=[pltpu.VMEM((LANES, tq), F32), pltpu.VMEM((seq, GROUP * tq), F32)],
        compiler_params=_cparams(("parallel", "parallel", "arbitrary")),
        name="nsa_attn",
    )(q_raw, gate_raw, c, s1, s2, q_norm.reshape(1, HEAD_DIM), k_c, v_c, k_s, v_s, k_w, v_w, overlap_t)


def _router_body(x_ref, g_ref, whi_ref, wlo_ref, b_ref, h_ref, r_ref):
    x = x_ref[...]
    h = x * lax.rsqrt(jnp.mean(x * x, axis=-1, keepdims=True) + NORM_EPS) * g_ref[...]
    h_ref[...] = h
    h_hi = h.astype(BF16)
    h_lo = (h - h_hi.astype(F32)).astype(BF16)
    logits = (_dot(h_hi, whi_ref[...]) + _dot(h_lo, whi_ref[...]) + _dot(h_hi, wlo_ref[...])
              + b_ref[...])
    lane = lax.broadcasted_iota(jnp.int32, logits.shape, 1)
    lane_f = lane.astype(F32)
    none = float(LANES)
    is_g = lane < N_GROUPS
    glog = jnp.where(is_g, logits, -jnp.inf)
    gmax = jnp.max(glog, axis=-1, keepdims=True)
    gidx = jnp.min(jnp.where(glog == gmax, lane_f, none), axis=-1, keepdims=True)
    g_p = 1.0 / jnp.sum(jnp.where(is_g, jnp.exp(logits - gmax), 0.0), axis=-1, keepdims=True)
    e_lo = N_GROUPS + gidx * PER_GROUP
    elog = jnp.where((lane_f >= e_lo) & (lane_f < e_lo + PER_GROUP), logits, -jnp.inf)
    m1 = jnp.max(elog, axis=-1, keepdims=True)
    i1 = jnp.min(jnp.where(elog == m1, lane_f, none), axis=-1, keepdims=True)
    elog = jnp.where(lane_f == i1, -jnp.inf, elog)
    m2 = jnp.max(elog, axis=-1, keepdims=True)
    i2 = jnp.min(jnp.where(elog == m2, lane_f, none), axis=-1, keepdims=True)
    t = jnp.exp(m2 - m1)
    w1 = g_p / (1.0 + t)
    w2 = g_p * t / (1.0 + t)
    r_ref[...] = jnp.where(lane == 0, i1 - N_GROUPS,
                           jnp.where(lane == 1, i2 - N_GROUPS,
                                     jnp.where(lane == 2, w1, jnp.where(lane == 3, w2, 0.0))))


def moe_router(x, norm, rg_w, rg_b, re_w, re_b, tm=256):
    m, d = x.shape
    pad = LANES - N_GROUPS - N_EXPERTS
    wr = jnp.pad(jnp.concatenate([rg_w, re_w], axis=1).astype(F32), ((0, 0), (0, pad)))
    whi = wr.astype(BF16)
    wlo = (wr - whi.astype(F32)).astype(BF16)
    bias = jnp.pad(jnp.concatenate([rg_b, re_b]).astype(F32), (0, pad)).reshape(1, LANES)
    const = lambda shape: pl.BlockSpec(shape, lambda i: (0, 0))
    return pl.pallas_call(
        _router_body,
        out_shape=(jax.ShapeDtypeStruct((m, d), F32), jax.ShapeDtypeStruct((m, LANES), F32)),
        grid=(m // tm,),
        in_specs=[pl.BlockSpec((tm, d), lambda i: (i, 0)), const((1, d)),
                  const((d, LANES)), const((d, LANES)), const((1, LANES))],
        out_specs=(pl.BlockSpec((tm, d), lambda i: (i, 0)), pl.BlockSpec((tm, LANES), lambda i: (i, 0))),
        compiler_params=_cparams(("parallel",)),
        name="moe_router",
    )(x, norm.reshape(1, d), whi, wlo, bias)


def _rank_body(e_ref, rank_ref, cnt_ref, carry_ref):
    @pl.when(pl.program_id(0) == 0)
    def _():
        carry_ref[...] = jnp.zeros_like(carry_ref)

    e = e_ref[0]
    sub = lax.broadcasted_iota(jnp.int32, (N_EXPERTS, RANK_T), 0)
    onehot = jnp.where(sub == e, 1.0, 0.0)
    r = lax.broadcasted_iota(jnp.int32, (RANK_T, RANK_T), 0)
    c = lax.broadcasted_iota(jnp.int32, (RANK_T, RANK_T), 1)
    before = jnp.where(r < c, 1.0, 0.0).astype(BF16)
    prefix = _dot(onehot.astype(BF16), before)
    carry = carry_ref[:, 0:1]
    rank = jnp.sum(onehot * (prefix + carry), axis=0, keepdims=True)
    rank_ref[0] = rank.astype(jnp.int32)
    carry_ref[...] = carry_ref[...] + jnp.sum(onehot, axis=1, keepdims=True)
    cnt_ref[...] = carry_ref[...]


def moe_rank(flat_e):
    a = flat_e.shape[0]
    nt = a // RANK_T
    rank, cnt = pl.pallas_call(
        _rank_body,
        out_shape=(jax.ShapeDtypeStruct((nt, 1, RANK_T), jnp.int32),
                   jax.ShapeDtypeStruct((N_EXPERTS, LANES), F32)),
        grid=(nt,),
        in_specs=[pl.BlockSpec((1, 1, RANK_T), lambda i: (i, 0, 0))],
        out_specs=(pl.BlockSpec((1, 1, RANK_T), lambda i: (i, 0, 0)),
                   pl.BlockSpec((N_EXPERTS, LANES), lambda i: (0, 0))),
        scratch_shapes=[pltpu.VMEM((N_EXPERTS, LANES), F32)],
        compiler_params=_cparams(("arbitrary",)),
        name="moe_rank",
    )(flat_e.reshape(nt, 1, RANK_T))
    return rank.reshape(a), cnt[:, 0].astype(jnp.int32)


def _row_copy(src, s, dst, d, sem):
    return pltpu.make_async_copy(src.at[pl.ds(s, 1), :], dst.at[pl.ds(d, 1), :], sem)


def _scatter_body(dest_ref, h_ref, xs_in_ref, xs_ref, sem):
    del xs_in_ref
    base = pl.program_id(0) * (MOE_TT * TOP_K)

    def start(r, carry):
        for k in range(TOP_K):
            _row_copy(h_ref, r, xs_ref, dest_ref[base + r * TOP_K + k], sem).start(priority=k)
        return carry

    def wait(r, carry):
        for k in range(TOP_K):
            _row_copy(h_ref, 0, xs_ref, 0, sem).wait()
        return carry

    lax.fori_loop(0, MOE_TT, start, 0, unroll=DMA_UNROLL)
    lax.fori_loop(0, MOE_TT, wait, 0, unroll=DMA_UNROLL)


def moe_scatter(h, dest, xs_init):
    m, d = h.shape
    n_rows = xs_init.shape[0]
    return pl.pallas_call(
        _scatter_body,
        out_shape=jax.ShapeDtypeStruct((n_rows, d), h.dtype),
        grid_spec=pltpu.PrefetchScalarGridSpec(
            num_scalar_prefetch=1,
            grid=(m // MOE_TT,),
            in_specs=[pl.BlockSpec((MOE_TT, d), lambda i, dest: (i, 0)),
                      pl.BlockSpec(memory_space=pl.ANY)],
            out_specs=pl.BlockSpec(memory_space=pl.ANY),
            scratch_shapes=[pltpu.SemaphoreType.DMA(())]),
        input_output_aliases={2: 0},
        compiler_params=_cparams(("arbitrary",)),
        name="moe_scatter",
    )(dest, h, xs_init)


def _moe_mm_body(blk_e_ref, nblk_ref, x_ref, w1_ref, w3_ref, w2_ref, y_ref):
    del blk_e_ref
    used = pl.program_id(0) < nblk_ref[0]

    @pl.when(used)
    def _():
        xb = x_ref[...].astype(BF16)
        hb = (jax.nn.silu(_dot(xb, w1_ref[0])) * _dot(xb, w3_ref[0])).astype(BF16)
        y_ref[...] = _dot(hb, w2_ref[0])

    @pl.when(jnp.logical_not(used))
    def _():
        y_ref[...] = jnp.zeros_like(y_ref)


def moe_experts(xs, blk_e, nblk, w1, w3, w2):
    n_rows, d = xs.shape
    ff = w1.shape[2]
    wspec = lambda r, c: pl.BlockSpec((1, r, c), lambda j, be, nb: (be[j], 0, 0))
    return pl.pallas_call(
        _moe_mm_body,
        out_shape=jax.ShapeDtypeStruct((n_rows, d), F32),
        grid_spec=pltpu.PrefetchScalarGridSpec(
            num_scalar_prefetch=2,
            grid=(n_rows // MOE_ROWS,),
            in_specs=[pl.BlockSpec((MOE_ROWS, d), lambda j, be, nb: (jnp.minimum(j, nb[0] - 1), 0)),
                      wspec(d, ff), wspec(d, ff), wspec(ff, d)],
            out_specs=pl.BlockSpec((MOE_ROWS, d), lambda j, be, nb: (j, 0))),
        compiler_params=_cparams(("arbitrary",)),
        name="moe_experts",
    )(blk_e, nblk, xs, w1, w3, w2)


def _combine_body(dest_ref, x_ref, r_ref, g_ref, y_ref, xo_ref, h_ref, buf, sem):
    i = pl.program_id(0)
    n = pl.num_programs(0)

    def start_step(step, slot):
        base = step * (MOE_TT * TOP_K)

        def body(r, carry):
            for k in range(TOP_K):
                _row_copy(y_ref, dest_ref[base + r * TOP_K + k], buf.at[slot], k * MOE_TT + r,
                          sem.at[slot]).start(priority=k)
            return carry

        lax.fori_loop(0, MOE_TT, body, 0, unroll=DMA_UNROLL)

    @pl.when(i == 0)
    def _():
        start_step(0, 0)

    @pl.when(i + 1 < n)
    def _():
        start_step(i + 1, (i + 1) % 2)

    slot = i % 2

    def wait(r, carry):
        for k in range(TOP_K):
            _row_copy(y_ref, 0, buf.at[slot], 0, sem.at[slot]).wait()
        return carry

    lax.fori_loop(0, MOE_TT, wait, 0, unroll=DMA_UNROLL)
    r = r_ref[...]
    moe = r[:, 2:3] * buf[slot, 0:MOE_TT, :] + r[:, 3:4] * buf[slot, MOE_TT:2 * MOE_TT, :]
    x = x_ref[...] + moe
    xo_ref[...] = x
    hn = x * lax.rsqrt(jnp.mean(x * x, axis=-1, keepdims=True) + NORM_EPS) * g_ref[...]
    h_ref[...] = hn.astype(h_ref.dtype)


def moe_combine(x, route, y, dest, norm):
    m, d = x.shape
    tok = lambda c: pl.BlockSpec((MOE_TT, c), lambda i, dest: (i, 0))
    return pl.pallas_call(
        _combine_body,
        out_shape=(jax.ShapeDtypeStruct((m, d), F32), jax.ShapeDtypeStruct((m, d), BF16)),
        grid_spec=pltpu.PrefetchScalarGridSpec(
            num_scalar_prefetch=1,
            grid=(m // MOE_TT,),
            in_specs=[tok(d), tok(LANES), pl.BlockSpec((1, d), lambda i, dest: (0, 0)),
                      pl.BlockSpec(memory_space=pl.ANY)],
            out_specs=(tok(d), tok(d)),
            scratch_shapes=[pltpu.VMEM((2, TOP_K * MOE_TT, d), F32),
                            pltpu.SemaphoreType.DMA((2,))]),
        compiler_params=_cparams(("arbitrary",)),
        name="moe_combine",
    )(dest, x, route, norm.reshape(1, d), y)


def moe_ffn(x, layer, norm2, rg_w, rg_b, re_w, re_b, w1, w3, w2, norm3):
    m, d = x.shape
    h, route = moe_router(x, norm2, rg_w, rg_b, re_w, re_b)
    flat_e = route[:, :TOP_K].astype(jnp.int32).reshape(m * TOP_K)
    rank, counts = moe_rank(flat_e)
    padded = (counts + MOE_ROWS - 1) // MOE_ROWS * MOE_ROWS
    pend = jnp.cumsum(padded)
    pstart = pend - padded
    dest = (pstart[flat_e] + rank).astype(jnp.int32)
    n_blocks = (m * TOP_K + N_EXPERTS * (MOE_ROWS - 1) + MOE_ROWS - 1) // MOE_ROWS
    nblk = (pend[-1:] // MOE_ROWS).astype(jnp.int32)
    starts = jnp.arange(n_blocks) * MOE_ROWS
    blk_e = jnp.minimum(jnp.sum(pend[None, :] <= starts[:, None], axis=1), N_EXPERTS - 1).astype(jnp.int32)
    xs = moe_scatter(h, dest, jnp.zeros((n_blocks * MOE_ROWS, d), h.dtype))
    y = moe_experts(xs, blk_e, nblk, cast_bf16(w1, layer), cast_bf16(w3, layer), cast_bf16(w2, layer))
    return moe_combine(x, route, y, dest, norm3)


def kernel(x, p, norm1, w_in, q_norm, k_norm, cmp_pos, cmp_w1, cmp_w2, gm_norm, gm_ws, gm_bs,
           s5_lambda_re, s5_lambda_im, s5_log_dt, s5_b_re, s5_b_im, s5_c_re, s5_c_im, s5_d,
           s5_w_glu, s5_b_glu, w_branch, w_out, norm2, router_group_w, router_group_b,
           router_expert_w, router_expert_b, expert_w1, expert_w3, expert_w2, norm3, w_ple,
           w_ple_gate):
    b, l, d = x.shape
    m = b * l
    xf = x.reshape(m, d)
    q_cols = HEADS * HEAD_DIM
    kv_cols = N_BRANCH * 2 * KV_HEADS * HEAD_DIM
    ng_cols = HEADS * N_BRANCH
    splits = [0, q_cols, q_cols + kv_cols, q_cols + kv_cols + ng_cols]
    splits.append(splits[-1] + 2 * GM_WIDTH)
    splits.append(splits[-1] + S5_WIDTH)
    splits.append(splits[-1] + 3 * d)
    per_head = GROUP * N_BRANCH
    w_in_t = jnp.swapaxes(w_in, 1, 2)
    for i in range(p.shape[0]):
        h = rms_cast(xf, norm1[i])
        w_all = cast_transposed_drop_rows(w_in_t, i, splits[2], ng_cols)
        w_cut = take_rows(w_in_t, i, splits[2], ng_cols).T
        widths = [q_cols, kv_cols, 2 * GM_WIDTH, S5_WIDTH, 3 * d]
        offs = [sum(widths[:k]) for k in range(len(widths))]
        proj = lambda k, name: mm_plain(h, w_all, col_off=offs[k], n=widths[k], name=name)
        q_raw = proj(0, "proj_q")
        kv_raw = proj(1, "proj_kv")
        gm_raw = proj(2, "proj_gm")
        s5_raw = proj(3, "proj_s5")
        mg_raw = proj(4, "proj_merge")
        w_ng = jnp.pad(w_cut[:, :ng_cols].reshape(d, KV_HEADS, per_head),
                       ((0, 0), (0, 0), (0, LANES - per_head))).reshape(d, KV_HEADS * LANES)
        ng_raw = mm_plain(h, w_ng.astype(BF16), name="proj_gate")

        o_gm = gmlp_mixer(gm_raw, gm_norm[i], gm_ws[i], gm_bs[i])
        o_s5 = s5_mixer(s5_raw, b, s5_lambda_re[i], s5_lambda_im[i], s5_log_dt[i], s5_b_re[i],
                        s5_b_im[i], s5_c_re[i], s5_c_im[i], s5_d[i], s5_w_glu[i], s5_b_glu[i])
        o_att = nsa_mixer(q_raw, kv_raw, ng_raw, b, q_norm[i], k_norm[i], cmp_pos[i], cmp_w1[i],
                          cmp_w2[i])
        wb = cast_bf16(w_branch, i)
        merged = mm_merge([o_gm, o_s5, o_att], [wb[0], wb[1], wb[2]], mg_raw)
        xf = mm_residual(merged, cast_bf16(w_out, i), xf)
        xf, h3 = moe_ffn(xf, i, norm2[i], router_group_w[i], router_group_b[i], router_expert_w[i],
                         router_expert_b[i], expert_w1, expert_w3, expert_w2, norm3[i])
        xf = mm_ple(h3, cast_bf16(w_ple_gate, i), p[i].reshape(m, -1).astype(BF16),
                    w_ple[i].astype(BF16), xf)
    return xf.reshape(b, l, d)
```
